```python
import numpy as np
import jax
import jax.numpy as jnp
from jax import lax

D_MODEL = 1024
BATCH = 16
SEQ = 2048
DEPTH = 1

HEAD_DIM = 64
PLE_DIM = 256
NORM_EPS = 1e-6
NEG_INF = -1e30

RW_HEADS = 8
RW_WIDTH = RW_HEADS * HEAD_DIM
RW_DECAY_LORA = 64
RW_AAA_LORA = 64
RW_GATE_LORA = 160
RW_LNX_EPS = 64e-5

NSA_Q_HEADS = 8
NSA_KV_HEADS = 2
NSA_GROUP = NSA_Q_HEADS // NSA_KV_HEADS
NSA_WIDTH = NSA_Q_HEADS * HEAD_DIM
NSA_KV_WIDTH = NSA_KV_HEADS * HEAD_DIM
CMP_BLOCK = 32
CMP_STRIDE = 16
CMP_HIDDEN = 128
SEL_BLOCK = 64
SEL_TOP = 16
SEL_QCHUNK = 64
SEL_FORCE_SCORE = 1e4
WINDOW = 512
WIN_QBLOCK = 128

D_FF = 2816
CONV_WIDTH = 3

IN_SIZES = (RW_WIDTH, RW_WIDTH, RW_WIDTH, NSA_WIDTH, NSA_KV_WIDTH, NSA_KV_WIDTH, NSA_KV_WIDTH, NSA_KV_WIDTH, NSA_KV_WIDTH, NSA_KV_WIDTH, 3 * NSA_Q_HEADS, D_MODEL, D_MODEL)
D_IN = 3 * RW_WIDTH + NSA_WIDTH + 6 * NSA_KV_WIDTH + 3 * NSA_Q_HEADS + 2 * D_MODEL

kernel_name = "hybrid_rwkv7_nsa_convglu_block"


def rmsnorm(x, g):
    xf = x.astype(jnp.float32)
    y = xf * lax.rsqrt(jnp.mean(xf * xf, axis=-1, keepdims=True) + NORM_EPS)
    return (y * g.astype(jnp.float32)).astype(x.dtype)


def token_shift(t):
    return jnp.pad(t, ((0, 0), (1, 0), (0, 0)))[:, :-1]


def masked_softmax(s, mask):
    s = jnp.where(mask, s.astype(jnp.float32), NEG_INF)
    return jnp.where(mask, jax.nn.softmax(s, axis=-1), 0.0)


def rwkv7_time_mix(u, r, k, v, mu_rkv, mu_wag, w0, w1, w2, a0, a1, a2, g1, g2, k_k, k_a, r_k, lnx_g, lnx_b):
    B, S, _ = u.shape
    H, N = RW_HEADS, HEAD_DIM
    f32 = jnp.float32
    r = r + (token_shift(r) - r) * mu_rkv[0]
    k = k + (token_shift(k) - k) * mu_rkv[1]
    v = v + (token_shift(v) - v) * mu_rkv[2]
    du = token_shift(u) - u
    xw = u + du * mu_wag[0]
    xa = u + du * mu_wag[1]
    xg = u + du * mu_wag[2]
    w = -jax.nn.softplus(-(w0 + jnp.tanh(xw @ w1) @ w2)) - 0.5
    decay = jnp.exp(-jnp.exp(w.astype(f32)))
    a = jax.nn.sigmoid(a0 + (xa @ a1) @ a2)
    g = jax.nn.sigmoid(xg @ g1) @ g2
    heads = lambda t: t.astype(f32).reshape(B, S, H, N)
    kk = heads(k * k_k)
    kk = kk / jnp.maximum(jnp.linalg.norm(kk, axis=-1, keepdims=True), 1e-12)
    k = k * (1.0 + (a - 1.0) * k_a)
    rh, kh, vh, ah, wh = heads(r), heads(k), heads(v), heads(a), heads(decay)

    def step(state, inp):
        r_t, w_t, k_t, v_t, kk_t, a_t = inp
        sa = jnp.einsum('bhvk,bhk->bhv', state, -kk_t)
        state = (state * w_t[:, :, None, :]
                 + sa[..., None] * (kk_t * a_t)[:, :, None, :]
                 + v_t[..., None] * k_t[:, :, None, :])
        return state, jnp.einsum('bhvk,bhk->bhv', state, r_t)

    tm = lambda t: jnp.moveaxis(t, 1, 0)
    state0 = jnp.zeros((B, H, N, N), f32)
    _, y = lax.scan(step, state0, (tm(rh), tm(wh), tm(kh), tm(vh), tm(kk), tm(ah)))
    y = jnp.moveaxis(y, 0, 1)
    mean = jnp.mean(y, axis=-1, keepdims=True)
    var = jnp.mean(jnp.square(y - mean), axis=-1, keepdims=True)
    y = ((y - mean) * lax.rsqrt(var + RW_LNX_EPS)).reshape(B, S, RW_WIDTH) * lnx_g + lnx_b
    bonus = jnp.sum(rh * kh * r_k, axis=-1, keepdims=True) * vh
    y = y + bonus.reshape(B, S, RW_WIDTH)
    return y.astype(u.dtype) * g


def nsa_attention(q, kc, vc, ks, vs, kw, vw, gates, cmp_pos, cmp_w1, cmp_w2):
    B, S, _ = q.shape
    Hk, G, Dh = NSA_KV_HEADS, NSA_GROUP, HEAD_DIM
    dt = q.dtype
    tpos = jnp.arange(S)
    q = (q * HEAD_DIM ** -0.5).reshape(B, S, Hk, G, Dh).transpose(0, 2, 3, 1, 4)
    kvh = lambda t: t.reshape(B, S, Hk, Dh).transpose(0, 2, 1, 3)
    kc, vc, ks, vs, kw, vw = kvh(kc), kvh(vc), kvh(ks), kvh(vs), kvh(kw), kvh(vw)

    n_cmp = (S - CMP_BLOCK) // CMP_STRIDE + 1
    cmp_start = np.arange(n_cmp) * CMP_STRIDE
    cmp_idx = cmp_start[:, None] + np.arange(CMP_BLOCK)

    def compress(t, pos, w1, w2):
        blk = t[:, :, cmp_idx] + pos
        return jax.nn.silu(blk.reshape(B, Hk, n_cmp, CMP_BLOCK * Dh) @ w1) @ w2

    kcb = compress(kc, cmp_pos[0], cmp_w1[0], cmp_w2[0])
    vcb = compress(vc, cmp_pos[1], cmp_w1[1], cmp_w2[1])
    cmp_mask = jnp.asarray((cmp_start + CMP_BLOCK - 1)[None, :] <= np.arange(S)[:, None])
    p_cmp = masked_softmax(jnp.einsum('bhgsd,bhcd->bhgsc', q, kcb), cmp_mask)
    o_cmp = jnp.einsum('bhgsc,bhcd->bhgsd', p_cmp.astype(dt), vcb)

    n_sel = S // SEL_BLOCK
    n_top = min(SEL_TOP, n_sel)
    sel_start = np.arange(n_sel) * SEL_BLOCK
    overlap = ((cmp_start[:, None] <= sel_start[None, :] + SEL_BLOCK - 1)
               & (cmp_start[:, None] + CMP_BLOCK - 1 >= sel_start[None, :])).astype(np.float32)
    importance = jnp.einsum('bhgsc,cj->bhsj', p_cmp, jnp.asarray(overlap))
    cur = (tpos // SEL_BLOCK)[:, None]
    blk = jnp.arange(n_sel)[None, :]
    forced = (blk == 0) | (blk == cur) | (blk == cur - 1)
    score = jnp.where(forced, SEL_FORCE_SCORE, jnp.where(blk <= cur, importance, -1.0))
    _, sel_idx = lax.top_k(score, n_top)
    ksb = ks.reshape(B, Hk, n_sel, SEL_BLOCK, Dh)
    vsb = vs.reshape(B, Hk, n_sel, SEL_BLOCK, Dh)
    bi = jnp.arange(B)[:, None, None, None]
    hi = jnp.arange(Hk)[None, :, None, None]
    n_q = S // SEL_QCHUNK

    def sel_chunk(args):
        qc, ic, pc = args
        kb = ksb[bi, hi, ic]
        vb = vsb[bi, hi, ic]
        s = jnp.einsum('bhgqd,bhqnld->bhgqnl', qc, kb)
        kpos = ic[..., None] * SEL_BLOCK + jnp.arange(SEL_BLOCK)
        mask = (kpos <= pc[:, None, None])[:, :, None]
        flat = n_top * SEL_BLOCK
        pr = masked_softmax(s.reshape(B, Hk, G, SEL_QCHUNK, flat),
                            mask.reshape(B, Hk, 1, SEL_QCHUNK, flat))
        return jnp.einsum('bhgqnl,bhqnld->bhgqd', pr.reshape(s.shape).astype(dt), vb)

    q_ch = q.reshape(B, Hk, G, n_q, SEL_QCHUNK, Dh).transpose(3, 0, 1, 2, 4, 5)
    i_ch = sel_idx.reshape(B, Hk, n_q, SEL_QCHUNK, n_top).transpose(2, 0, 1, 3, 4)
    p_ch = tpos.reshape(n_q, SEL_QCHUNK)
    o_sel = lax.map(sel_chunk, (q_ch, i_ch, p_ch))
    o_sel = o_sel.transpose(1, 2, 3, 0, 4, 5).reshape(B, Hk, G, S, Dh)

    n_b = S // WIN_QBLOCK
    n_w = WINDOW // WIN_QBLOCK
    span = (n_w + 1) * WIN_QBLOCK

    def band(t):
        tp = jnp.pad(t, ((0, 0), (0, 0), (WINDOW, 0), (0, 0))).reshape(B, Hk, n_b + n_w, WIN_QBLOCK, Dh)
        return jnp.concatenate([tp[:, :, j:j + n_b] for j in range(n_w + 1)], axis=3)

    kwb, vwb = band(kw), band(vw)
    qpos = np.arange(n_b)[:, None] * WIN_QBLOCK + np.arange(WIN_QBLOCK)
    kpos = np.arange(n_b)[:, None] * WIN_QBLOCK - WINDOW + np.arange(span)
    dist = qpos[:, :, None] - kpos[:, None, :]
    win_mask = jnp.asarray((dist >= 0) & (dist < WINDOW) & (kpos[:, None, :] >= 0))
    s_w = jnp.einsum('bhgnqd,bhnkd->bhgnqk', q.reshape(B, Hk, G, n_b, WIN_QBLOCK, Dh), kwb)
    p_w = masked_softmax(s_w, win_mask)
    o_win = jnp.einsum('bhgnqk,bhnkd->bhgnqd', p_w.astype(dt), vwb).reshape(B, Hk, G, S, Dh)

    gt = jax.nn.sigmoid(gates.reshape(B, S, Hk, G, 3).transpose(0, 2, 3, 1, 4))
    o = gt[..., 0:1] * o_cmp + gt[..., 1:2] * o_sel + gt[..., 2:3] * o_win
    return o.transpose(0, 3, 1, 2, 4).reshape(B, S, NSA_WIDTH)


def conv_glu_ffn(u, w_up, conv_w, conv_b, w_down):
    S = u.shape[1]
    up = u @ w_up
    upp = jnp.pad(up, ((0, 0), (CONV_WIDTH - 1, 0), (0, 0)))
    hc = conv_b + sum(conv_w[j] * upp[:, j:j + S] for j in range(CONV_WIDTH))
    a, b = jnp.split(hc, 2, axis=-1)
    return (jax.nn.silu(a) * b) @ w_down


def setup_inputs(seed: int = 0) -> dict:
    key = jax.random.key(seed)
    ks = iter(list(jax.random.split(key, 40)))
    nrm = lambda shape, scale: scale * jax.random.normal(next(ks), shape, jnp.float32)
    uni = lambda shape, lo, hi: jax.random.uniform(next(ks), shape, jnp.float32, lo, hi)
    L, D, Da = DEPTH, D_MODEL, RW_WIDTH
    return {
        'x': nrm((BATCH, SEQ, D), 1.0),
        'p': nrm((L, BATCH, SEQ, PLE_DIM), 1.0),
        'ln1_g': 1.0 + nrm((L, D), 0.02),
        'w_in': nrm((L, D, D_IN), D ** -0.5),
        'rw_mu_rkv': uni((L, 3, Da), 0.0, 1.0),
        'rw_mu_wag': uni((L, 3, D), 0.0, 1.0),
        'rw_w0': uni((L, Da), -6.0, 1.0),
        'rw_w1': nrm((L, D, RW_DECAY_LORA), D ** -0.5),
        'rw_w2': nrm((L, RW_DECAY_LORA, Da), 0.5 * RW_DECAY_LORA ** -0.5),
        'rw_a0': nrm((L, Da), 0.5),
        'rw_a1': nrm((L, D, RW_AAA_LORA), D ** -0.5),
        'rw_a2': nrm((L, RW_AAA_LORA, Da), 0.5 * RW_AAA_LORA ** -0.5),
        'rw_g1': nrm((L, D, RW_GATE_LORA), D ** -0.5),
        'rw_g2': nrm((L, RW_GATE_LORA, Da), RW_GATE_LORA ** -0.5),
        'rw_k_k': 0.85 + nrm((L, Da), 0.02),
        'rw_k_a': 1.0 + nrm((L, Da), 0.02),
        'rw_r_k': nrm((L, RW_HEADS, HEAD_DIM), 0.1),
        'rw_lnx_g': 1.0 + nrm((L, Da), 0.02),
        'rw_lnx_b': nrm((L, Da), 0.01),
        'nsa_cmp_pos': nrm((L, 2, CMP_BLOCK, HEAD_DIM), 0.1),
        'nsa_cmp_w1': nrm((L, 2, CMP_BLOCK * HEAD_DIM, CMP_HIDDEN), (CMP_BLOCK * HEAD_DIM) ** -0.5),
        'nsa_cmp_w2': nrm((L, 2, CMP_HIDDEN, HEAD_DIM), CMP_HIDDEN ** -0.5),
        'w_out_a': nrm((L, Da, D), Da ** -0.5),
        'w_out_b': nrm((L, NSA_WIDTH, D), NSA_WIDTH ** -0.5),
        'w_out': nrm((L, D, D), D ** -0.5),
        'ln2_g': 1.0 + nrm((L, D), 0.02),
        'w_up': nrm((L, D, 2 * D_FF), D ** -0.5),
        'conv_w': nrm((L, CONV_WIDTH, 2 * D_FF), CONV_WIDTH ** -0.5),
        'conv_b': nrm((L, 2 * D_FF), 0.01),
        'w_down': nrm((L, D_FF, D), D_FF ** -0.5),
        'ln3_g': 1.0 + nrm((L, D), 0.02),
        'w_ple_gate': nrm((L, D, D), D ** -0.5),
        'w_ple_proj': nrm((L, PLE_DIM, D), PLE_DIM ** -0.5),
        'ln_f_g': 1.0 + nrm((D,), 0.02),
    }


def reference(x, p, ln1_g, w_in, rw_mu_rkv, rw_mu_wag, rw_w0, rw_w1, rw_w2, rw_a0, rw_a1, rw_a2,
              rw_g1, rw_g2, rw_k_k, rw_k_a, rw_r_k, rw_lnx_g, rw_lnx_b, nsa_cmp_pos, nsa_cmp_w1,
              nsa_cmp_w2, w_out_a, w_out_b, w_out, ln2_g, w_up, conv_w, conv_b, w_down, ln3_g,
              w_ple_gate, w_ple_proj, ln_f_g):
    splits = [int(s) for s in np.cumsum(IN_SIZES)[:-1]]
    h = x
    for i in range(DEPTH):
        u = rmsnorm(h, ln1_g[i])
        (r, k, v, q, kc, vc, ks, vs, kw, vw, nsa_g, gate_a, gate_b) = jnp.split(u @ w_in[i], splits, axis=-1)
        y_a = rwkv7_time_mix(u, r, k, v, rw_mu_rkv[i], rw_mu_wag[i], rw_w0[i], rw_w1[i], rw_w2[i],
                             rw_a0[i], rw_a1[i], rw_a2[i], rw_g1[i], rw_g2[i], rw_k_k[i], rw_k_a[i],
                             rw_r_k[i], rw_lnx_g[i], rw_lnx_b[i]) @ w_out_a[i]
        y_b = nsa_attention(q, kc, vc, ks, vs, kw, vw, nsa_g, nsa_cmp_pos[i], nsa_cmp_w1[i],
                            nsa_cmp_w2[i]) @ w_out_b[i]
        h = h + (jax.nn.sigmoid(gate_a) * y_a + jax.nn.sigmoid(gate_b) * y_b) @ w_out[i]
        h = h + conv_glu_ffn(rmsnorm(h, ln2_g[i]), w_up[i], conv_w[i], conv_b[i], w_down[i])
        h = h + jax.nn.sigmoid(rmsnorm(h, ln3_g[i]) @ w_ple_gate[i]) * (p[i] @ w_ple_proj[i])
    return rmsnorm(h, ln_f_g)
```

```python
import functools

import numpy as np
import jax
import jax.numpy as jnp
from jax import lax
from jax.experimental import pallas as pl
from jax.experimental.pallas import tpu as pltpu

F32 = jnp.float32
BF16 = jnp.bfloat16

HEAD_DIM = 64
NORM_EPS = 1e-6
NEG_INF = -1e30

RW_HEADS = 8
RW_WIDTH = RW_HEADS * HEAD_DIM
RW_DECAY_LORA = 64
RW_AAA_LORA = 64
RW_GATE_LORA = 160
RW_LNX_EPS = 64e-5
RW_CHUNK = 64
RW_GROUP = 4
RW_GROUP_W = RW_GROUP * HEAD_DIM
RW_GATE_PAD = 256

NSA_Q_HEADS = 8
NSA_KV_HEADS = 2
NSA_GROUP = NSA_Q_HEADS // NSA_KV_HEADS
NSA_WIDTH = NSA_Q_HEADS * HEAD_DIM
NSA_KV_WIDTH = NSA_KV_HEADS * HEAD_DIM
CMP_BLOCK = 32
CMP_STRIDE = 16
CMP_HIDDEN = 128
SEL_BLOCK = 64
SEL_TOP = 16
SEL_FORCE_SCORE = 1e4
WINDOW = 512

CONV_WIDTH = 3

RW_COLS = 3 * RW_WIDTH + 2 * 128 + 2 * RW_GATE_PAD
NSA_COLS = NSA_WIDTH + 6 * NSA_KV_WIDTH + 256
GATE_COLS = 2 * 1024

VMEM_LIMIT = 56 * 1024 * 1024


def _dot(a, b):
    return jnp.dot(a, b, preferred_element_type=F32)


def _dot_nt(a, b):
    return lax.dot_general(a, b, (((1,), (1,)), ((), ())), preferred_element_type=F32)


def _dot_tn(a, b):
    return lax.dot_general(a, b, (((0,), (0,)), ((), ())), preferred_element_type=F32)


def _split2(x):
    hi = x.astype(BF16)
    lo = (x - hi.astype(F32)).astype(BF16)
    return hi, lo


def _split3(x):
    hi = x.astype(BF16)
    r1 = x - hi.astype(F32)
    mid = r1.astype(BF16)
    lo = (r1 - mid.astype(F32)).astype(BF16)
    return hi, mid, lo


def _sigmoid(x):
    return 1.0 / (1.0 + jnp.exp(-x))


def _softplus(x):
    return jnp.maximum(x, 0.0) + jnp.log(1.0 + jnp.exp(-jnp.abs(x)))


def _proj_kernel(x_ref, g_ref, w_ref, rw_ref, nsa_ref, gate_ref, *, chunk):
    x = x_ref[...]
    ms = jnp.mean(x * x, axis=-1, keepdims=True)
    u = (x * lax.rsqrt(ms + NORM_EPS) * g_ref[...]).astype(BF16)
    col = 0
    for o_ref in (rw_ref, nsa_ref, gate_ref):
        width = o_ref.shape[-1]
        for c in range(0, width, chunk):
            o_ref[:, c:c + chunk] = _dot(u, w_ref[:, col + c:col + c + chunk]).astype(o_ref.dtype)
        col += width


def _proj_call(x2, g, w_all, tm=512, chunk=768):
    T, D = x2.shape
    n_all = w_all.shape[1]
    return pl.pallas_call(
        functools.partial(_proj_kernel, chunk=chunk),
        grid=(T // tm,),
        in_specs=[
            pl.BlockSpec((tm, D), lambda i: (i, 0)),
            pl.BlockSpec((1, D), lambda i: (0, 0)),
            pl.BlockSpec((D, n_all), lambda i: (0, 0)),
        ],
        out_specs=[
            pl.BlockSpec((tm, RW_COLS), lambda i: (i, 0)),
            pl.BlockSpec((tm, NSA_COLS), lambda i: (i, 0)),
            pl.BlockSpec((tm, GATE_COLS), lambda i: (i, 0)),
        ],
        out_shape=[
            jax.ShapeDtypeStruct((T, RW_COLS), BF16),
            jax.ShapeDtypeStruct((T, NSA_COLS), BF16),
            jax.ShapeDtypeStruct((T, GATE_COLS), BF16),
        ],
        compiler_params=pltpu.CompilerParams(
            dimension_semantics=("arbitrary",), vmem_limit_bytes=VMEM_LIMIT),
        name="proj",
    )(x2, g, w_all)


def _rwkv_kernel(x_ref, mu_ref, vec_ref, w2a2_ref, g2_ref, o_ref, state_ref, prev_ref):
    C = RW_CHUNK
    GW = RW_GROUP_W
    t_idx = pl.program_id(1)

    @pl.when(t_idx == 0)
    def _():
        state_ref[...] = jnp.zeros_like(state_ref)
        prev_ref[...] = jnp.zeros_like(prev_ref)

    x = x_ref[0].astype(F32)
    row = lax.broadcasted_iota(jnp.int32, (C, 1), 0)
    xs = jnp.where(row == 0, prev_ref[0:1, :], pltpu.roll(x, 1, axis=0))
    prev_ref[0:1, :] = x[C - 1:C, :]

    W = RW_WIDTH
    mu = mu_ref[...]
    w0, a0, k_k, k_a, r_k, lnx_g, lnx_b = (vec_ref[i:i + 1, :] for i in range(7))

    def lerp(j):
        cur = x[:, j * W:(j + 1) * W]
        return cur + (xs[:, j * W:(j + 1) * W] - cur) * mu[j:j + 1, :]

    r, k, v = lerp(0), lerp(1), lerp(2)
    o = 3 * W
    pre_a = x[:, o:o + 128] + xs[:, o + 128:o + 256]
    lane = lax.broadcasted_iota(jnp.int32, (C, 128), 1)
    h_a = jnp.where(lane < RW_DECAY_LORA, jnp.tanh(pre_a), pre_a)
    lwa = _dot(h_a.astype(BF16), w2a2_ref[...])
    w = -_softplus(-(w0 + lwa[:, :W])) - 0.5
    ld = -jnp.exp(w)
    a = _sigmoid(a0 + lwa[:, W:])
    o += 256
    pre_g = x[:, o:o + RW_GATE_PAD] + xs[:, o + RW_GATE_PAD:o + 2 * RW_GATE_PAD]
    g = _dot(_sigmoid(pre_g).astype(BF16), g2_ref[...])

    gr = lax.broadcasted_iota(jnp.int32, (GW, GW), 0) // HEAD_DIM
    gc = lax.broadcasted_iota(jnp.int32, (GW, GW), 1) // HEAD_DIM
    blk = gr == gc
    ones_bd = jnp.where(blk, 1.0, 0.0).astype(BF16)

    def headsum(z):
        hi, lo = _split2(z)
        lhs = jnp.concatenate([hi[:, :GW], hi[:, GW:], lo[:, :GW], lo[:, GW:]], axis=0)
        s = _dot(lhs, ones_bd)
        s = s[:2 * C] + s[2 * C:]
        return jnp.concatenate([s[:C], s[C:]], axis=1)

    kkr = k * k_k
    kk = kkr / jnp.maximum(jnp.sqrt(headsum(kkr * kkr)), 1e-12)
    k2 = k * (1.0 + (a - 1.0) * k_a)
    b = kk * a

    tr = lax.broadcasted_iota(jnp.int32, (C, C), 0)
    tc = lax.broadcasted_iota(jnp.int32, (C, C), 1)
    tri = jnp.where(tr >= tc, 1.0, 0.0).astype(BF16)
    l_inc = _dot(tri, jnp.concatenate(_split3(ld), axis=1))
    l_inc = l_inc[:, :W] + l_inc[:, W:2 * W] + l_inc[:, 2 * W:]

    t_n = lax.broadcasted_iota(jnp.int32, (C, GW), 0)
    s_n = lax.broadcasted_iota(jnp.int32, (C, GW), 1) % HEAD_DIM
    strict = t_n > s_n
    incl = t_n >= s_n
    eye_n = jnp.where(t_n == s_n, 1.0, 0.0)

    def bd(z):
        z4 = jnp.concatenate([z] * RW_GROUP, axis=0)
        return jnp.where(blk, z4, 0.0).astype(BF16)

    ys = []
    for gi in range(RW_WIDTH // GW):
        sl = slice(gi * GW, (gi + 1) * GW)
        lg = l_inc[:, sl]
        ldg = ld[:, sl]
        lc = lg[C - 1:C, :]
        e_neg = jnp.exp(-lg)
        e_tail = jnp.exp(lc - lg)
        r_h = r[:, sl] * jnp.exp(lg)
        a_h = -kk[:, sl] * jnp.exp(lg - ldg)
        b_h = b[:, sl] * e_neg
        k_h = k2[:, sl] * e_neg
        b_t = b[:, sl] * e_tail
        k_t = k2[:, sl] * e_tail
        vg = v[:, sl]

        ar = jnp.concatenate([a_h, r_h], axis=0).astype(BF16)
        m1 = _dot_nt(ar, bd(b_h))
        m2 = _dot_nt(ar, bd(k_h))
        m_ab = jnp.where(strict, m1[:C], 0.0)
        m_rb = jnp.where(incl, m1[C:], 0.0)
        m_ak = jnp.where(strict, m2[:C], 0.0)
        m_rk = jnp.where(incl, m2[C:], 0.0)

        tinv = eye_n + m_ab
        p = _dot(m_ab.astype(BF16), bd(m_ab))
        power = 2
        while 2 * power < C:
            tp = _dot(jnp.concatenate([tinv, p], axis=0).astype(BF16), bd(p))
            tinv = tinv + tp[:C]
            p = tp[C:]
            power *= 2
        tinv = tinv + _dot(tinv.astype(BF16), bd(p))

        s_bd = state_ref[gi]
        s_bf = s_bd.astype(BF16)
        bd_v = bd(vg)
        xz = _dot_nt(a_h.astype(BF16), s_bf) + _dot(m_ak.astype(BF16), bd_v)
        u = _dot(tinv.astype(BF16), bd(xz))
        y = (_dot_nt(r_h.astype(BF16), s_bf)
             + _dot(jnp.concatenate([m_rb, m_rk], axis=1).astype(BF16),
                    jnp.concatenate([bd(u), bd_v], axis=0)))
        upd = _dot_tn(jnp.concatenate([u, vg], axis=0).astype(BF16),
                      jnp.concatenate([b_t, k_t], axis=0).astype(BF16))
        state_ref[gi] = s_bd * jnp.exp(lc) + jnp.where(blk, upd, 0.0)
        ys.append(y)

    y = jnp.concatenate(ys, axis=1)
    mean = headsum(y) * (1.0 / HEAD_DIM)
    yc = y - mean
    var = headsum(yc * yc) * (1.0 / HEAD_DIM)
    yn = yc * lax.rsqrt(var + RW_LNX_EPS) * lnx_g + lnx_b
    yn = yn + headsum(r * k2 * r_k) * v
    o_ref[0] = (yn * g).astype(o_ref.dtype)


def _rwkv_call(rw3, mu, vecs, w2a2, g2p):
    B, S, _ = rw3.shape
    C = RW_CHUNK
    n_groups = RW_WIDTH // RW_GROUP_W
    return pl.pallas_call(
        _rwkv_kernel,
        grid=(B, S // C),
        in_specs=[
            pl.BlockSpec((1, C, RW_COLS), lambda b, t: (b, t, 0)),
            pl.BlockSpec(mu.shape, lambda b, t: (0, 0)),
            pl.BlockSpec(vecs.shape, lambda b, t: (0, 0)),
            pl.BlockSpec(w2a2.shape, lambda b, t: (0, 0)),
            pl.BlockSpec(g2p.shape, lambda b, t: (0, 0)),
        ],
        out_specs=pl.BlockSpec((1, C, RW_WIDTH), lambda b, t: (b, t, 0)),
        out_shape=jax.ShapeDtypeStruct((B, S, RW_WIDTH), BF16),
        scratch_shapes=[
            pltpu.VMEM((n_groups, RW_GROUP_W, RW_GROUP_W), F32),
            pltpu.VMEM((8, RW_COLS), F32),
        ],
        compiler_params=pltpu.CompilerParams(
            dimension_semantics=("arbitrary", "arbitrary"), vmem_limit_bytes=VMEM_LIMIT),
        name="rwkv",
    )(rw3, mu, vecs, w2a2, g2p)


def _compress_kernel(kv_ref, wab_ref, pos_ref, w1_ref, w2_ref, o_ref):
    n_half = kv_ref.shape[2]
    for j in range(2):
        pab = _dot(kv_ref[0, j], wab_ref[j])
        half = NSA_KV_HEADS * CMP_HIDDEN
        pa, pb = pab[:, :half], pab[:, half:]
        pb = pltpu.roll(pb, n_half - 1, axis=0)
        pos_term = _dot(pos_ref[j], w1_ref[j])[0:1]
        hid = pa + pb + jnp.concatenate([pos_term] * NSA_KV_HEADS, axis=1)
        act = (hid * _sigmoid(hid)).astype(BF16)
        for hk in range(NSA_KV_HEADS):
            o_ref[0, j, hk] = _dot(act[:, hk * CMP_HIDDEN:(hk + 1) * CMP_HIDDEN], w2_ref[j])


def _compress_call(kv2, wab, pos8, w1, w2):
    B, _, n_half, width = kv2.shape
    full = lambda a: pl.BlockSpec(a.shape, lambda b: (0,) * a.ndim)
    return pl.pallas_call(
        _compress_kernel,
        grid=(B,),
        in_specs=[pl.BlockSpec((1, 2, n_half, width), lambda b: (b, 0, 0, 0)),
                  full(wab), full(pos8), full(w1), full(w2)],
        out_specs=pl.BlockSpec((1, 2, NSA_KV_HEADS, n_half, HEAD_DIM), lambda b: (b, 0, 0, 0, 0)),
        out_shape=jax.ShapeDtypeStruct((B, 2, NSA_KV_HEADS, n_half, HEAD_DIM), F32),
        compiler_params=pltpu.CompilerParams(
            dimension_semantics=("arbitrary",), vmem_limit_bytes=VMEM_LIMIT),
        name="nsa_compress",
    )(kv2, wab, pos8, w1, w2)


NSA_TQ = 128
NSA_KEY_STEP = 512
Q_OFF = 0
KV_OFF = NSA_WIDTH
NSA_GATE_OFF = NSA_WIDTH + 6 * NSA_KV_WIDTH


def _attend(qs, k, v, valid):
    s = jnp.where(valid, _dot_nt(qs, k), NEG_INF)
    m = jnp.max(s, axis=-1, keepdims=True)
    e = jnp.where(valid, jnp.exp(s - m), 0.0)
    l = jnp.sum(e, axis=-1, keepdims=True)
    o = _dot(e.astype(BF16), v)
    return o / jnp.where(l > 0.0, l, 1.0)


def _nsa_kernel(x_ref, ks_ref, vs_ref, kw_ref, vw_ref, cmp_ref, ovt_ref, exp_ref, gexp_ref,
                o_ref, osel_ref, *, seq):
    tq = NSA_TQ
    G = NSA_GROUP
    rows = G * tq
    n_half = cmp_ref.shape[3]
    n_cmp = n_half - 1
    n_sel = seq // SEL_BLOCK
    n_top = min(SEL_TOP, n_sel)
    q0 = pl.program_id(1) * tq

    x = x_ref[0]
    gates = _sigmoid(_dot(x[:, NSA_GATE_OFF:], gexp_ref[...]))

    tpos_r = q0 + lax.broadcasted_iota(jnp.int32, (rows, 1), 0) % tq

    span = min(WINDOW + tq, seq)
    kstart = pl.multiple_of(jnp.minimum(jnp.maximum(q0 - WINDOW, 0), seq - span), tq)

    for hk in range(NSA_KV_HEADS):
        qh = x[:, hk * G * HEAD_DIM:(hk + 1) * G * HEAD_DIM] * jnp.asarray(HEAD_DIM ** -0.5, BF16)
        qs = jnp.concatenate([qh[:, g * HEAD_DIM:(g + 1) * HEAD_DIM] for g in range(G)], axis=0)
        hsl = slice(hk * HEAD_DIM, (hk + 1) * HEAD_DIM)

        kcb = cmp_ref[0, 0, hk].astype(BF16)
        vcb = cmp_ref[0, 1, hk].astype(BF16)
        cidx = lax.broadcasted_iota(jnp.int32, (rows, n_half), 1)
        cvalid = (cidx * CMP_STRIDE + (CMP_BLOCK - 1) <= tpos_r) & (cidx < n_cmp)
        s = jnp.where(cvalid, _dot_nt(qs, kcb), NEG_INF)
        m = jnp.max(s, axis=-1, keepdims=True)
        e = jnp.where(cvalid, jnp.exp(s - m), 0.0)
        l = jnp.sum(e, axis=-1, keepdims=True)
        p_c = e / jnp.where(l > 0.0, l, 1.0)
        o_c = _dot(p_c.astype(BF16), vcb)

        psum = p_c[0:tq]
        for g in range(1, G):
            psum = psum + p_c[g * tq:(g + 1) * tq]
        hi, lo = _split2(psum)
        imp2 = _dot_nt(ovt_ref[...], jnp.concatenate([hi, lo], axis=0))
        imp = imp2[:, :tq] + imp2[:, tq:]
        jblk = lax.broadcasted_iota(jnp.int32, (n_sel, tq), 0)
        cur = (q0 + lax.broadcasted_iota(jnp.int32, (n_sel, tq), 1)) // SEL_BLOCK
        forced = (jblk == 0) | (jblk == cur) | (jblk == cur - 1)
        score = jnp.where(forced, SEL_FORCE_SCORE, jnp.where(jblk <= cur, imp, -1.0))
        rank = jnp.zeros((n_sel, tq), F32)
        for j in range(n_sel):
            sj = score[j:j + 1, :]
            ahead = (sj > score) | ((sj == score) & (j < jblk))
            rank = rank + jnp.where(ahead, 1.0, 0.0)
        sel_t = jnp.where(rank < n_top, 1.0, 0.0).astype(BF16)

        n_steps = (q0 + tq + NSA_KEY_STEP - 1) // NSA_KEY_STEP
        for n in range(1, seq // NSA_KEY_STEP + 1):
            nk = n * NSA_KEY_STEP

            @pl.when(n_steps == n)
            def _(nk=nk):
                chosen = _dot_tn(sel_t, exp_ref[:, :nk])
                chosen = jnp.concatenate([chosen] * G, axis=0)
                kpos = lax.broadcasted_iota(jnp.int32, (rows, nk), 1)
                valid = (chosen > 0.5) & (kpos <= tpos_r)
                osel_ref[...] = _attend(qs, ks_ref[0, :nk, hsl], vs_ref[0, :nk, hsl], valid)

        o_s = osel_ref[...]

        kpos = kstart + lax.broadcasted_iota(jnp.int32, (rows, span), 1)
        dist = tpos_r - kpos
        wvalid = (dist >= 0) & (dist < WINDOW)
        o_w = _attend(qs, kw_ref[0, pl.ds(kstart, span), hsl], vw_ref[0, pl.ds(kstart, span), hsl], wvalid)

        nat = lambda o: jnp.concatenate([o[g * tq:(g + 1) * tq] for g in range(G)], axis=1)
        c0 = hk * G * HEAD_DIM
        c1 = c0 + G * HEAD_DIM
        out = (gates[:, c0:c1] * nat(o_c)
               + gates[:, NSA_WIDTH + c0:NSA_WIDTH + c1] * nat(o_s)
               + gates[:, 2 * NSA_WIDTH + c0:2 * NSA_WIDTH + c1] * nat(o_w))
        o_ref[0, :, c0:c1] = out.astype(o_ref.dtype)


def _nsa_call(nsa3, cmp, ovt, expand, gexp):
    B, S, _ = nsa3.shape
    tq = NSA_TQ
    kv_spec = lambda j: pl.BlockSpec((1, S, NSA_KV_WIDTH), lambda b, i, j=j: (b, 0, KV_OFF // NSA_KV_WIDTH + j))
    full = lambda a: pl.BlockSpec(a.shape, lambda b, i: (0,) * a.ndim)
    return pl.pallas_call(
        functools.partial(_nsa_kernel, seq=S),
        grid=(B, S // tq),
        in_specs=[pl.BlockSpec((1, tq, NSA_COLS), lambda b, i: (b, i, 0)),
                  kv_spec(2), kv_spec(3), kv_spec(4), kv_spec(5),
                  pl.BlockSpec((1,) + cmp.shape[1:], lambda b, i: (b, 0, 0, 0, 0)),
                  full(ovt), full(expand), full(gexp)],
        out_specs=pl.BlockSpec((1, tq, NSA_WIDTH), lambda b, i: (b, i, 0)),
        out_shape=jax.ShapeDtypeStruct((B, S, NSA_WIDTH), BF16),
        scratch_shapes=[pltpu.VMEM((NSA_GROUP * tq, HEAD_DIM), F32)],
        compiler_params=pltpu.CompilerParams(
            dimension_semantics=("arbitrary", "arbitrary"), vmem_limit_bytes=VMEM_LIMIT),
        name="nsa_attention",
    )(nsa3, nsa3, nsa3, nsa3, nsa3, cmp, ovt, expand, gexp)


def _nsa_constants(S):
    n_half = S // CMP_STRIDE
    n_cmp = (S - CMP_BLOCK) // CMP_STRIDE + 1
    n_sel = S // SEL_BLOCK
    cmp_start = np.arange(n_half) * CMP_STRIDE
    sel_start = np.arange(n_sel) * SEL_BLOCK
    overlap = ((cmp_start[:, None] <= sel_start[None, :] + SEL_BLOCK - 1)
               & (cmp_start[:, None] + CMP_BLOCK - 1 >= sel_start[None, :])
               & (np.arange(n_half)[:, None] < n_cmp)).astype(np.float32)
    expand = (np.arange(S)[None, :] // SEL_BLOCK == np.arange(n_sel)[:, None]).astype(np.float32)
    gexp = np.zeros((NSA_COLS - NSA_GATE_OFF, 3 * NSA_WIDTH), np.float32)
    for h in range(NSA_Q_HEADS):
        for j in range(3):
            gexp[h * 3 + j, j * NSA_WIDTH + h * HEAD_DIM:j * NSA_WIDTH + (h + 1) * HEAD_DIM] = 1.0
    return (jnp.asarray(overlap.T, BF16), jnp.asarray(expand, BF16), jnp.asarray(gexp, BF16))


def _nsa_from_proj(nsa3, P):
    B, S, _ = nsa3.shape
    n_half = S // CMP_STRIDE
    kc = nsa3[:, :, KV_OFF:KV_OFF + NSA_KV_WIDTH].reshape(B, n_half, CMP_STRIDE * NSA_KV_WIDTH)
    vc = nsa3[:, :, KV_OFF + NSA_KV_WIDTH:KV_OFF + 2 * NSA_KV_WIDTH].reshape(B, n_half, CMP_STRIDE * NSA_KV_WIDTH)
    kv2 = jnp.stack([kc, vc], axis=1)
    w1 = P['nsa_cmp_w1'][0]
    w1h = w1.reshape(2, 2, CMP_STRIDE, HEAD_DIM, CMP_HIDDEN)
    eye = jnp.eye(NSA_KV_HEADS, dtype=F32)
    wab = jnp.einsum('jaldn,hg->jlhdagn', w1h, eye).reshape(
        2, CMP_STRIDE * NSA_KV_WIDTH, 2 * NSA_KV_HEADS * CMP_HIDDEN).astype(BF16)
    pos8 = jnp.broadcast_to(P['nsa_cmp_pos'][0].reshape(2, 1, CMP_BLOCK * HEAD_DIM),
                            (2, 8, CMP_BLOCK * HEAD_DIM)).astype(BF16)
    cmp = _compress_call(kv2, wab, pos8, w1.astype(BF16), P['nsa_cmp_w2'][0].astype(BF16))
    ovt, expand, gexp = _nsa_constants(S)
    return _nsa_call(nsa3, cmp, ovt, expand, gexp)


def _merge_kernel(x_ref, ya_ref, yb_ref, gate_ref, wa_ref, wb_ref, wo_ref, o_ref):
    D = x_ref.shape[-1]
    ta = _dot(ya_ref[...], wa_ref[...])
    tb = _dot(yb_ref[...], wb_ref[...])
    ga = _sigmoid(gate_ref[:, :D].astype(F32))
    gb = _sigmoid(gate_ref[:, D:].astype(F32))
    mix = (ga * ta + gb * tb).astype(BF16)
    o_ref[...] = x_ref[...] + _dot(mix, wo_ref[...])


def _merge_call(x2, ya2, yb2, gates, wa, wb, wo, tm=512):
    T, D = x2.shape
    row = lambda w: pl.BlockSpec((tm, w), lambda i: (i, 0))
    full = lambda a: pl.BlockSpec(a.shape, lambda i: (0,) * a.ndim)
    return pl.pallas_call(
        _merge_kernel,
        grid=(T // tm,),
        in_specs=[row(D), row(ya2.shape[1]), row(yb2.shape[1]), row(gates.shape[1]),
                  full(wa), full(wb), full(wo)],
        out_specs=row(D),
        out_shape=jax.ShapeDtypeStruct((T, D), F32),
        compiler_params=pltpu.CompilerParams(
            dimension_semantics=("arbitrary",), vmem_limit_bytes=VMEM_LIMIT),
        name="merge",
    )(x2, ya2, yb2, gates, wa, wb, wo)


FFN_HALO = 8


def _rms(x, g):
    return x * lax.rsqrt(jnp.mean(x * x, axis=-1, keepdims=True) + NORM_EPS) * g


def _ffn_kernel(h_ref, halo_ref, p_ref, ln_ref, wup_ref, cw_ref, cb_ref, wdn_ref, wpg_ref, wpp_ref,
                o_ref, *, tiles_per_seq, fc):
    tm, D = h_ref.shape
    d_ff = wdn_ref.shape[0]
    h = h_ref[...]
    first = (pl.program_id(0) % tiles_per_seq) == 0
    halo = jnp.where(first, 0.0, halo_ref[...])
    ln2, ln3, lnf = ln_ref[0:1, :], ln_ref[1:2, :], ln_ref[2:3, :]
    u = jnp.concatenate([_rms(halo, ln2), _rms(h, ln2)], axis=0).astype(BF16)

    acc = jnp.zeros((tm, D), F32)
    for c in range(0, d_ff, fc):
        def conv(col):
            up = _dot(u, wup_ref[:, col:col + fc])
            out = cb_ref[:, col:col + fc]
            for j in range(CONV_WIDTH):
                lo = FFN_HALO - (CONV_WIDTH - 1) + j
                out = out + cw_ref[j:j + 1, col:col + fc] * up[lo:lo + tm]
            return out
        a = conv(c)
        b = conv(d_ff + c)
        act = (a * _sigmoid(a) * b).astype(BF16)
        acc = acc + _dot(act, wdn_ref[c:c + fc, :])
    h2 = h + acc
    gate = _sigmoid(_dot(_rms(h2, ln3).astype(BF16), wpg_ref[...]))
    h3 = h2 + gate * _dot(p_ref[...].astype(BF16), wpp_ref[...])
    o_ref[...] = _rms(h3, lnf)


def _ffn_call(h2d, p2d, lns, wup, cw, cb, wdn, wpg, wpp, seq, tm=256, fc=256):
    T, D = h2d.shape
    tiles_per_seq = seq // tm
    row = lambda w: pl.BlockSpec((tm, w), lambda i: (i, 0))
    full = lambda a: pl.BlockSpec(a.shape, lambda i: (0,) * a.ndim, pipeline_mode=pl.Buffered(1))
    halo = pl.BlockSpec((FFN_HALO, D), lambda i: (jnp.maximum(i * (tm // FFN_HALO) - 1, 0), 0))
    return pl.pallas_call(
        functools.partial(_ffn_kernel, tiles_per_seq=tiles_per_seq, fc=fc),
        grid=(T // tm,),
        in_specs=[row(D), halo, row(p2d.shape[1]), full(lns), full(wup), full(cw), full(cb),
                  full(wdn), full(wpg), full(wpp)],
        out_specs=row(D),
        out_shape=jax.ShapeDtypeStruct((T, D), F32),
        compiler_params=pltpu.CompilerParams(
            dimension_semantics=("arbitrary",), vmem_limit_bytes=VMEM_LIMIT),
        name="ffn",
    )(h2d, h2d, p2d, lns, wup, cw, cb, wdn, wpg, wpp)


def _prep_proj_weights(w_in, mu_wag, w1, a1, g1):
    D = w_in.shape[0]
    sizes = (RW_WIDTH, RW_WIDTH, RW_WIDTH, NSA_WIDTH) + (NSA_KV_WIDTH,) * 6 + (3 * NSA_Q_HEADS, D, D)
    offs = np.concatenate([[0], np.cumsum(sizes)])
    part = lambda i, j: w_in[:, offs[i]:offs[j]]
    mw, ma, mg = mu_wag[0][:, None], mu_wag[1][:, None], mu_wag[2][:, None]
    zg = jnp.zeros((D, RW_GATE_PAD - RW_GATE_LORA), F32)
    rw = jnp.concatenate([
        part(0, 3),
        (1.0 - mw) * w1, (1.0 - ma) * a1,
        mw * w1, ma * a1,
        (1.0 - mg) * g1, zg,
        mg * g1, zg], axis=1)
    nsa = jnp.concatenate([part(3, 11), jnp.zeros((D, 256 - 3 * NSA_Q_HEADS), F32)], axis=1)
    gates = part(11, 13)
    return jnp.concatenate([rw, nsa, gates], axis=1).astype(BF16)


def _prep_rwkv_weights(w2, a2, g2):
    z = jnp.zeros_like(w2)
    w2a2 = jnp.concatenate([jnp.concatenate([w2, z], axis=1),
                            jnp.concatenate([z, a2], axis=1)], axis=0).astype(BF16)
    g2p = jnp.concatenate([g2, jnp.zeros((RW_GATE_PAD - RW_GATE_LORA, RW_WIDTH), F32)],
                          axis=0).astype(BF16)
    return w2a2, g2p


def _rwkv_from_proj(rw3, P):
    w2a2, g2p = _prep_rwkv_weights(P['rw_w2'][0], P['rw_a2'][0], P['rw_g2'][0])
    vecs = jnp.stack([P['rw_w0'][0], P['rw_a0'][0], P['rw_k_k'][0], P['rw_k_a'][0],
                      P['rw_r_k'][0].reshape(-1), P['rw_lnx_g'][0], P['rw_lnx_b'][0],
                      jnp.zeros((RW_WIDTH,), F32)], axis=0)
    return _rwkv_call(rw3, P['rw_mu_rkv'][0], vecs, w2a2, g2p)


def kernel(x, p, ln1_g, w_in, rw_mu_rkv, rw_mu_wag, rw_w0, rw_w1, rw_w2, rw_a0, rw_a1, rw_a2, rw_g1, rw_g2, rw_k_k, rw_k_a, rw_r_k, rw_lnx_g, rw_lnx_b, nsa_cmp_pos, nsa_cmp_w1, nsa_cmp_w2, w_out_a, w_out_b, w_out, ln2_g, w_up, conv_w, conv_b, w_down, ln3_g, w_ple_gate, w_ple_proj, ln_f_g):
    B, S, D = x.shape
    T = B * S
    assert w_in.shape[0] == 1, "single-layer block"
    P = dict(rw_mu_rkv=rw_mu_rkv, rw_w0=rw_w0, rw_w2=rw_w2, rw_a0=rw_a0, rw_a2=rw_a2, rw_g2=rw_g2,
             rw_k_k=rw_k_k, rw_k_a=rw_k_a, rw_r_k=rw_r_k, rw_lnx_g=rw_lnx_g, rw_lnx_b=rw_lnx_b,
             nsa_cmp_pos=nsa_cmp_pos, nsa_cmp_w1=nsa_cmp_w1, nsa_cmp_w2=nsa_cmp_w2)
    h = x.reshape(T, D)
    w_all = _prep_proj_weights(w_in[0], rw_mu_wag[0], rw_w1[0], rw_a1[0], rw_g1[0])
    rw, nsa, gates = _proj_call(h, ln1_g[0][None], w_all)
    ya = _rwkv_from_proj(rw.reshape(B, S, RW_COLS), P)
    yb = _nsa_from_proj(nsa.reshape(B, S, NSA_COLS), P)
    h1 = _merge_call(h, ya.reshape(T, RW_WIDTH), yb.reshape(T, NSA_WIDTH), gates,
                     w_out_a[0].astype(BF16), w_out_b[0].astype(BF16), w_out[0].astype(BF16))
    lns = jnp.stack([ln2_g[0], ln3_g[0], ln_f_g], axis=0)
    out = _ffn_call(h1, p[0].reshape(T, -1), lns, w_up[0].astype(BF16), conv_w[0], conv_b[0][None],
                    w_down[0].astype(BF16), w_ple_gate[0].astype(BF16), w_ple_proj[0].astype(BF16), S)
    return out.reshape(B, S, D)
```

```python
import functools

import numpy as np
import jax
import jax.numpy as jnp
from jax import lax
from jax.experimental import pallas as pl
from jax.experimental.pallas import tpu as pltpu

F32 = jnp.float32
BF16 = jnp.bfloat16

HEAD_DIM = 64
NORM_EPS = 1e-6
NEG_INF = -1e30

RW_HEADS = 8
RW_WIDTH = RW_HEADS * HEAD_DIM
RW_DECAY_LORA = 64
RW_AAA_LORA = 64
RW_GATE_LORA = 160
RW_LNX_EPS = 64e-5
RW_CHUNK = 64
RW_GROUP = 4
RW_GROUP_W = RW_GROUP * HEAD_DIM
RW_GATE_PAD = 256

NSA_Q_HEADS = 8
NSA_KV_HEADS = 2
NSA_GROUP = NSA_Q_HEADS // NSA_KV_HEADS
NSA_WIDTH = NSA_Q_HEADS * HEAD_DIM
NSA_KV_WIDTH = NSA_KV_HEADS * HEAD_DIM
CMP_BLOCK = 32
CMP_STRIDE = 16
CMP_HIDDEN = 128
SEL_BLOCK = 64
SEL_TOP = 16
SEL_FORCE_SCORE = 1e4
WINDOW = 512

CONV_WIDTH = 3

RW_COLS = 3 * RW_WIDTH + 2 * 128 + 2 * RW_GATE_PAD
NSA_COLS = NSA_WIDTH + 6 * NSA_KV_WIDTH + 256
GATE_COLS = 2 * 1024

VMEM_LIMIT = 56 * 1024 * 1024


def _dot(a, b):
    return jnp.dot(a, b, preferred_element_type=F32)


def _dot_nt(a, b):
    return lax.dot_general(a, b, (((1,), (1,)), ((), ())), preferred_element_type=F32)


def _dot_tn(a, b):
    return lax.dot_general(a, b, (((0,), (0,)), ((), ())), preferred_element_type=F32)


def _split2(x):
    hi = x.astype(BF16)
    lo = (x - hi.astype(F32)).astype(BF16)
    return hi, lo


def _split3(x):
    hi = x.astype(BF16)
    r1 = x - hi.astype(F32)
    mid = r1.astype(BF16)
    lo = (r1 - mid.astype(F32)).astype(BF16)
    return hi, mid, lo


def _sigmoid(x):
    return 1.0 / (1.0 + jnp.exp(-x))


def _softplus(x):
    return jnp.maximum(x, 0.0) + jnp.log(1.0 + jnp.exp(-jnp.abs(x)))


def _proj_kernel(x_ref, g_ref, w_ref, rw_ref, nsa_ref, gate_ref, *, chunk):
    x = x_ref[...]
    ms = jnp.mean(x * x, axis=-1, keepdims=True)
    u = (x * lax.rsqrt(ms + NORM_EPS) * g_ref[...]).astype(BF16)
    col = 0
    for o_ref in (rw_ref, nsa_ref, gate_ref):
        width = o_ref.shape[-1]
        for c in range(0, width, chunk):
            o_ref[:, c:c + chunk] = _dot(u, w_ref[:, col + c:col + c + chunk]).astype(o_ref.dtype)
        col += width


def _proj_call(x2, g, w_all, tm=512, chunk=768):
    T, D = x2.shape
    n_all = w_all.shape[1]
    return pl.pallas_call(
        functools.partial(_proj_kernel, chunk=chunk),
        grid=(T // tm,),
        in_specs=[
            pl.BlockSpec((tm, D), lambda i: (i, 0)),
            pl.BlockSpec((1, D), lambda i: (0, 0)),
            pl.BlockSpec((D, n_all), lambda i: (0, 0)),
        ],
        out_specs=[
            pl.BlockSpec((tm, RW_COLS), lambda i: (i, 0)),
            pl.BlockSpec((tm, NSA_COLS), lambda i: (i, 0)),
            pl.BlockSpec((tm, GATE_COLS), lambda i: (i, 0)),
        ],
        out_shape=[
            jax.ShapeDtypeStruct((T, RW_COLS), BF16),
            jax.ShapeDtypeStruct((T, NSA_COLS), BF16),
            jax.ShapeDtypeStruct((T, GATE_COLS), BF16),
        ],
        compiler_params=pltpu.CompilerParams(
            dimension_semantics=("arbitrary",), vmem_limit_bytes=VMEM_LIMIT),
        name="proj",
    )(x2, g, w_all)


def _rwkv_kernel(x_ref, mu_ref, vec_ref, w2a2_ref, g2_ref, o_ref, state_ref, prev_ref):
    C = RW_CHUNK
    GW = RW_GROUP_W
    W = RW_WIDTH
    NB = x_ref.shape[0]
    R = NB * C
    t_idx = pl.program_id(1)

    @pl.when(t_idx == 0)
    def _():
        state_ref[...] = jnp.zeros_like(state_ref)
        prev_ref[...] = jnp.zeros_like(prev_ref)

    x = x_ref[...].reshape(R, RW_COLS).astype(F32)
    rolled = pltpu.roll(x, 1, axis=0)
    row8 = lax.broadcasted_iota(jnp.int32, (8, 1), 0)
    pieces = []
    for bi in range(NB):
        pieces.append(jnp.where(row8 == 0, prev_ref[bi, 0:1, :], rolled[bi * C:bi * C + 8]))
        pieces.append(rolled[bi * C + 8:(bi + 1) * C])
        prev_ref[bi, 0:1, :] = x[(bi + 1) * C - 1:(bi + 1) * C, :]
    xs = jnp.concatenate(pieces, axis=0)

    mu = mu_ref[...]
    w0, a0, k_k, k_a, r_k, lnx_g, lnx_b = (vec_ref[i:i + 1, :] for i in range(7))

    def lerp(j):
        cur = x[:, j * W:(j + 1) * W]
        return cur + (xs[:, j * W:(j + 1) * W] - cur) * mu[j:j + 1, :]

    r, k, v = lerp(0), lerp(1), lerp(2)
    o = 3 * W
    pre_a = x[:, o:o + 128] + xs[:, o + 128:o + 256]
    lane = lax.broadcasted_iota(jnp.int32, (R, 128), 1)
    h_a = jnp.where(lane < RW_DECAY_LORA, jnp.tanh(pre_a), pre_a)
    lwa = _dot(h_a.astype(BF16), w2a2_ref[...])
    w = -_softplus(-(w0 + lwa[:, :W])) - 0.5
    ld = -jnp.exp(w)
    a = _sigmoid(a0 + lwa[:, W:])
    o += 256
    pre_g = x[:, o:o + RW_GATE_PAD] + xs[:, o + RW_GATE_PAD:o + 2 * RW_GATE_PAD]
    g = _dot(_sigmoid(pre_g).astype(BF16), g2_ref[...])

    gr = lax.broadcasted_iota(jnp.int32, (GW, GW), 0) // HEAD_DIM
    gc = lax.broadcasted_iota(jnp.int32, (GW, GW), 1) // HEAD_DIM
    blk = gr == gc
    ones_bd = jnp.where(blk, 1.0, 0.0).astype(BF16)

    def headsums(zs):
        parts = []
        for z in zs:
            hi, lo = _split2(z)
            parts += [hi[:, :GW], hi[:, GW:], lo[:, :GW], lo[:, GW:]]
        s = _dot(jnp.concatenate(parts, axis=0), ones_bd)
        outs = []
        for i in range(len(zs)):
            q = s[4 * R * i:4 * R * (i + 1)]
            q = q[:2 * R] + q[2 * R:]
            outs.append(jnp.concatenate([q[:R], q[R:]], axis=1))
        return outs

    kkr = k * k_k
    k2 = k * (1.0 + (a - 1.0) * k_a)
    kk_ss, bonus = headsums([kkr * kkr, r * k2 * r_k])
    kk = kkr / jnp.maximum(jnp.sqrt(kk_ss), 1e-12)
    b = kk * a

    tr = lax.broadcasted_iota(jnp.int32, (R, R), 0)
    tc = lax.broadcasted_iota(jnp.int32, (R, R), 1)
    tri = jnp.where((tr >= tc) & (tr // C == tc // C), 1.0, 0.0).astype(BF16)
    l_inc = _dot(tri, jnp.concatenate(_split3(ld), axis=1))
    l_inc = l_inc[:, :W] + l_inc[:, W:2 * W] + l_inc[:, 2 * W:]

    t_n = lax.broadcasted_iota(jnp.int32, (C, GW), 0)
    s_n = lax.broadcasted_iota(jnp.int32, (C, GW), 1) % HEAD_DIM
    strict = t_n > s_n
    incl = t_n >= s_n
    eye_n = jnp.where(t_n == s_n, 1.0, 0.0)

    def bd(z):
        z4 = jnp.concatenate([z] * RW_GROUP, axis=0)
        return jnp.where(blk, z4, 0.0).astype(BF16)

    e_neg = jnp.exp(-l_inc)
    r_hat = r * jnp.exp(l_inc)
    a_hat = -kk * jnp.exp(l_inc - ld)
    b_hat = b * e_neg
    k_hat = k2 * e_neg

    n_grp = W // GW
    chains = [(bi, gi) for bi in range(NB) for gi in range(n_grp)]
    cut = lambda z, c: z[c[0] * C:(c[0] + 1) * C, c[1] * GW:(c[1] + 1) * GW]
    each = lambda f, *lists: [f(*args) for args in zip(*lists)]

    a_h = [cut(a_hat, c).astype(BF16) for c in chains]
    r_h = [cut(r_hat, c).astype(BF16) for c in chains]
    ar = each(lambda x1, x2: jnp.concatenate([x1, x2], axis=0), a_h, r_h)
    m1 = each(_dot_nt, ar, [bd(cut(b_hat, c)) for c in chains])
    m2 = each(_dot_nt, ar, [bd(cut(k_hat, c)) for c in chains])
    m_ab = [jnp.where(strict, m[:C], 0.0) for m in m1]
    m_rb = [jnp.where(incl, m[C:], 0.0) for m in m1]
    m_ak = [jnp.where(strict, m[:C], 0.0) for m in m2]
    m_rk = [jnp.where(incl, m[C:], 0.0) for m in m2]

    tinv = [eye_n + m for m in m_ab]
    p = each(lambda m: _dot(m.astype(BF16), bd(m)), m_ab)
    power = 2
    while 2 * power < C:
        tp = each(lambda t, q: _dot(jnp.concatenate([t, q], axis=0).astype(BF16), bd(q)), tinv, p)
        tinv = each(lambda t, x1: t + x1[:C], tinv, tp)
        p = [x1[C:] for x1 in tp]
        power *= 2
    tinv = each(lambda t, q: t + _dot(t.astype(BF16), bd(q)), tinv, p)

    s_old = [state_ref[i * GW:(i + 1) * GW, :] for i in range(len(chains))]
    s_bf = [s.astype(BF16) for s in s_old]
    vg = [cut(v, c) for c in chains]
    bd_v = [bd(x1) for x1 in vg]
    xz = each(lambda x1, s, m, bv: _dot_nt(x1, s) + _dot(m.astype(BF16), bv), a_h, s_bf, m_ak, bd_v)
    u = each(lambda t, x1: _dot(t.astype(BF16), bd(x1)), tinv, xz)
    y = each(lambda x1, s, mb, mk, uu, bv:
             _dot_nt(x1, s) + _dot(jnp.concatenate([mb, mk], axis=1).astype(BF16),
                                   jnp.concatenate([bd(uu), bv], axis=0)),
             r_h, s_bf, m_rb, m_rk, u, bd_v)
    new_states = []
    for c, uu, vv, s in zip(chains, u, vg, s_old):
        lg = cut(l_inc, c)
        lc = lg[C - 1:C, :]
        e_tail = jnp.exp(lc - lg)
        upd = _dot_tn(jnp.concatenate([uu, vv], axis=0).astype(BF16),
                      jnp.concatenate([cut(b, c) * e_tail, cut(k2, c) * e_tail], axis=0).astype(BF16))
        new_states.append(s * jnp.exp(lc) + jnp.where(blk, upd, 0.0))
    state_ref[...] = jnp.concatenate(new_states, axis=0)
    y_rows = [jnp.concatenate(y[bi * n_grp:(bi + 1) * n_grp], axis=1) for bi in range(NB)]
    y = jnp.concatenate(y_rows, axis=0)
    mean = headsums([y])[0] * (1.0 / HEAD_DIM)
    yc = y - mean
    var = headsums([yc * yc])[0] * (1.0 / HEAD_DIM)
    yn = yc * lax.rsqrt(var + RW_LNX_EPS) * lnx_g + lnx_b
    yn = yn + bonus * v
    o_ref[...] = (yn * g).reshape(NB, C, W).astype(o_ref.dtype)


RW_SEQS_PER_STEP = 4


def _rwkv_call(rw3, mu, vecs, w2a2, g2p):
    B, S, _ = rw3.shape
    C = RW_CHUNK
    nb = RW_SEQS_PER_STEP if B % RW_SEQS_PER_STEP == 0 else 1
    n_groups = RW_WIDTH // RW_GROUP_W
    return pl.pallas_call(
        _rwkv_kernel,
        grid=(B // nb, S // C),
        in_specs=[
            pl.BlockSpec((nb, C, RW_COLS), lambda b, t: (b, t, 0)),
            pl.BlockSpec(mu.shape, lambda b, t: (0, 0)),
            pl.BlockSpec(vecs.shape, lambda b, t: (0, 0)),
            pl.BlockSpec(w2a2.shape, lambda b, t: (0, 0)),
            pl.BlockSpec(g2p.shape, lambda b, t: (0, 0)),
        ],
        out_specs=pl.BlockSpec((nb, C, RW_WIDTH), lambda b, t: (b, t, 0)),
        out_shape=jax.ShapeDtypeStruct((B, S, RW_WIDTH), BF16),
        scratch_shapes=[
            pltpu.VMEM((nb * n_groups * RW_GROUP_W, RW_GROUP_W), F32),
            pltpu.VMEM((nb, 8, RW_COLS), F32),
        ],
        compiler_params=pltpu.CompilerParams(
            dimension_semantics=("arbitrary", "arbitrary"), vmem_limit_bytes=VMEM_LIMIT),
        name="rwkv",
    )(rw3, mu, vecs, w2a2, g2p)


def _compress_kernel(kv_ref, wab_ref, pos_ref, w1_ref, w2_ref, o_ref):
    n_half = kv_ref.shape[2]
    for j in range(2):
        pab = _dot(kv_ref[0, j], wab_ref[j])
        half = NSA_KV_HEADS * CMP_HIDDEN
        pa, pb = pab[:, :half], pab[:, half:]
        pb = pltpu.roll(pb, n_half - 1, axis=0)
        pos_term = _dot(pos_ref[j], w1_ref[j])[0:1]
        hid = pa + pb + jnp.concatenate([pos_term] * NSA_KV_HEADS, axis=1)
        act = (hid * _sigmoid(hid)).astype(BF16)
        for hk in range(NSA_KV_HEADS):
            o_ref[0, j, hk] = _dot(act[:, hk * CMP_HIDDEN:(hk + 1) * CMP_HIDDEN], w2_ref[j])


def _compress_call(kv2, wab, pos8, w1, w2):
    B, _, n_half, width = kv2.shape
    full = lambda a: pl.BlockSpec(a.shape, lambda b: (0,) * a.ndim)
    return pl.pallas_call(
        _compress_kernel,
        grid=(B,),
        in_specs=[pl.BlockSpec((1, 2, n_half, width), lambda b: (b, 0, 0, 0)),
                  full(wab), full(pos8), full(w1), full(w2)],
        out_specs=pl.BlockSpec((1, 2, NSA_KV_HEADS, n_half, HEAD_DIM), lambda b: (b, 0, 0, 0, 0)),
        out_shape=jax.ShapeDtypeStruct((B, 2, NSA_KV_HEADS, n_half, HEAD_DIM), F32),
        compiler_params=pltpu.CompilerParams(
            dimension_semantics=("arbitrary",), vmem_limit_bytes=VMEM_LIMIT),
        name="nsa_compress",
    )(kv2, wab, pos8, w1, w2)


NSA_TQ = 128
NSA_KEY_STEP = 512
Q_OFF = 0
KV_OFF = NSA_WIDTH
NSA_GATE_OFF = NSA_WIDTH + 6 * NSA_KV_WIDTH


def _attend(qs, k, v, valid):
    s = jnp.where(valid, _dot_nt(qs, k), NEG_INF)
    m = jnp.max(s, axis=-1, keepdims=True)
    e = jnp.where(valid, jnp.exp(s - m), 0.0)
    l = jnp.sum(e, axis=-1, keepdims=True)
    o = _dot(e.astype(BF16), v)
    return o / jnp.where(l > 0.0, l, 1.0)


def _nsa_kernel(x_ref, ks_ref, vs_ref, kw_ref, vw_ref, cmp_ref, ovt_ref, exp_ref, gexp_ref,
                o_ref, osel_ref, *, seq):
    tq = NSA_TQ
    G = NSA_GROUP
    rows = G * tq
    n_half = cmp_ref.shape[3]
    n_cmp = n_half - 1
    n_sel = seq // SEL_BLOCK
    n_top = min(SEL_TOP, n_sel)
    q0 = pl.program_id(1) * tq

    x = x_ref[0]
    gates = _sigmoid(_dot(x[:, NSA_GATE_OFF:], gexp_ref[...]))

    tpos_r = q0 + lax.broadcasted_iota(jnp.int32, (rows, 1), 0) % tq

    span = min(WINDOW + tq, seq)
    kstart = pl.multiple_of(jnp.minimum(jnp.maximum(q0 - WINDOW, 0), seq - span), tq)

    for hk in range(NSA_KV_HEADS):
        qh = x[:, hk * G * HEAD_DIM:(hk + 1) * G * HEAD_DIM] * jnp.asarray(HEAD_DIM ** -0.5, BF16)
        qs = jnp.concatenate([qh[:, g * HEAD_DIM:(g + 1) * HEAD_DIM] for g in range(G)], axis=0)
        hsl = slice(hk * HEAD_DIM, (hk + 1) * HEAD_DIM)

        kcb = cmp_ref[0, 0, hk].astype(BF16)
        vcb = cmp_ref[0, 1, hk].astype(BF16)
        cidx = lax.broadcasted_iota(jnp.int32, (rows, n_half), 1)
        cvalid = (cidx * CMP_STRIDE + (CMP_BLOCK - 1) <= tpos_r) & (cidx < n_cmp)
        s = jnp.where(cvalid, _dot_nt(qs, kcb), NEG_INF)
        m = jnp.max(s, axis=-1, keepdims=True)
        e = jnp.where(cvalid, jnp.exp(s - m), 0.0)
        l = jnp.sum(e, axis=-1, keepdims=True)
        p_c = e / jnp.where(l > 0.0, l, 1.0)
        o_c = _dot(p_c.astype(BF16), vcb)

        psum = p_c[0:tq]
        for g in range(1, G):
            psum = psum + p_c[g * tq:(g + 1) * tq]
        hi, lo = _split2(psum)
        imp2 = _dot_nt(ovt_ref[...], jnp.concatenate([hi, lo], axis=0))
        imp = imp2[:, :tq] + imp2[:, tq:]
        jblk = lax.broadcasted_iota(jnp.int32, (n_sel, tq), 0)
        cur = (q0 + lax.broadcasted_iota(jnp.int32, (n_sel, tq), 1)) // SEL_BLOCK
        forced = (jblk == 0) | (jblk == cur) | (jblk == cur - 1)
        score = jnp.where(forced, SEL_FORCE_SCORE, jnp.where(jblk <= cur, imp, -1.0))
        rank = jnp.zeros((n_sel, tq), F32)
        for j in range(n_sel):
            sj = score[j:j + 1, :]
            ahead = (sj > score) | ((sj == score) & (j < jblk))
            rank = rank + jnp.where(ahead, 1.0, 0.0)
        sel_t = jnp.where(rank < n_top, 1.0, 0.0).astype(BF16)

        n_steps = (q0 + tq + NSA_KEY_STEP - 1) // NSA_KEY_STEP
        for n in range(1, seq // NSA_KEY_STEP + 1):
            nk = n * NSA_KEY_STEP

            @pl.when(n_steps == n)
            def _(nk=nk):
                chosen = _dot_tn(sel_t, exp_ref[:, :nk])
                chosen = jnp.concatenate([chosen] * G, axis=0)
                kpos = lax.broadcasted_iota(jnp.int32, (rows, nk), 1)
                valid = (chosen > 0.5) & (kpos <= tpos_r)
                osel_ref[...] = _attend(qs, ks_ref[0, :nk, hsl], vs_ref[0, :nk, hsl], valid)

        o_s = osel_ref[...]

        kpos = kstart + lax.broadcasted_iota(jnp.int32, (rows, span), 1)
        dist = tpos_r - kpos
        wvalid = (dist >= 0) & (dist < WINDOW)
        o_w = _attend(qs, kw_ref[0, pl.ds(kstart, span), hsl], vw_ref[0, pl.ds(kstart, span), hsl], wvalid)

        nat = lambda o: jnp.concatenate([o[g * tq:(g + 1) * tq] for g in range(G)], axis=1)
        c0 = hk * G * HEAD_DIM
        c1 = c0 + G * HEAD_DIM
        out = (gates[:, c0:c1] * nat(o_c)
               + gates[:, NSA_WIDTH + c0:NSA_WIDTH + c1] * nat(o_s)
               + gates[:, 2 * NSA_WIDTH + c0:2 * NSA_WIDTH + c1] * nat(o_w))
        o_ref[0, :, c0:c1] = out.astype(o_ref.dtype)


def _nsa_call(nsa3, cmp, ovt, expand, gexp):
    B, S, _ = nsa3.shape
    tq = NSA_TQ
    kv_spec = lambda j: pl.BlockSpec((1, S, NSA_KV_WIDTH), lambda b, i, j=j: (b, 0, KV_OFF // NSA_KV_WIDTH + j))
    full = lambda a: pl.BlockSpec(a.shape, lambda b, i: (0,) * a.ndim)
    return pl.pallas_call(
        functools.partial(_nsa_kernel, seq=S),
        grid=(B, S // tq),
        in_specs=[pl.BlockSpec((1, tq, NSA_COLS), lambda b, i: (b, i, 0)),
                  kv_spec(2), kv_spec(3), kv_spec(4), kv_spec(5),
                  pl.BlockSpec((1,) + cmp.shape[1:], lambda b, i: (b, 0, 0, 0, 0)),
                  full(ovt), full(expand), full(gexp)],
        out_specs=pl.BlockSpec((1, tq, NSA_WIDTH), lambda b, i: (b, i, 0)),
        out_shape=jax.ShapeDtypeStruct((B, S, NSA_WIDTH), BF16),
        scratch_shapes=[pltpu.VMEM((NSA_GROUP * tq, HEAD_DIM), F32)],
        compiler_params=pltpu.CompilerParams(
            dimension_semantics=("arbitrary", "arbitrary"), vmem_limit_bytes=VMEM_LIMIT),
        name="nsa_attention",
    )(nsa3, nsa3, nsa3, nsa3, nsa3, cmp, ovt, expand, gexp)


def _nsa_constants(S):
    n_half = S // CMP_STRIDE
    n_cmp = (S - CMP_BLOCK) // CMP_STRIDE + 1
    n_sel = S // SEL_BLOCK
    cmp_start = np.arange(n_half) * CMP_STRIDE
    sel_start = np.arange(n_sel) * SEL_BLOCK
    overlap = ((cmp_start[:, None] <= sel_start[None, :] + SEL_BLOCK - 1)
               & (cmp_start[:, None] + CMP_BLOCK - 1 >= sel_start[None, :])
               & (np.arange(n_half)[:, None] < n_cmp)).astype(np.float32)
    expand = (np.arange(S)[None, :] // SEL_BLOCK == np.arange(n_sel)[:, None]).astype(np.float32)
    gexp = np.zeros((NSA_COLS - NSA_GATE_OFF, 3 * NSA_WIDTH), np.float32)
    for h in range(NSA_Q_HEADS):
        for j in range(3):
            gexp[h * 3 + j, j * NSA_WIDTH + h * HEAD_DIM:j * NSA_WIDTH + (h + 1) * HEAD_DIM] = 1.0
    return (jnp.asarray(overlap.T, BF16), jnp.asarray(expand, BF16), jnp.asarray(gexp, BF16))


def _nsa_from_proj(nsa3, P):
    B, S, _ = nsa3.shape
    n_half = S // CMP_STRIDE
    kc = nsa3[:, :, KV_OFF:KV_OFF + NSA_KV_WIDTH].reshape(B, n_half, CMP_STRIDE * NSA_KV_WIDTH)
    vc = nsa3[:, :, KV_OFF + NSA_KV_WIDTH:KV_OFF + 2 * NSA_KV_WIDTH].reshape(B, n_half, CMP_STRIDE * NSA_KV_WIDTH)
    kv2 = jnp.stack([kc, vc], axis=1)
    w1 = P['nsa_cmp_w1'][0]
    w1h = w1.reshape(2, 2, CMP_STRIDE, HEAD_DIM, CMP_HIDDEN)
    eye = jnp.eye(NSA_KV_HEADS, dtype=F32)
    wab = jnp.einsum('jaldn,hg->jlhdagn', w1h, eye).reshape(
        2, CMP_STRIDE * NSA_KV_WIDTH, 2 * NSA_KV_HEADS * CMP_HIDDEN).astype(BF16)
    pos8 = jnp.broadcast_to(P['nsa_cmp_pos'][0].reshape(2, 1, CMP_BLOCK * HEAD_DIM),
                            (2, 8, CMP_BLOCK * HEAD_DIM)).astype(BF16)
    cmp = _compress_call(kv2, wab, pos8, w1.astype(BF16), P['nsa_cmp_w2'][0].astype(BF16))
    ovt, expand, gexp = _nsa_constants(S)
    return _nsa_call(nsa3, cmp, ovt, expand, gexp)


def _merge_kernel(x_ref, ya_ref, yb_ref, gate_ref, wa_ref, wb_ref, wo_ref, o_ref):
    D = x_ref.shape[-1]
    ta = _dot(ya_ref[...], wa_ref[...])
    tb = _dot(yb_ref[...], wb_ref[...])
    ga = _sigmoid(gate_ref[:, :D].astype(F32))
    gb = _sigmoid(gate_ref[:, D:].astype(F32))
    mix = (ga * ta + gb * tb).astype(BF16)
    o_ref[...] = x_ref[...] + _dot(mix, wo_ref[...])


def _merge_call(x2, ya2, yb2, gates, wa, wb, wo, tm=512):
    T, D = x2.shape
    row = lambda w: pl.BlockSpec((tm, w), lambda i: (i, 0))
    full = lambda a: pl.BlockSpec(a.shape, lambda i: (0,) * a.ndim)
    return pl.pallas_call(
        _merge_kernel,
        grid=(T // tm,),
        in_specs=[row(D), row(ya2.shape[1]), row(yb2.shape[1]), row(gates.shape[1]),
                  full(wa), full(wb), full(wo)],
        out_specs=row(D),
        out_shape=jax.ShapeDtypeStruct((T, D), F32),
        compiler_params=pltpu.CompilerParams(
            dimension_semantics=("arbitrary",), vmem_limit_bytes=VMEM_LIMIT),
        name="merge",
    )(x2, ya2, yb2, gates, wa, wb, wo)


FFN_HALO = 8


def _rms(x, g):
    return x * lax.rsqrt(jnp.mean(x * x, axis=-1, keepdims=True) + NORM_EPS) * g


def _ffn_kernel(h_ref, halo_ref, p_ref, ln_ref, wup_ref, cw_ref, cb_ref, wdn_ref, wpg_ref, wpp_ref,
                o_ref, *, tiles_per_seq, fc):
    tm, D = h_ref.shape
    d_ff = wdn_ref.shape[0]
    h = h_ref[...]
    first = (pl.program_id(0) % tiles_per_seq) == 0
    halo = jnp.where(first, 0.0, halo_ref[...])
    ln2, ln3, lnf = ln_ref[0:1, :], ln_ref[1:2, :], ln_ref[2:3, :]
    u = jnp.concatenate([_rms(halo, ln2), _rms(h, ln2)], axis=0).astype(BF16)

    acc = jnp.zeros((tm, D), F32)
    for c in range(0, d_ff, fc):
        def conv(col):
            up = _dot(u, wup_ref[:, col:col + fc])
            out = cb_ref[:, col:col + fc]
            for j in range(CONV_WIDTH):
                lo = FFN_HALO - (CONV_WIDTH - 1) + j
                out = out + cw_ref[j:j + 1, col:col + fc] * up[lo:lo + tm]
            return out
        a = conv(c)
        b = conv(d_ff + c)
        act = (a * _sigmoid(a) * b).astype(BF16)
        acc = acc + _dot(act, wdn_ref[c:c + fc, :])
    h2 = h + acc
    gate = _sigmoid(_dot(_rms(h2, ln3).astype(BF16), wpg_ref[...]))
    h3 = h2 + gate * _dot(p_ref[...].astype(BF16), wpp_ref[...])
    o_ref[...] = _rms(h3, lnf)


def _ffn_call(h2d, p2d, lns, wup, cw, cb, wdn, wpg, wpp, seq, tm=256, fc=256):
    T, D = h2d.shape
    tiles_per_seq = seq // tm
    row = lambda w: pl.BlockSpec((tm, w), lambda i: (i, 0))
    full = lambda a: pl.BlockSpec(a.shape, lambda i: (0,) * a.ndim, pipeline_mode=pl.Buffered(1))
    halo = pl.BlockSpec((FFN_HALO, D), lambda i: (jnp.maximum(i * (tm // FFN_HALO) - 1, 0), 0))
    return pl.pallas_call(
        functools.partial(_ffn_kernel, tiles_per_seq=tiles_per_seq, fc=fc),
        grid=(T // tm,),
        in_specs=[row(D), halo, row(p2d.shape[1]), full(lns), full(wup), full(cw), full(cb),
                  full(wdn), full(wpg), full(wpp)],
        out_specs=row(D),
        out_shape=jax.ShapeDtypeStruct((T, D), F32),
        compiler_params=pltpu.CompilerParams(
            dimension_semantics=("arbitrary",), vmem_limit_bytes=VMEM_LIMIT),
        name="ffn",
    )(h2d, h2d, p2d, lns, wup, cw, cb, wdn, wpg, wpp)


def _prep_proj_weights(w_in, mu_wag, w1, a1, g1):
    D = w_in.shape[0]
    sizes = (RW_WIDTH, RW_WIDTH, RW_WIDTH, NSA_WIDTH) + (NSA_KV_WIDTH,) * 6 + (3 * NSA_Q_HEADS, D, D)
    offs = np.concatenate([[0], np.cumsum(sizes)])
    part = lambda i, j: w_in[:, offs[i]:offs[j]]
    mw, ma, mg = mu_wag[0][:, None], mu_wag[1][:, None], mu_wag[2][:, None]
    zg = jnp.zeros((D, RW_GATE_PAD - RW_GATE_LORA), F32)
    rw = jnp.concatenate([
        part(0, 3),
        (1.0 - mw) * w1, (1.0 - ma) * a1,
        mw * w1, ma * a1,
        (1.0 - mg) * g1, zg,
        mg * g1, zg], axis=1)
    nsa = jnp.concatenate([part(3, 11), jnp.zeros((D, 256 - 3 * NSA_Q_HEADS), F32)], axis=1)
    gates = part(11, 13)
    return jnp.concatenate([rw, nsa, gates], axis=1).astype(BF16)


def _prep_rwkv_weights(w2, a2, g2):
    z = jnp.zeros_like(w2)
    w2a2 = jnp.concatenate([jnp.concatenate([w2, z], axis=1),
                            jnp.concatenate([z, a2], axis=1)], axis=0).astype(BF16)
    g2p = jnp.concatenate([g2, jnp.zeros((RW_GATE_PAD - RW_GATE_LORA, RW_WIDTH), F32)],
                          axis=0).astype(BF16)
    return w2a2, g2p


def _rwkv_from_proj(rw3, P):
    w2a2, g2p = _prep_rwkv_weights(P['rw_w2'][0], P['rw_a2'][0], P['rw_g2'][0])
    vecs = jnp.stack([P['rw_w0'][0], P['rw_a0'][0], P['rw_k_k'][0], P['rw_k_a'][0],
                      P['rw_r_k'][0].reshape(-1), P['rw_lnx_g'][0], P['rw_lnx_b'][0],
                      jnp.zeros((RW_WIDTH,), F32)], axis=0)
    return _rwkv_call(rw3, P['rw_mu_rkv'][0], vecs, w2a2, g2p)


def kernel(x, p, ln1_g, w_in, rw_mu_rkv, rw_mu_wag, rw_w0, rw_w1, rw_w2, rw_a0, rw_a1, rw_a2, rw_g1, rw_g2, rw_k_k, rw_k_a, rw_r_k, rw_lnx_g, rw_lnx_b, nsa_cmp_pos, nsa_cmp_w1, nsa_cmp_w2, w_out_a, w_out_b, w_out, ln2_g, w_up, conv_w, conv_b, w_down, ln3_g, w_ple_gate, w_ple_proj, ln_f_g):
    B, S, D = x.shape
    T = B * S
    assert w_in.shape[0] == 1, "single-layer block"
    P = dict(rw_mu_rkv=rw_mu_rkv, rw_w0=rw_w0, rw_w2=rw_w2, rw_a0=rw_a0, rw_a2=rw_a2, rw_g2=rw_g2,
             rw_k_k=rw_k_k, rw_k_a=rw_k_a, rw_r_k=rw_r_k, rw_lnx_g=rw_lnx_g, rw_lnx_b=rw_lnx_b,
             nsa_cmp_pos=nsa_cmp_pos, nsa_cmp_w1=nsa_cmp_w1, nsa_cmp_w2=nsa_cmp_w2)
    h = x.reshape(T, D)
    w_all = _prep_proj_weights(w_in[0], rw_mu_wag[0], rw_w1[0], rw_a1[0], rw_g1[0])
    rw, nsa, gates = _proj_call(h, ln1_g[0][None], w_all)
    ya = _rwkv_from_proj(rw.reshape(B, S, RW_COLS), P)
    yb = _nsa_from_proj(nsa.reshape(B, S, NSA_COLS), P)
    h1 = _merge_call(h, ya.reshape(T, RW_WIDTH), yb.reshape(T, NSA_WIDTH), gates,
                     w_out_a[0].astype(BF16), w_out_b[0].astype(BF16), w_out[0].astype(BF16))
    lns = jnp.stack([ln2_g[0], ln3_g[0], ln_f_g], axis=0)
    out = _ffn_call(h1, p[0].reshape(T, -1), lns, w_up[0].astype(BF16), conv_w[0], conv_b[0][None],
                    w_down[0].astype(BF16), w_ple_gate[0].astype(BF16), w_ple_proj[0].astype(BF16), S)
    return out.reshape(B, S, D)
```

```python
import functools

import numpy as np
import jax
import jax.numpy as jnp
from jax import lax
from jax.experimental import pallas as pl
from jax.experimental.pallas import tpu as pltpu

F32 = jnp.float32
BF16 = jnp.bfloat16

HEAD_DIM = 64
NORM_EPS = 1e-6
NEG_INF = -1e30

RW_HEADS = 8
RW_WIDTH = RW_HEADS * HEAD_DIM
RW_DECAY_LORA = 64
RW_AAA_LORA = 64
RW_GATE_LORA = 160
RW_LNX_EPS = 64e-5
RW_CHUNK = 64
RW_GROUP = 4
RW_GROUP_W = RW_GROUP * HEAD_DIM
RW_GATE_PAD = 256

NSA_Q_HEADS = 8
NSA_KV_HEADS = 2
NSA_GROUP = NSA_Q_HEADS // NSA_KV_HEADS
NSA_WIDTH = NSA_Q_HEADS * HEAD_DIM
NSA_KV_WIDTH = NSA_KV_HEADS * HEAD_DIM
CMP_BLOCK = 32
CMP_STRIDE = 16
CMP_HIDDEN = 128
SEL_BLOCK = 64
SEL_TOP = 16
SEL_FORCE_SCORE = 1e4
WINDOW = 512

CONV_WIDTH = 3

RW_COLS = 3 * RW_WIDTH + 2 * 128 + 2 * RW_GATE_PAD
NSA_COLS = NSA_WIDTH + 6 * NSA_KV_WIDTH + 256
GATE_COLS = 2 * 1024

VMEM_LIMIT = 56 * 1024 * 1024


def _dot(a, b):
    return jnp.dot(a, b, preferred_element_type=F32)


def _dot_nt(a, b):
    return lax.dot_general(a, b, (((1,), (1,)), ((), ())), preferred_element_type=F32)


def _dot_tn(a, b):
    return lax.dot_general(a, b, (((0,), (0,)), ((), ())), preferred_element_type=F32)


def _split2(x):
    hi = x.astype(BF16)
    lo = (x - hi.astype(F32)).astype(BF16)
    return hi, lo


def _split3(x):
    hi = x.astype(BF16)
    r1 = x - hi.astype(F32)
    mid = r1.astype(BF16)
    lo = (r1 - mid.astype(F32)).astype(BF16)
    return hi, mid, lo


def _sigmoid(x):
    return 1.0 / (1.0 + jnp.exp(-x))


def _softplus(x):
    return jnp.maximum(x, 0.0) + jnp.log(1.0 + jnp.exp(-jnp.abs(x)))


def _proj_kernel(x_ref, g_ref, w_ref, rw_ref, nsa_ref, gate_ref, *, chunk):
    x = x_ref[...]
    ms = jnp.mean(x * x, axis=-1, keepdims=True)
    u = (x * lax.rsqrt(ms + NORM_EPS) * g_ref[...]).astype(BF16)
    col = 0
    for o_ref in (rw_ref, nsa_ref, gate_ref):
        width = o_ref.shape[-1]
        for c in range(0, width, chunk):
            o_ref[:, c:c + chunk] = _dot(u, w_ref[:, col + c:col + c + chunk]).astype(o_ref.dtype)
        col += width


def _proj_call(x2, g, w_all, tm=512, chunk=768):
    T, D = x2.shape
    n_all = w_all.shape[1]
    return pl.pallas_call(
        functools.partial(_proj_kernel, chunk=chunk),
        grid=(T // tm,),
        in_specs=[
            pl.BlockSpec((tm, D), lambda i: (i, 0)),
            pl.BlockSpec((1, D), lambda i: (0, 0)),
            pl.BlockSpec((D, n_all), lambda i: (0, 0)),
        ],
        out_specs=[
            pl.BlockSpec((tm, RW_COLS), lambda i: (i, 0)),
            pl.BlockSpec((tm, NSA_COLS), lambda i: (i, 0)),
            pl.BlockSpec((tm, GATE_COLS), lambda i: (i, 0)),
        ],
        out_shape=[
            jax.ShapeDtypeStruct((T, RW_COLS), BF16),
            jax.ShapeDtypeStruct((T, NSA_COLS), BF16),
            jax.ShapeDtypeStruct((T, GATE_COLS), BF16),
        ],
        compiler_params=pltpu.CompilerParams(
            dimension_semantics=("arbitrary",), vmem_limit_bytes=VMEM_LIMIT),
        name="proj",
    )(x2, g, w_all)


def _rwkv_kernel(x_ref, mu_ref, vec_ref, w2a2_ref, g2_ref, o_ref, state_ref, prev_ref):
    C = RW_CHUNK
    GW = RW_GROUP_W
    W = RW_WIDTH
    NB = x_ref.shape[0]
    R = NB * C
    t_idx = pl.program_id(1)

    @pl.when(t_idx == 0)
    def _():
        state_ref[...] = jnp.zeros_like(state_ref)
        prev_ref[...] = jnp.zeros_like(prev_ref)

    x = x_ref[...].reshape(R, RW_COLS).astype(F32)
    rolled = pltpu.roll(x, 1, axis=0)
    row8 = lax.broadcasted_iota(jnp.int32, (8, 1), 0)
    pieces = []
    for bi in range(NB):
        pieces.append(jnp.where(row8 == 0, prev_ref[bi, 0:1, :], rolled[bi * C:bi * C + 8]))
        pieces.append(rolled[bi * C + 8:(bi + 1) * C])
        prev_ref[bi, 0:1, :] = x[(bi + 1) * C - 1:(bi + 1) * C, :]
    xs = jnp.concatenate(pieces, axis=0)

    mu = mu_ref[...]
    w0, a0, k_k, k_a, r_k, lnx_g, lnx_b = (vec_ref[i:i + 1, :] for i in range(7))

    def lerp(j):
        cur = x[:, j * W:(j + 1) * W]
        return cur + (xs[:, j * W:(j + 1) * W] - cur) * mu[j:j + 1, :]

    r, k, v = lerp(0), lerp(1), lerp(2)
    o = 3 * W
    pre_a = x[:, o:o + 128] + xs[:, o + 128:o + 256]
    lane = lax.broadcasted_iota(jnp.int32, (R, 128), 1)
    h_a = jnp.where(lane < RW_DECAY_LORA, jnp.tanh(pre_a), pre_a)
    lwa = _dot(h_a.astype(BF16), w2a2_ref[...])
    w = -_softplus(-(w0 + lwa[:, :W])) - 0.5
    ld = -jnp.exp(w)
    a = _sigmoid(a0 + lwa[:, W:])
    o += 256
    pre_g = x[:, o:o + RW_GATE_PAD] + xs[:, o + RW_GATE_PAD:o + 2 * RW_GATE_PAD]
    g = _dot(_sigmoid(pre_g).astype(BF16), g2_ref[...])

    gr = lax.broadcasted_iota(jnp.int32, (GW, GW), 0) // HEAD_DIM
    gc = lax.broadcasted_iota(jnp.int32, (GW, GW), 1) // HEAD_DIM
    blk = gr == gc
    ones_bd = jnp.where(blk, 1.0, 0.0).astype(BF16)

    def headsums(zs):
        parts = []
        for z in zs:
            hi, lo = _split2(z)
            parts += [hi[:, :GW], hi[:, GW:], lo[:, :GW], lo[:, GW:]]
        s = _dot(jnp.concatenate(parts, axis=0), ones_bd)
        outs = []
        for i in range(len(zs)):
            q = s[4 * R * i:4 * R * (i + 1)]
            q = q[:2 * R] + q[2 * R:]
            outs.append(jnp.concatenate([q[:R], q[R:]], axis=1))
        return outs

    kkr = k * k_k
    k2 = k * (1.0 + (a - 1.0) * k_a)
    kk_ss, bonus = headsums([kkr * kkr, r * k2 * r_k])
    kk = kkr / jnp.maximum(jnp.sqrt(kk_ss), 1e-12)
    b = kk * a

    tr = lax.broadcasted_iota(jnp.int32, (R, R), 0)
    tc = lax.broadcasted_iota(jnp.int32, (R, R), 1)
    tri = jnp.where((tr >= tc) & (tr // C == tc // C), 1.0, 0.0).astype(BF16)
    l_inc = _dot(tri, jnp.concatenate(_split3(ld), axis=1))
    l_inc = l_inc[:, :W] + l_inc[:, W:2 * W] + l_inc[:, 2 * W:]

    t_n = lax.broadcasted_iota(jnp.int32, (C, GW), 0)
    s_n = lax.broadcasted_iota(jnp.int32, (C, GW), 1) % HEAD_DIM
    strict = t_n > s_n
    incl = t_n >= s_n
    eye_n = jnp.where(t_n == s_n, 1.0, 0.0)

    def bd(z):
        z4 = jnp.concatenate([z] * RW_GROUP, axis=0)
        return jnp.where(blk, z4, 0.0).astype(BF16)

    e_neg = jnp.exp(-l_inc)
    r_hat = r * jnp.exp(l_inc)
    a_hat = -kk * jnp.exp(l_inc - ld)
    b_hat = b * e_neg
    k_hat = k2 * e_neg

    n_grp = W // GW
    chains = [(bi, gi) for bi in range(NB) for gi in range(n_grp)]
    cut = lambda z, c: z[c[0] * C:(c[0] + 1) * C, c[1] * GW:(c[1] + 1) * GW]
    each = lambda f, *lists: [f(*args) for args in zip(*lists)]

    a_h = [cut(a_hat, c).astype(BF16) for c in chains]
    r_h = [cut(r_hat, c).astype(BF16) for c in chains]
    ar = each(lambda x1, x2: jnp.concatenate([x1, x2], axis=0), a_h, r_h)
    m1 = each(_dot_nt, ar, [bd(cut(b_hat, c)) for c in chains])
    m2 = each(_dot_nt, ar, [bd(cut(k_hat, c)) for c in chains])
    m_ab = [jnp.where(strict, m[:C], 0.0) for m in m1]
    m_rb = [jnp.where(incl, m[C:], 0.0) for m in m1]
    m_ak = [jnp.where(strict, m[:C], 0.0) for m in m2]
    m_rk = [jnp.where(incl, m[C:], 0.0) for m in m2]

    tinv = [eye_n + m for m in m_ab]
    p = each(lambda m: _dot(m.astype(BF16), bd(m)), m_ab)
    power = 2
    while 2 * power < C:
        tp = each(lambda t, q: _dot(jnp.concatenate([t, q], axis=0).astype(BF16), bd(q)), tinv, p)
        tinv = each(lambda t, x1: t + x1[:C], tinv, tp)
        p = [x1[C:] for x1 in tp]
        power *= 2
    tinv = each(lambda t, q: t + _dot(t.astype(BF16), bd(q)), tinv, p)

    s_old = [state_ref[i * GW:(i + 1) * GW, :] for i in range(len(chains))]
    s_bf = [s.astype(BF16) for s in s_old]
    vg = [cut(v, c) for c in chains]
    bd_v = [bd(x1) for x1 in vg]
    xz = each(lambda x1, s, m, bv: _dot_nt(x1, s) + _dot(m.astype(BF16), bv), a_h, s_bf, m_ak, bd_v)
    u = each(lambda t, x1: _dot(t.astype(BF16), bd(x1)), tinv, xz)
    y = each(lambda x1, s, mb, mk, uu, bv:
             _dot_nt(x1, s) + _dot(jnp.concatenate([mb, mk], axis=1).astype(BF16),
                                   jnp.concatenate([bd(uu), bv], axis=0)),
             r_h, s_bf, m_rb, m_rk, u, bd_v)
    new_states = []
    for c, uu, vv, s in zip(chains, u, vg, s_old):
        lg = cut(l_inc, c)
        lc = lg[C - 1:C, :]
        e_tail = jnp.exp(lc - lg)
        upd = _dot_tn(jnp.concatenate([uu, vv], axis=0).astype(BF16),
                      jnp.concatenate([cut(b, c) * e_tail, cut(k2, c) * e_tail], axis=0).astype(BF16))
        new_states.append(s * jnp.exp(lc) + jnp.where(blk, upd, 0.0))
    state_ref[...] = jnp.concatenate(new_states, axis=0)
    y_rows = [jnp.concatenate(y[bi * n_grp:(bi + 1) * n_grp], axis=1) for bi in range(NB)]
    y = jnp.concatenate(y_rows, axis=0)
    mean = headsums([y])[0] * (1.0 / HEAD_DIM)
    yc = y - mean
    var = headsums([yc * yc])[0] * (1.0 / HEAD_DIM)
    yn = yc * lax.rsqrt(var + RW_LNX_EPS) * lnx_g + lnx_b
    yn = yn + bonus * v
    o_ref[...] = (yn * g).reshape(NB, C, W).astype(o_ref.dtype)


RW_SEQS_PER_STEP = 4


def _rwkv_call(rw3, mu, vecs, w2a2, g2p):
    B, S, _ = rw3.shape
    C = RW_CHUNK
    nb = RW_SEQS_PER_STEP if B % RW_SEQS_PER_STEP == 0 else 1
    n_groups = RW_WIDTH // RW_GROUP_W
    return pl.pallas_call(
        _rwkv_kernel,
        grid=(B // nb, S // C),
        in_specs=[
            pl.BlockSpec((nb, C, RW_COLS), lambda b, t: (b, t, 0)),
            pl.BlockSpec(mu.shape, lambda b, t: (0, 0)),
            pl.BlockSpec(vecs.shape, lambda b, t: (0, 0)),
            pl.BlockSpec(w2a2.shape, lambda b, t: (0, 0)),
            pl.BlockSpec(g2p.shape, lambda b, t: (0, 0)),
        ],
        out_specs=pl.BlockSpec((nb, C, RW_WIDTH), lambda b, t: (b, t, 0)),
        out_shape=jax.ShapeDtypeStruct((B, S, RW_WIDTH), BF16),
        scratch_shapes=[
            pltpu.VMEM((nb * n_groups * RW_GROUP_W, RW_GROUP_W), F32),
            pltpu.VMEM((nb, 8, RW_COLS), F32),
        ],
        compiler_params=pltpu.CompilerParams(
            dimension_semantics=("arbitrary", "arbitrary"), vmem_limit_bytes=VMEM_LIMIT),
        name="rwkv",
    )(rw3, mu, vecs, w2a2, g2p)


def _compress_kernel(kv_ref, wab_ref, pos_ref, w1_ref, w2_ref, w2t_ref, kcb_ref, vcbt_ref):
    n_half = kv_ref.shape[2]
    for j in range(2):
        pab = _dot(kv_ref[0, j], wab_ref[j])
        half = NSA_KV_HEADS * CMP_HIDDEN
        pa, pb = pab[:, :half], pab[:, half:]
        pb = pltpu.roll(pb, n_half - 1, axis=0)
        pos_term = _dot(pos_ref[j], w1_ref[j])[0:1]
        hid = pa + pb + jnp.concatenate([pos_term] * NSA_KV_HEADS, axis=1)
        act = (hid * _sigmoid(hid)).astype(BF16)
        for hk in range(NSA_KV_HEADS):
            a_h = act[:, hk * CMP_HIDDEN:(hk + 1) * CMP_HIDDEN]
            if j == 0:
                kcb_ref[0, hk] = _dot(a_h, w2_ref[j]).astype(kcb_ref.dtype)
            else:
                vcbt_ref[0, hk] = _dot_nt(w2t_ref[j], a_h).astype(vcbt_ref.dtype)


def _compress_call(kv2, wab, pos8, w1, w2, w2t):
    B, _, n_half, width = kv2.shape
    full = lambda a: pl.BlockSpec(a.shape, lambda b: (0,) * a.ndim)
    return pl.pallas_call(
        _compress_kernel,
        grid=(B,),
        in_specs=[pl.BlockSpec((1, 2, n_half, width), lambda b: (b, 0, 0, 0)),
                  full(wab), full(pos8), full(w1), full(w2), full(w2t)],
        out_specs=[pl.BlockSpec((1, NSA_KV_HEADS, n_half, HEAD_DIM), lambda b: (b, 0, 0, 0)),
                   pl.BlockSpec((1, NSA_KV_HEADS, HEAD_DIM, n_half), lambda b: (b, 0, 0, 0))],
        out_shape=[jax.ShapeDtypeStruct((B, NSA_KV_HEADS, n_half, HEAD_DIM), BF16),
                   jax.ShapeDtypeStruct((B, NSA_KV_HEADS, HEAD_DIM, n_half), BF16)],
        compiler_params=pltpu.CompilerParams(
            dimension_semantics=("arbitrary",), vmem_limit_bytes=VMEM_LIMIT),
        name="nsa_compress",
    )(kv2, wab, pos8, w1, w2, w2t)


KV_OFF = NSA_WIDTH
NSA_GATE_OFF = NSA_WIDTH + 6 * NSA_KV_WIDTH

NSA2_TQ = 256
NSA_AUG = 128
NSA_BIAS_ROWS = 32


def _nsa2_kernel(qt_ref, gt_ref, ks_ref, vst_ref, kw_ref, vwt_ref, kcb_ref, vcbt_ref, ovt_ref,
                 tri_ref, wbias_ref, o_ref, *, seq):
    tq = NSA2_TQ
    G = NSA_GROUP
    R = G * tq
    n_half = kcb_ref.shape[2]
    n_cmp = n_half - 1
    n_sel = seq // SEL_BLOCK
    n_top = min(SEL_TOP, n_sel)
    n_wchunks = WINDOW // tq + 1
    step = pl.program_id(1)
    q0 = step * tq

    t_lane = q0 + lax.broadcasted_iota(jnp.int32, (1, R), 1) % tq
    gates = _sigmoid(gt_ref[0].astype(F32))
    row32 = lax.broadcasted_iota(jnp.int32, (NSA_AUG - HEAD_DIM - NSA_BIAS_ROWS, tq), 0)
    pad_rows = jnp.where(row32 == 0, NEG_INF, 0.0).astype(BF16)

    hrow = lambda hk, g: slice((hk * G + g) * HEAD_DIM, (hk * G + g + 1) * HEAD_DIM)
    o_cmp, qaug = {}, {}
    for hk in range(NSA_KV_HEADS):
        q64 = jnp.concatenate([qt_ref[0, hrow(hk, g), :] for g in range(G)], axis=1)
        q64 = q64 * jnp.asarray(HEAD_DIM ** -0.5, BF16)

        cidx = lax.broadcasted_iota(jnp.int32, (n_half, R), 0)
        cvalid = (cidx * CMP_STRIDE + (CMP_BLOCK - 1) <= t_lane) & (cidx < n_cmp)
        s = jnp.where(cvalid, _dot(kcb_ref[0, hk], q64), NEG_INF)
        m = jnp.max(s, axis=0, keepdims=True)
        e = jnp.where(cvalid, jnp.exp(s - m), 0.0)
        l = jnp.sum(e, axis=0, keepdims=True)
        p_c = e / jnp.where(l > 0.0, l, 1.0)
        o_c = _dot(vcbt_ref[0, hk], p_c.astype(BF16))

        psum = p_c[:, 0:tq]
        for g in range(1, G):
            psum = psum + p_c[:, g * tq:(g + 1) * tq]
        hi, lo = _split2(psum)
        imp2 = _dot(ovt_ref[...], jnp.concatenate([hi, lo], axis=1))
        imp = imp2[:, :tq] + imp2[:, tq:]
        jblk = lax.broadcasted_iota(jnp.int32, (n_sel, tq), 0)
        cur = (q0 + lax.broadcasted_iota(jnp.int32, (n_sel, tq), 1)) // SEL_BLOCK
        forced = (jblk == 0) | (jblk == cur) | (jblk == cur - 1)
        score = jnp.where(forced, SEL_FORCE_SCORE, jnp.where(jblk <= cur, imp, -1.0))
        rank = jnp.zeros((n_sel, tq), F32)
        for j in range(n_sel):
            sj = score[j:j + 1, :]
            ahead = (sj > score) | ((sj == score) & (j < jblk))
            rank = rank + jnp.where(ahead, 1.0, 0.0)
        sel_bias = jnp.where(rank < n_top, 0.0, NEG_INF).astype(BF16)
        for g in range(G):
            o_cmp[hk, g] = o_c[:, g * tq:(g + 1) * tq]
            qaug[hk, g] = jnp.concatenate([q64[:, g * tq:(g + 1) * tq], sel_bias, pad_rows], axis=0)

    chains = [(hk, g) for hk in range(NSA_KV_HEADS) for g in range(G)]

    def softmax_pv(s_list, vt_of, carry=None):
        m_blk = [jnp.max(s, axis=0, keepdims=True) for s in s_list]
        if carry is None:
            m_new = m_blk
        else:
            m_new = [jnp.maximum(c[0], mb) for c, mb in zip(carry, m_blk)]
        p = [jnp.exp(s - mn) for s, mn in zip(s_list, m_new)]
        l_blk = [jnp.sum(x, axis=0, keepdims=True) for x in p]
        pv = [_dot(vt_of(c), x.astype(BF16)) for c, x in zip(chains, p)]
        if carry is None:
            return [(mn, lb, a) for mn, lb, a in zip(m_new, l_blk, pv)]
        alpha = [jnp.exp(c[0] - mn) for c, mn in zip(carry, m_new)]
        return [(mn, c[1] * al + lb, c[2] * al + a)
                for c, mn, al, lb, a in zip(carry, m_new, alpha, l_blk, pv)]

    kw_rows = [kw_ref[0, hk, pl.ds(step, n_wchunks)].reshape(n_wchunks * tq, NSA_AUG)
               for hk in range(NSA_KV_HEADS)]
    vw_cols = [jnp.concatenate([vwt_ref[0, hk, step + w] for w in range(n_wchunks)], axis=1)
               for hk in range(NSA_KV_HEADS)]
    wbias = wbias_ref[...]
    s_win = [_dot(kw_rows[hk], qaug[hk, g]) + wbias for hk, g in chains]
    win = softmax_pv(s_win, lambda c: vw_cols[c[0]])

    tri = tri_ref[...]
    s_diag = [_dot(ks_ref[0, hk, step], qaug[hk, g]) + tri for hk, g in chains]
    carry = softmax_pv(s_diag, lambda c: vst_ref[0, c[0], step])

    def body(j, flat):
        carry = [tuple(flat[3 * i:3 * i + 3]) for i in range(len(chains))]
        s_j = [_dot(ks_ref[0, hk, j], qaug[hk, g]) for hk, g in chains]
        new = softmax_pv(s_j, lambda c: vst_ref[0, c[0], j], carry)
        return tuple(x for c in new for x in c)

    flat = lax.fori_loop(0, step, body, tuple(x for c in carry for x in c))
    sel = [tuple(flat[3 * i:3 * i + 3]) for i in range(len(chains))]

    for i, (hk, g) in enumerate(chains):
        gate = lambda j: gates[(hk * G + g) * 3 + j:(hk * G + g) * 3 + j + 1, :]
        out = (gate(0) * o_cmp[hk, g] + gate(1) * (sel[i][2] / sel[i][1])
               + gate(2) * (win[i][2] / win[i][1]))
        o_ref[0, hrow(hk, g), :] = out.astype(o_ref.dtype)


def _nsa2_call(qt, gt, ks5, vst5, kw5, vwt5, kcb, vcbt, ovt, tri, wbias, seq):
    B = qt.shape[0]
    tq = NSA2_TQ
    per_b = lambda a: pl.BlockSpec((1,) + a.shape[1:], lambda b, i: (b,) + (0,) * (a.ndim - 1))
    full = lambda a: pl.BlockSpec(a.shape, lambda b, i: (0,) * a.ndim)
    return pl.pallas_call(
        functools.partial(_nsa2_kernel, seq=seq),
        grid=(B, seq // tq),
        in_specs=[pl.BlockSpec((1, NSA_WIDTH, tq), lambda b, i: (b, 0, i)),
                  pl.BlockSpec((1, gt.shape[1], tq), lambda b, i: (b, 0, i)),
                  per_b(ks5), per_b(vst5), per_b(kw5), per_b(vwt5), per_b(kcb), per_b(vcbt),
                  full(ovt), full(tri), full(wbias)],
        out_specs=pl.BlockSpec((1, NSA_WIDTH, tq), lambda b, i: (b, 0, i)),
        out_shape=jax.ShapeDtypeStruct((B, NSA_WIDTH, seq), BF16),
        compiler_params=pltpu.CompilerParams(
            dimension_semantics=("arbitrary", "arbitrary"), vmem_limit_bytes=VMEM_LIMIT),
        name="nsa_attention",
    )(qt, gt, ks5, vst5, kw5, vwt5, kcb, vcbt, ovt, tri, wbias)


def _nsa2_from_proj(nsa3, P):
    B, S, _ = nsa3.shape
    tq = NSA2_TQ
    Hk, Dh = NSA_KV_HEADS, HEAD_DIM
    assert S % tq == 0 and WINDOW % tq == 0 and S // SEL_BLOCK == NSA_BIAS_ROWS
    n_half = S // CMP_STRIDE
    kc = nsa3[:, :, KV_OFF:KV_OFF + NSA_KV_WIDTH].reshape(B, n_half, CMP_STRIDE * NSA_KV_WIDTH)
    vc = nsa3[:, :, KV_OFF + NSA_KV_WIDTH:KV_OFF + 2 * NSA_KV_WIDTH].reshape(B, n_half, CMP_STRIDE * NSA_KV_WIDTH)
    kv2 = jnp.stack([kc, vc], axis=1)
    w1 = P['nsa_cmp_w1'][0]
    w1h = w1.reshape(2, 2, CMP_STRIDE, HEAD_DIM, CMP_HIDDEN)
    eye = jnp.eye(NSA_KV_HEADS, dtype=F32)
    wab = jnp.einsum('jaldn,hg->jlhdagn', w1h, eye).reshape(
        2, CMP_STRIDE * NSA_KV_WIDTH, 2 * NSA_KV_HEADS * CMP_HIDDEN).astype(BF16)
    pos8 = jnp.broadcast_to(P['nsa_cmp_pos'][0].reshape(2, 1, CMP_BLOCK * HEAD_DIM),
                            (2, 8, CMP_BLOCK * HEAD_DIM)).astype(BF16)
    w2 = P['nsa_cmp_w2'][0].astype(BF16)
    kcb, vcbt = _compress_call(kv2, wab, pos8, w1.astype(BF16), w2, jnp.swapaxes(w2, 1, 2))

    qt = jnp.swapaxes(nsa3[:, :, :NSA_WIDTH], 1, 2)
    gt = jnp.swapaxes(nsa3[:, :, NSA_GATE_OFF:NSA_GATE_OFF + 32], 1, 2)
    heads = lambda j: nsa3[:, :, KV_OFF + j * NSA_KV_WIDTH:KV_OFF + (j + 1) * NSA_KV_WIDTH].reshape(
        B, S, Hk, Dh).transpose(0, 2, 1, 3)
    ks, vs, kw, vw = heads(2), heads(3), heads(4), heads(5)
    n_sel = S // SEL_BLOCK
    onehot = (np.arange(S)[:, None] // SEL_BLOCK == np.arange(n_sel)[None, :]).astype(np.float32)
    rest = NSA_AUG - Dh - n_sel
    ks_aug = jnp.concatenate([ks, jnp.broadcast_to(jnp.asarray(onehot, BF16), (B, Hk, S, n_sel)),
                              jnp.zeros((B, Hk, S, rest), BF16)], axis=-1)
    ks5 = ks_aug.reshape(B, Hk, S // tq, tq, NSA_AUG)
    vst5 = vs.reshape(B, Hk, S // tq, tq, Dh).swapaxes(3, 4)
    Sp = S + WINDOW
    front = ((0, 0), (0, 0), (WINDOW, 0), (0, 0))
    flag = np.zeros((Sp, rest), np.float32)
    flag[:WINDOW, 0] = 1.0
    kw_aug = jnp.concatenate([jnp.pad(kw, front), jnp.zeros((B, Hk, Sp, n_sel), BF16),
                              jnp.broadcast_to(jnp.asarray(flag, BF16), (B, Hk, Sp, rest))], axis=-1)
    kw5 = kw_aug.reshape(B, Hk, Sp // tq, tq, NSA_AUG)
    vwt5 = jnp.pad(vw, front).reshape(B, Hk, Sp // tq, tq, Dh).swapaxes(3, 4)

    n_cmp = (S - CMP_BLOCK) // CMP_STRIDE + 1
    cmp_start = np.arange(n_half) * CMP_STRIDE
    sel_start = np.arange(n_sel) * SEL_BLOCK
    overlap = ((cmp_start[:, None] <= sel_start[None, :] + SEL_BLOCK - 1)
               & (cmp_start[:, None] + CMP_BLOCK - 1 >= sel_start[None, :])
               & (np.arange(n_half)[:, None] < n_cmp)).astype(np.float32)
    tri = np.where(np.arange(tq)[:, None] <= np.arange(tq)[None, :], 0.0, NEG_INF).astype(np.float32)
    wbias = np.concatenate([NEG_INF - tri, np.zeros((WINDOW - tq, tq), np.float32), tri], axis=0)
    return _nsa2_call(qt, gt, ks5, vst5, kw5, vwt5, kcb, vcbt, jnp.asarray(overlap.T, BF16),
                      jnp.asarray(tri), jnp.asarray(wbias), S)


def _merge_kernel(x_ref, ya_ref, ybt_ref, gate_ref, wa_ref, wb_ref, wo_ref, o_ref):
    D = x_ref.shape[-1]
    ta = _dot(ya_ref[...], wa_ref[...])
    tb = _dot_tn(ybt_ref[0], wb_ref[...])
    ga = _sigmoid(gate_ref[:, :D].astype(F32))
    gb = _sigmoid(gate_ref[:, D:].astype(F32))
    mix = (ga * ta + gb * tb).astype(BF16)
    o_ref[...] = x_ref[...] + _dot(mix, wo_ref[...])


def _merge_call(x2, ya2, ybt, gates, wa, wb, wo, tm=512):
    T, D = x2.shape
    tiles_per_seq = ybt.shape[2] // tm
    row = lambda w: pl.BlockSpec((tm, w), lambda i: (i, 0))
    full = lambda a: pl.BlockSpec(a.shape, lambda i: (0,) * a.ndim)
    ybt_spec = pl.BlockSpec((1, ybt.shape[1], tm), lambda i: (i // tiles_per_seq, 0, i % tiles_per_seq))
    return pl.pallas_call(
        _merge_kernel,
        grid=(T // tm,),
        in_specs=[row(D), row(ya2.shape[1]), ybt_spec, row(gates.shape[1]),
                  full(wa), full(wb), full(wo)],
        out_specs=row(D),
        out_shape=jax.ShapeDtypeStruct((T, D), F32),
        compiler_params=pltpu.CompilerParams(
            dimension_semantics=("arbitrary",), vmem_limit_bytes=VMEM_LIMIT),
        name="merge",
    )(x2, ya2, ybt, gates, wa, wb, wo)


FFN_HALO = 8


def _rms(x, g):
    return x * lax.rsqrt(jnp.mean(x * x, axis=-1, keepdims=True) + NORM_EPS) * g


def _ffn_kernel(h_ref, halo_ref, p_ref, ln_ref, wup_ref, cw_ref, cb_ref, wdn_ref, wpg_ref, wpp_ref,
                o_ref, *, tiles_per_seq, fc):
    tm, D = h_ref.shape
    d_ff = wdn_ref.shape[0]
    h = h_ref[...]
    first = (pl.program_id(0) % tiles_per_seq) == 0
    halo = jnp.where(first, 0.0, halo_ref[...])
    ln2, ln3, lnf = ln_ref[0:1, :], ln_ref[1:2, :], ln_ref[2:3, :]
    u = jnp.concatenate([_rms(halo, ln2), _rms(h, ln2)], axis=0).astype(BF16)

    acc = jnp.zeros((tm, D), F32)
    for c in range(0, d_ff, fc):
        def conv(col):
            up = _dot(u, wup_ref[:, col:col + fc])
            out = cb_ref[:, col:col + fc]
            for j in range(CONV_WIDTH):
                lo = FFN_HALO - (CONV_WIDTH - 1) + j
                out = out + cw_ref[j:j + 1, col:col + fc] * up[lo:lo + tm]
            return out
        a = conv(c)
        b = conv(d_ff + c)
        act = (a * _sigmoid(a) * b).astype(BF16)
        acc = acc + _dot(act, wdn_ref[c:c + fc, :])
    h2 = h + acc
    gate = _sigmoid(_dot(_rms(h2, ln3).astype(BF16), wpg_ref[...]))
    h3 = h2 + gate * _dot(p_ref[...].astype(BF16), wpp_ref[...])
    o_ref[...] = _rms(h3, lnf)


def _ffn_call(h2d, p2d, lns, wup, cw, cb, wdn, wpg, wpp, seq, tm=256, fc=256):
    T, D = h2d.shape
    tiles_per_seq = seq // tm
    row = lambda w: pl.BlockSpec((tm, w), lambda i: (i, 0))
    full = lambda a: pl.BlockSpec(a.shape, lambda i: (0,) * a.ndim, pipeline_mode=pl.Buffered(1))
    halo = pl.BlockSpec((FFN_HALO, D), lambda i: (jnp.maximum(i * (tm // FFN_HALO) - 1, 0), 0))
    return pl.pallas_call(
        functools.partial(_ffn_kernel, tiles_per_seq=tiles_per_seq, fc=fc),
        grid=(T // tm,),
        in_specs=[row(D), halo, row(p2d.shape[1]), full(lns), full(wup), full(cw), full(cb),
                  full(wdn), full(wpg), full(wpp)],
        out_specs=row(D),
        out_shape=jax.ShapeDtypeStruct((T, D), F32),
        compiler_params=pltpu.CompilerParams(
            dimension_semantics=("arbitrary",), vmem_limit_bytes=VMEM_LIMIT),
        name="ffn",
    )(h2d, h2d, p2d, lns, wup, cw, cb, wdn, wpg, wpp)


def _prep_proj_weights(w_in, mu_wag, w1, a1, g1):
    D = w_in.shape[0]
    sizes = (RW_WIDTH, RW_WIDTH, RW_WIDTH, NSA_WIDTH) + (NSA_KV_WIDTH,) * 6 + (3 * NSA_Q_HEADS, D, D)
    offs = np.concatenate([[0], np.cumsum(sizes)])
    part = lambda i, j: w_in[:, offs[i]:offs[j]]
    mw, ma, mg = mu_wag[0][:, None], mu_wag[1][:, None], mu_wag[2][:, None]
    zg = jnp.zeros((D, RW_GATE_PAD - RW_GATE_LORA), F32)
    rw = jnp.concatenate([
        part(0, 3),
        (1.0 - mw) * w1, (1.0 - ma) * a1,
        mw * w1, ma * a1,
        (1.0 - mg) * g1, zg,
        mg * g1, zg], axis=1)
    nsa = jnp.concatenate([part(3, 11), jnp.zeros((D, 256 - 3 * NSA_Q_HEADS), F32)], axis=1)
    gates = part(11, 13)
    return jnp.concatenate([rw, nsa, gates], axis=1).astype(BF16)


def _prep_rwkv_weights(w2, a2, g2):
    z = jnp.zeros_like(w2)
    w2a2 = jnp.concatenate([jnp.concatenate([w2, z], axis=1),
                            jnp.concatenate([z, a2], axis=1)], axis=0).astype(BF16)
    g2p = jnp.concatenate([g2, jnp.zeros((RW_GATE_PAD - RW_GATE_LORA, RW_WIDTH), F32)],
                          axis=0).astype(BF16)
    return w2a2, g2p


def _rwkv_from_proj(rw3, P):
    w2a2, g2p = _prep_rwkv_weights(P['rw_w2'][0], P['rw_a2'][0], P['rw_g2'][0])
    vecs = jnp.stack([P['rw_w0'][0], P['rw_a0'][0], P['rw_k_k'][0], P['rw_k_a'][0],
                      P['rw_r_k'][0].reshape(-1), P['rw_lnx_g'][0], P['rw_lnx_b'][0],
                      jnp.zeros((RW_WIDTH,), F32)], axis=0)
    return _rwkv_call(rw3, P['rw_mu_rkv'][0], vecs, w2a2, g2p)


def kernel(x, p, ln1_g, w_in, rw_mu_rkv, rw_mu_wag, rw_w0, rw_w1, rw_w2, rw_a0, rw_a1, rw_a2, rw_g1, rw_g2, rw_k_k, rw_k_a, rw_r_k, rw_lnx_g, rw_lnx_b, nsa_cmp_pos, nsa_cmp_w1, nsa_cmp_w2, w_out_a, w_out_b, w_out, ln2_g, w_up, conv_w, conv_b, w_down, ln3_g, w_ple_gate, w_ple_proj, ln_f_g):
    B, S, D = x.shape
    T = B * S
    assert w_in.shape[0] == 1, "single-layer block"
    P = dict(rw_mu_rkv=rw_mu_rkv, rw_w0=rw_w0, rw_w2=rw_w2, rw_a0=rw_a0, rw_a2=rw_a2, rw_g2=rw_g2,
             rw_k_k=rw_k_k, rw_k_a=rw_k_a, rw_r_k=rw_r_k, rw_lnx_g=rw_lnx_g, rw_lnx_b=rw_lnx_b,
             nsa_cmp_pos=nsa_cmp_pos, nsa_cmp_w1=nsa_cmp_w1, nsa_cmp_w2=nsa_cmp_w2)
    h = x.reshape(T, D)
    w_all = _prep_proj_weights(w_in[0], rw_mu_wag[0], rw_w1[0], rw_a1[0], rw_g1[0])
    rw, nsa, gates = _proj_call(h, ln1_g[0][None], w_all)
    ya = _rwkv_from_proj(rw.reshape(B, S, RW_COLS), P)
    ybt = _nsa2_from_proj(nsa.reshape(B, S, NSA_COLS), P)
    h1 = _merge_call(h, ya.reshape(T, RW_WIDTH), ybt, gates,
                     w_out_a[0].astype(BF16), w_out_b[0].astype(BF16), w_out[0].astype(BF16))
    lns = jnp.stack([ln2_g[0], ln3_g[0], ln_f_g], axis=0)
    out = _ffn_call(h1, p[0].reshape(T, -1), lns, w_up[0].astype(BF16), conv_w[0], conv_b[0][None],
                    w_down[0].astype(BF16), w_ple_gate[0].astype(BF16), w_ple_proj[0].astype(BF16), S)
    return out.reshape(B, S, D)
```

```python
import functools

import numpy as np
import jax
import jax.numpy as jnp
from jax import lax
from jax.experimental import pallas as pl
from jax.experimental.pallas import tpu as pltpu

F32 = jnp.float32
BF16 = jnp.bfloat16

HEAD_DIM = 64
NORM_EPS = 1e-6
NEG_INF = -1e30

RW_HEADS = 8
RW_WIDTH = RW_HEADS * HEAD_DIM
RW_DECAY_LORA = 64
RW_AAA_LORA = 64
RW_GATE_LORA = 160
RW_LNX_EPS = 64e-5
RW_CHUNK = 64
RW_GROUP = 4
RW_GROUP_W = RW_GROUP * HEAD_DIM
RW_GATE_PAD = 256

NSA_Q_HEADS = 8
NSA_KV_HEADS = 2
NSA_GROUP = NSA_Q_HEADS // NSA_KV_HEADS
NSA_WIDTH = NSA_Q_HEADS * HEAD_DIM
NSA_KV_WIDTH = NSA_KV_HEADS * HEAD_DIM
CMP_BLOCK = 32
CMP_STRIDE = 16
CMP_HIDDEN = 128
SEL_BLOCK = 64
SEL_TOP = 16
SEL_FORCE_SCORE = 1e4
WINDOW = 512

CONV_WIDTH = 3

RW_COLS = 3 * RW_WIDTH + 2 * 128 + 2 * RW_GATE_PAD
NSA_COLS = NSA_WIDTH + 6 * NSA_KV_WIDTH + 256
GATE_COLS = 2 * 1024

VMEM_LIMIT = 56 * 1024 * 1024


def _dot(a, b):
    return jnp.dot(a, b, preferred_element_type=F32)


def _dot_nt(a, b):
    return lax.dot_general(a, b, (((1,), (1,)), ((), ())), preferred_element_type=F32)


def _dot_tn(a, b):
    return lax.dot_general(a, b, (((0,), (0,)), ((), ())), preferred_element_type=F32)


def _split2(x):
    hi = x.astype(BF16)
    lo = (x - hi.astype(F32)).astype(BF16)
    return hi, lo


def _split3(x):
    hi = x.astype(BF16)
    r1 = x - hi.astype(F32)
    mid = r1.astype(BF16)
    lo = (r1 - mid.astype(F32)).astype(BF16)
    return hi, mid, lo


def _sigmoid(x):
    return 1.0 / (1.0 + jnp.exp(-x))


def _softplus(x):
    return jnp.maximum(x, 0.0) + jnp.log(1.0 + jnp.exp(-jnp.abs(x)))


def _proj_kernel(x_ref, g_ref, w_ref, rw_ref, nsa_ref, gate_ref, *, chunk):
    x = x_ref[...]
    ms = jnp.mean(x * x, axis=-1, keepdims=True)
    u = (x * lax.rsqrt(ms + NORM_EPS) * g_ref[...]).astype(BF16)
    col = 0
    for o_ref in (rw_ref, nsa_ref, gate_ref):
        width = o_ref.shape[-1]
        for c in range(0, width, chunk):
            o_ref[:, c:c + chunk] = _dot(u, w_ref[:, col + c:col + c + chunk]).astype(o_ref.dtype)
        col += width


def _proj_call(x2, g, w_all, tm=512, chunk=768):
    T, D = x2.shape
    n_all = w_all.shape[1]
    return pl.pallas_call(
        functools.partial(_proj_kernel, chunk=chunk),
        grid=(T // tm,),
        in_specs=[
            pl.BlockSpec((tm, D), lambda i: (i, 0)),
            pl.BlockSpec((1, D), lambda i: (0, 0)),
            pl.BlockSpec((D, n_all), lambda i: (0, 0)),
        ],
        out_specs=[
            pl.BlockSpec((tm, RW_COLS), lambda i: (i, 0)),
            pl.BlockSpec((tm, NSA_COLS), lambda i: (i, 0)),
            pl.BlockSpec((tm, GATE_COLS), lambda i: (i, 0)),
        ],
        out_shape=[
            jax.ShapeDtypeStruct((T, RW_COLS), BF16),
            jax.ShapeDtypeStruct((T, NSA_COLS), BF16),
            jax.ShapeDtypeStruct((T, GATE_COLS), BF16),
        ],
        compiler_params=pltpu.CompilerParams(
            dimension_semantics=("arbitrary",), vmem_limit_bytes=VMEM_LIMIT),
        name="proj",
    )(x2, g, w_all)


def _rwkv_kernel(x_ref, mu_ref, vec_ref, w2a2_ref, g2_ref, o_ref, state_ref, prev_ref):
    C = RW_CHUNK
    GW = RW_GROUP_W
    W = RW_WIDTH
    NB = x_ref.shape[0]
    R = NB * C
    t_idx = pl.program_id(1)

    @pl.when(t_idx == 0)
    def _():
        state_ref[...] = jnp.zeros_like(state_ref)
        prev_ref[...] = jnp.zeros_like(prev_ref)

    x = x_ref[...].reshape(R, RW_COLS).astype(F32)
    rolled = pltpu.roll(x, 1, axis=0)
    row8 = lax.broadcasted_iota(jnp.int32, (8, 1), 0)
    pieces = []
    for bi in range(NB):
        pieces.append(jnp.where(row8 == 0, prev_ref[bi, 0:1, :], rolled[bi * C:bi * C + 8]))
        pieces.append(rolled[bi * C + 8:(bi + 1) * C])
        prev_ref[bi, 0:1, :] = x[(bi + 1) * C - 1:(bi + 1) * C, :]
    xs = jnp.concatenate(pieces, axis=0)

    mu = mu_ref[...]
    w0, a0, k_k, k_a, r_k, lnx_g, lnx_b = (vec_ref[i:i + 1, :] for i in range(7))

    def lerp(j):
        cur = x[:, j * W:(j + 1) * W]
        return cur + (xs[:, j * W:(j + 1) * W] - cur) * mu[j:j + 1, :]

    r, k, v = lerp(0), lerp(1), lerp(2)
    o = 3 * W
    pre_a = x[:, o:o + 128] + xs[:, o + 128:o + 256]
    lane = lax.broadcasted_iota(jnp.int32, (R, 128), 1)
    h_a = jnp.where(lane < RW_DECAY_LORA, jnp.tanh(pre_a), pre_a)
    lwa = _dot(h_a.astype(BF16), w2a2_ref[...])
    w = -_softplus(-(w0 + lwa[:, :W])) - 0.5
    ld = -jnp.exp(w)
    a = _sigmoid(a0 + lwa[:, W:])
    o += 256
    pre_g = x[:, o:o + RW_GATE_PAD] + xs[:, o + RW_GATE_PAD:o + 2 * RW_GATE_PAD]
    g = _dot(_sigmoid(pre_g).astype(BF16), g2_ref[...])

    gr = lax.broadcasted_iota(jnp.int32, (GW, GW), 0) // HEAD_DIM
    gc = lax.broadcasted_iota(jnp.int32, (GW, GW), 1) // HEAD_DIM
    blk = gr == gc
    ones_bd = jnp.where(blk, 1.0, 0.0).astype(BF16)

    def headsums(zs):
        parts = []
        for z in zs:
            hi, lo = _split2(z)
            parts += [hi[:, :GW], hi[:, GW:], lo[:, :GW], lo[:, GW:]]
        s = _dot(jnp.concatenate(parts, axis=0), ones_bd)
        outs = []
        for i in range(len(zs)):
            q = s[4 * R * i:4 * R * (i + 1)]
            q = q[:2 * R] + q[2 * R:]
            outs.append(jnp.concatenate([q[:R], q[R:]], axis=1))
        return outs

    kkr = k * k_k
    k2 = k * (1.0 + (a - 1.0) * k_a)
    kk_ss, bonus = headsums([kkr * kkr, r * k2 * r_k])
    kk = kkr / jnp.maximum(jnp.sqrt(kk_ss), 1e-12)
    b = kk * a

    tr = lax.broadcasted_iota(jnp.int32, (R, R), 0)
    tc = lax.broadcasted_iota(jnp.int32, (R, R), 1)
    tri = jnp.where((tr >= tc) & (tr // C == tc // C), 1.0, 0.0).astype(BF16)
    l_inc = _dot(tri, jnp.concatenate(_split3(ld), axis=1))
    l_inc = l_inc[:, :W] + l_inc[:, W:2 * W] + l_inc[:, 2 * W:]

    t_n = lax.broadcasted_iota(jnp.int32, (C, GW), 0)
    s_n = lax.broadcasted_iota(jnp.int32, (C, GW), 1) % HEAD_DIM
    strict = t_n > s_n
    incl = t_n >= s_n
    eye_n = jnp.where(t_n == s_n, 1.0, 0.0)

    def bd(z):
        z4 = jnp.concatenate([z] * RW_GROUP, axis=0)
        return jnp.where(blk, z4, 0.0).astype(BF16)

    e_neg = jnp.exp(-l_inc)
    r_hat = r * jnp.exp(l_inc)
    a_hat = -kk * jnp.exp(l_inc - ld)
    b_hat = b * e_neg
    k_hat = k2 * e_neg

    n_grp = W // GW
    chains = [(bi, gi) for bi in range(NB) for gi in range(n_grp)]
    cut = lambda z, c: z[c[0] * C:(c[0] + 1) * C, c[1] * GW:(c[1] + 1) * GW]
    each = lambda f, *lists: [f(*args) for args in zip(*lists)]

    a_h = [cut(a_hat, c).astype(BF16) for c in chains]
    r_h = [cut(r_hat, c).astype(BF16) for c in chains]
    ar = each(lambda x1, x2: jnp.concatenate([x1, x2], axis=0), a_h, r_h)
    m1 = each(_dot_nt, ar, [bd(cut(b_hat, c)) for c in chains])
    m2 = each(_dot_nt, ar, [bd(cut(k_hat, c)) for c in chains])
    m_ab = [jnp.where(strict, m[:C], 0.0) for m in m1]
    m_rb = [jnp.where(incl, m[C:], 0.0) for m in m1]
    m_ak = [jnp.where(strict, m[:C], 0.0) for m in m2]
    m_rk = [jnp.where(incl, m[C:], 0.0) for m in m2]

    tinv = [eye_n + m for m in m_ab]
    p = each(lambda m: _dot(m.astype(BF16), bd(m)), m_ab)
    power = 2
    while 2 * power < C:
        tp = each(lambda t, q: _dot(jnp.concatenate([t, q], axis=0).astype(BF16), bd(q)), tinv, p)
        tinv = each(lambda t, x1: t + x1[:C], tinv, tp)
        p = [x1[C:] for x1 in tp]
        power *= 2
    tinv = each(lambda t, q: t + _dot(t.astype(BF16), bd(q)), tinv, p)

    s_old = [state_ref[i * GW:(i + 1) * GW, :] for i in range(len(chains))]
    s_bf = [s.astype(BF16) for s in s_old]
    vg = [cut(v, c) for c in chains]
    bd_v = [bd(x1) for x1 in vg]
    xz = each(lambda x1, s, m, bv: _dot_nt(x1, s) + _dot(m.astype(BF16), bv), a_h, s_bf, m_ak, bd_v)
    u = each(lambda t, x1: _dot(t.astype(BF16), bd(x1)), tinv, xz)
    y = each(lambda x1, s, mb, mk, uu, bv:
             _dot_nt(x1, s) + _dot(jnp.concatenate([mb, mk], axis=1).astype(BF16),
                                   jnp.concatenate([bd(uu), bv], axis=0)),
             r_h, s_bf, m_rb, m_rk, u, bd_v)
    new_states = []
    for c, uu, vv, s in zip(chains, u, vg, s_old):
        lg = cut(l_inc, c)
        lc = lg[C - 1:C, :]
        e_tail = jnp.exp(lc - lg)
        upd = _dot_tn(jnp.concatenate([uu, vv], axis=0).astype(BF16),
                      jnp.concatenate([cut(b, c) * e_tail, cut(k2, c) * e_tail], axis=0).astype(BF16))
        new_states.append(s * jnp.exp(lc) + jnp.where(blk, upd, 0.0))
    state_ref[...] = jnp.concatenate(new_states, axis=0)
    y_rows = [jnp.concatenate(y[bi * n_grp:(bi + 1) * n_grp], axis=1) for bi in range(NB)]
    y = jnp.concatenate(y_rows, axis=0)
    mean = headsums([y])[0] * (1.0 / HEAD_DIM)
    yc = y - mean
    var = headsums([yc * yc])[0] * (1.0 / HEAD_DIM)
    yn = yc * lax.rsqrt(var + RW_LNX_EPS) * lnx_g + lnx_b
    yn = yn + bonus * v
    o_ref[...] = (yn * g).reshape(NB, C, W).astype(o_ref.dtype)


RW_SEQS_PER_STEP = 4


def _rwkv_call(rw3, mu, vecs, w2a2, g2p):
    B, S, _ = rw3.shape
    C = RW_CHUNK
    nb = RW_SEQS_PER_STEP if B % RW_SEQS_PER_STEP == 0 else 1
    n_groups = RW_WIDTH // RW_GROUP_W
    return pl.pallas_call(
        _rwkv_kernel,
        grid=(B // nb, S // C),
        in_specs=[
            pl.BlockSpec((nb, C, RW_COLS), lambda b, t: (b, t, 0)),
            pl.BlockSpec(mu.shape, lambda b, t: (0, 0)),
            pl.BlockSpec(vecs.shape, lambda b, t: (0, 0)),
            pl.BlockSpec(w2a2.shape, lambda b, t: (0, 0)),
            pl.BlockSpec(g2p.shape, lambda b, t: (0, 0)),
        ],
        out_specs=pl.BlockSpec((nb, C, RW_WIDTH), lambda b, t: (b, t, 0)),
        out_shape=jax.ShapeDtypeStruct((B, S, RW_WIDTH), BF16),
        scratch_shapes=[
            pltpu.VMEM((nb * n_groups * RW_GROUP_W, RW_GROUP_W), F32),
            pltpu.VMEM((nb, 8, RW_COLS), F32),
        ],
        compiler_params=pltpu.CompilerParams(
            dimension_semantics=("arbitrary", "arbitrary"), vmem_limit_bytes=VMEM_LIMIT),
        name="rwkv",
    )(rw3, mu, vecs, w2a2, g2p)


def _compress_kernel(kv_ref, wab_ref, pos_ref, w1_ref, w2_ref, w2t_ref, kcb_ref, vcbt_ref):
    n_half = kv_ref.shape[2]
    for j in range(2):
        pab = _dot(kv_ref[0, j], wab_ref[j])
        half = NSA_KV_HEADS * CMP_HIDDEN
        pa, pb = pab[:, :half], pab[:, half:]
        pb = pltpu.roll(pb, n_half - 1, axis=0)
        pos_term = _dot(pos_ref[j], w1_ref[j])[0:1]
        hid = pa + pb + jnp.concatenate([pos_term] * NSA_KV_HEADS, axis=1)
        act = (hid * _sigmoid(hid)).astype(BF16)
        for hk in range(NSA_KV_HEADS):
            a_h = act[:, hk * CMP_HIDDEN:(hk + 1) * CMP_HIDDEN]
            if j == 0:
                kcb_ref[0, hk] = _dot(a_h, w2_ref[j]).astype(kcb_ref.dtype)
            else:
                vcbt_ref[0, hk] = _dot_nt(w2t_ref[j], a_h).astype(vcbt_ref.dtype)


def _compress_call(kv2, wab, pos8, w1, w2, w2t):
    B, _, n_half, width = kv2.shape
    full = lambda a: pl.BlockSpec(a.shape, lambda b: (0,) * a.ndim)
    return pl.pallas_call(
        _compress_kernel,
        grid=(B,),
        in_specs=[pl.BlockSpec((1, 2, n_half, width), lambda b: (b, 0, 0, 0)),
                  full(wab), full(pos8), full(w1), full(w2), full(w2t)],
        out_specs=[pl.BlockSpec((1, NSA_KV_HEADS, n_half, HEAD_DIM), lambda b: (b, 0, 0, 0)),
                   pl.BlockSpec((1, NSA_KV_HEADS, HEAD_DIM, n_half), lambda b: (b, 0, 0, 0))],
        out_shape=[jax.ShapeDtypeStruct((B, NSA_KV_HEADS, n_half, HEAD_DIM), BF16),
                   jax.ShapeDtypeStruct((B, NSA_KV_HEADS, HEAD_DIM, n_half), BF16)],
        compiler_params=pltpu.CompilerParams(
            dimension_semantics=("arbitrary",), vmem_limit_bytes=VMEM_LIMIT),
        name="nsa_compress",
    )(kv2, wab, pos8, w1, w2, w2t)


KV_OFF = NSA_WIDTH
NSA_GATE_OFF = NSA_WIDTH + 6 * NSA_KV_WIDTH

NSA2_TQ = 256
NSA_AUG = 128
NSA_BIAS_ROWS = 32


def _nsa2_kernel(qt_ref, gt_ref, ks_ref, vst_ref, kw_ref, vwt_ref, kcb_ref, vcbt_ref, ovt_ref,
                 tri_ref, wbias_ref, o_ref, *, seq):
    tq = NSA2_TQ
    G = NSA_GROUP
    R = G * tq
    n_half = kcb_ref.shape[2]
    n_cmp = n_half - 1
    n_sel = seq // SEL_BLOCK
    n_top = min(SEL_TOP, n_sel)
    n_wchunks = WINDOW // tq + 1
    step = pl.program_id(1)
    q0 = step * tq

    t_lane = q0 + lax.broadcasted_iota(jnp.int32, (1, R), 1) % tq
    gates = _sigmoid(gt_ref[0].astype(F32))
    row32 = lax.broadcasted_iota(jnp.int32, (NSA_AUG - HEAD_DIM - NSA_BIAS_ROWS, tq), 0)
    pad_rows = jnp.where(row32 == 0, NEG_INF, 0.0).astype(BF16)

    hrow = lambda hk, g: slice((hk * G + g) * HEAD_DIM, (hk * G + g + 1) * HEAD_DIM)
    o_cmp, qaug = {}, {}
    for hk in range(NSA_KV_HEADS):
        q64 = jnp.concatenate([qt_ref[0, hrow(hk, g), :] for g in range(G)], axis=1)
        q64 = q64 * jnp.asarray(HEAD_DIM ** -0.5, BF16)

        cidx = lax.broadcasted_iota(jnp.int32, (n_half, R), 0)
        cvalid = (cidx * CMP_STRIDE + (CMP_BLOCK - 1) <= t_lane) & (cidx < n_cmp)
        s = jnp.where(cvalid, _dot(kcb_ref[0, hk], q64), NEG_INF)
        m = jnp.max(s, axis=0, keepdims=True)
        e = jnp.where(cvalid, jnp.exp(s - m), 0.0)
        l = jnp.sum(e, axis=0, keepdims=True)
        p_c = e / jnp.where(l > 0.0, l, 1.0)
        o_c = _dot(vcbt_ref[0, hk], p_c.astype(BF16))

        psum = p_c[:, 0:tq]
        for g in range(1, G):
            psum = psum + p_c[:, g * tq:(g + 1) * tq]
        hi, lo = _split2(psum)
        imp2 = _dot(ovt_ref[...], jnp.concatenate([hi, lo], axis=1))
        imp = imp2[:, :tq] + imp2[:, tq:]
        jblk = lax.broadcasted_iota(jnp.int32, (n_sel, tq), 0)
        cur = (q0 + lax.broadcasted_iota(jnp.int32, (n_sel, tq), 1)) // SEL_BLOCK
        forced = (jblk == 0) | (jblk == cur) | (jblk == cur - 1)
        score = jnp.where(forced, SEL_FORCE_SCORE, jnp.where(jblk <= cur, imp, -1.0))
        rank = jnp.zeros((n_sel, tq), F32)
        for j in range(n_sel):
            sj = score[j:j + 1, :]
            ahead = (sj > score) | ((sj == score) & (j < jblk))
            rank = rank + jnp.where(ahead, 1.0, 0.0)
        sel_bias = jnp.where(rank < n_top, 0.0, NEG_INF).astype(BF16)
        for g in range(G):
            o_cmp[hk, g] = o_c[:, g * tq:(g + 1) * tq]
            qaug[hk, g] = jnp.concatenate([q64[:, g * tq:(g + 1) * tq], sel_bias, pad_rows], axis=0)

    chains = [(hk, g) for hk in range(NSA_KV_HEADS) for g in range(G)]

    def softmax_pv(s_list, vt_of, carry=None):
        m_blk = [jnp.max(s, axis=0, keepdims=True) for s in s_list]
        if carry is None:
            m_new = m_blk
        else:
            m_new = [jnp.maximum(c[0], mb) for c, mb in zip(carry, m_blk)]
        p = [jnp.exp(s - mn) for s, mn in zip(s_list, m_new)]
        l_blk = [jnp.sum(x, axis=0, keepdims=True) for x in p]
        pv = [_dot(vt_of(c), x.astype(BF16)) for c, x in zip(chains, p)]
        if carry is None:
            return [(mn, lb, a) for mn, lb, a in zip(m_new, l_blk, pv)]
        alpha = [jnp.exp(c[0] - mn) for c, mn in zip(carry, m_new)]
        return [(mn, c[1] * al + lb, c[2] * al + a)
                for c, mn, al, lb, a in zip(carry, m_new, alpha, l_blk, pv)]

    kw_rows = [kw_ref[0, hk, pl.ds(step, n_wchunks)].reshape(n_wchunks * tq, NSA_AUG)
               for hk in range(NSA_KV_HEADS)]
    vw_cols = [jnp.concatenate([vwt_ref[0, hk, step + w] for w in range(n_wchunks)], axis=1)
               for hk in range(NSA_KV_HEADS)]
    wbias = wbias_ref[...]
    s_win = [_dot(kw_rows[hk], qaug[hk, g]) + wbias for hk, g in chains]
    win = softmax_pv(s_win, lambda c: vw_cols[c[0]])

    tri = tri_ref[...]
    s_diag = [_dot(ks_ref[0, hk, step], qaug[hk, g]) + tri for hk, g in chains]
    carry = softmax_pv(s_diag, lambda c: vst_ref[0, c[0], step])

    def body(j, flat):
        carry = [tuple(flat[3 * i:3 * i + 3]) for i in range(len(chains))]
        s_j = [_dot(ks_ref[0, hk, j], qaug[hk, g]) for hk, g in chains]
        new = softmax_pv(s_j, lambda c: vst_ref[0, c[0], j], carry)
        return tuple(x for c in new for x in c)

    flat = lax.fori_loop(0, step, body, tuple(x for c in carry for x in c))
    sel = [tuple(flat[3 * i:3 * i + 3]) for i in range(len(chains))]

    for i, (hk, g) in enumerate(chains):
        gate = lambda j: gates[(hk * G + g) * 3 + j:(hk * G + g) * 3 + j + 1, :]
        out = (gate(0) * o_cmp[hk, g] + gate(1) * (sel[i][2] / sel[i][1])
               + gate(2) * (win[i][2] / win[i][1]))
        o_ref[0, hrow(hk, g), :] = out.astype(o_ref.dtype)


def _nsa2_call(qt, gt, ks5, vst5, kw5, vwt5, kcb, vcbt, ovt, tri, wbias, seq):
    B = qt.shape[0]
    tq = NSA2_TQ
    per_b = lambda a: pl.BlockSpec((1,) + a.shape[1:], lambda b, i: (b,) + (0,) * (a.ndim - 1))
    full = lambda a: pl.BlockSpec(a.shape, lambda b, i: (0,) * a.ndim)
    return pl.pallas_call(
        functools.partial(_nsa2_kernel, seq=seq),
        grid=(B, seq // tq),
        in_specs=[pl.BlockSpec((1, NSA_WIDTH, tq), lambda b, i: (b, 0, i)),
                  pl.BlockSpec((1, gt.shape[1], tq), lambda b, i: (b, 0, i)),
                  per_b(ks5), per_b(vst5), per_b(kw5), per_b(vwt5), per_b(kcb), per_b(vcbt),
                  full(ovt), full(tri), full(wbias)],
        out_specs=pl.BlockSpec((1, NSA_WIDTH, tq), lambda b, i: (b, 0, i)),
        out_shape=jax.ShapeDtypeStruct((B, NSA_WIDTH, seq), BF16),
        compiler_params=pltpu.CompilerParams(
            dimension_semantics=("arbitrary", "arbitrary"), vmem_limit_bytes=VMEM_LIMIT),
        name="nsa_attention",
    )(qt, gt, ks5, vst5, kw5, vwt5, kcb, vcbt, ovt, tri, wbias)


def _nsa2_from_proj(nsa3, P):
    B, S, _ = nsa3.shape
    tq = NSA2_TQ
    Hk, Dh = NSA_KV_HEADS, HEAD_DIM
    assert S % tq == 0 and WINDOW % tq == 0 and S // SEL_BLOCK == NSA_BIAS_ROWS
    n_half = S // CMP_STRIDE
    kc = nsa3[:, :, KV_OFF:KV_OFF + NSA_KV_WIDTH].reshape(B, n_half, CMP_STRIDE * NSA_KV_WIDTH)
    vc = nsa3[:, :, KV_OFF + NSA_KV_WIDTH:KV_OFF + 2 * NSA_KV_WIDTH].reshape(B, n_half, CMP_STRIDE * NSA_KV_WIDTH)
    kv2 = jnp.stack([kc, vc], axis=1)
    w1 = P['nsa_cmp_w1'][0]
    w1h = w1.reshape(2, 2, CMP_STRIDE, HEAD_DIM, CMP_HIDDEN)
    eye = jnp.eye(NSA_KV_HEADS, dtype=F32)
    wab = jnp.einsum('jaldn,hg->jlhdagn', w1h, eye).reshape(
        2, CMP_STRIDE * NSA_KV_WIDTH, 2 * NSA_KV_HEADS * CMP_HIDDEN).astype(BF16)
    pos8 = jnp.broadcast_to(P['nsa_cmp_pos'][0].reshape(2, 1, CMP_BLOCK * HEAD_DIM),
                            (2, 8, CMP_BLOCK * HEAD_DIM)).astype(BF16)
    w2 = P['nsa_cmp_w2'][0].astype(BF16)
    kcb, vcbt = _compress_call(kv2, wab, pos8, w1.astype(BF16), w2, jnp.swapaxes(w2, 1, 2))

    qt = jnp.swapaxes(nsa3[:, :, :NSA_WIDTH], 1, 2)
    gt = jnp.swapaxes(nsa3[:, :, NSA_GATE_OFF:NSA_GATE_OFF + 32], 1, 2)
    heads = lambda j: nsa3[:, :, KV_OFF + j * NSA_KV_WIDTH:KV_OFF + (j + 1) * NSA_KV_WIDTH].reshape(
        B, S, Hk, Dh).transpose(0, 2, 1, 3)
    ks, vs, kw, vw = heads(2), heads(3), heads(4), heads(5)
    n_sel = S // SEL_BLOCK
    onehot = (np.arange(S)[:, None] // SEL_BLOCK == np.arange(n_sel)[None, :]).astype(np.float32)
    rest = NSA_AUG - Dh - n_sel
    ks_aug = jnp.concatenate([ks, jnp.broadcast_to(jnp.asarray(onehot, BF16), (B, Hk, S, n_sel)),
                              jnp.zeros((B, Hk, S, rest), BF16)], axis=-1)
    ks5 = ks_aug.reshape(B, Hk, S // tq, tq, NSA_AUG)
    vst5 = vs.reshape(B, Hk, S // tq, tq, Dh).swapaxes(3, 4)
    Sp = S + WINDOW
    front = ((0, 0), (0, 0), (WINDOW, 0), (0, 0))
    flag = np.zeros((Sp, rest), np.float32)
    flag[:WINDOW, 0] = 1.0
    kw_aug = jnp.concatenate([jnp.pad(kw, front), jnp.zeros((B, Hk, Sp, n_sel), BF16),
                              jnp.broadcast_to(jnp.asarray(flag, BF16), (B, Hk, Sp, rest))], axis=-1)
    kw5 = kw_aug.reshape(B, Hk, Sp // tq, tq, NSA_AUG)
    vwt5 = jnp.pad(vw, front).reshape(B, Hk, Sp // tq, tq, Dh).swapaxes(3, 4)

    n_cmp = (S - CMP_BLOCK) // CMP_STRIDE + 1
    cmp_start = np.arange(n_half) * CMP_STRIDE
    sel_start = np.arange(n_sel) * SEL_BLOCK
    overlap = ((cmp_start[:, None] <= sel_start[None, :] + SEL_BLOCK - 1)
               & (cmp_start[:, None] + CMP_BLOCK - 1 >= sel_start[None, :])
               & (np.arange(n_half)[:, None] < n_cmp)).astype(np.float32)
    tri = np.where(np.arange(tq)[:, None] <= np.arange(tq)[None, :], 0.0, NEG_INF).astype(np.float32)
    wbias = np.concatenate([NEG_INF - tri, np.zeros((WINDOW - tq, tq), np.float32), tri], axis=0)
    return _nsa2_call(qt, gt, ks5, vst5, kw5, vwt5, kcb, vcbt, jnp.asarray(overlap.T, BF16),
                      jnp.asarray(tri), jnp.asarray(wbias), S)


def _merge_kernel(x_ref, ya_ref, ybt_ref, gate_ref, wa_ref, wb_ref, wo_ref, o_ref):
    D = x_ref.shape[-1]
    ta = _dot(ya_ref[...], wa_ref[...])
    tb = _dot_tn(ybt_ref[0], wb_ref[...])
    ga = _sigmoid(gate_ref[:, :D].astype(F32))
    gb = _sigmoid(gate_ref[:, D:].astype(F32))
    mix = (ga * ta + gb * tb).astype(BF16)
    o_ref[...] = x_ref[...] + _dot(mix, wo_ref[...])


def _merge_call(x2, ya2, ybt, gates, wa, wb, wo, tm=512):
    T, D = x2.shape
    tiles_per_seq = ybt.shape[2] // tm
    row = lambda w: pl.BlockSpec((tm, w), lambda i: (i, 0))
    full = lambda a: pl.BlockSpec(a.shape, lambda i: (0,) * a.ndim)
    ybt_spec = pl.BlockSpec((1, ybt.shape[1], tm), lambda i: (i // tiles_per_seq, 0, i % tiles_per_seq))
    return pl.pallas_call(
        _merge_kernel,
        grid=(T // tm,),
        in_specs=[row(D), row(ya2.shape[1]), ybt_spec, row(gates.shape[1]),
                  full(wa), full(wb), full(wo)],
        out_specs=row(D),
        out_shape=jax.ShapeDtypeStruct((T, D), F32),
        compiler_params=pltpu.CompilerParams(
            dimension_semantics=("arbitrary",), vmem_limit_bytes=VMEM_LIMIT),
        name="merge",
    )(x2, ya2, ybt, gates, wa, wb, wo)


FFN_HALO = 8


def _rms(x, g):
    return x * lax.rsqrt(jnp.mean(x * x, axis=-1, keepdims=True) + NORM_EPS) * g


def _ffn_kernel(h_ref, halo_ref, p_ref, ln_ref, wup_ref, cw_ref, cb_ref, wdn_ref, wpg_ref, wpp_ref,
                o_ref, up0a_ref, up0b_ref, up1a_ref, up1b_ref, act_ref, *, tiles_per_seq, fc):
    up_refs = ((up0a_ref, up0b_ref), (up1a_ref, up1b_ref))
    tm = act_ref.shape[0]
    D = halo_ref.shape[1]
    V = tm // 8
    d_ff = wdn_ref.shape[0]
    slabs = lambda ref, w: jnp.concatenate([ref[0, :, a * w:(a + 1) * w] for a in range(V)], axis=0)
    h = slabs(h_ref, D)
    first = (pl.program_id(0) % tiles_per_seq) == 0
    halo = jnp.where(first, 0.0, halo_ref[...])
    ln2, ln3, lnf = ln_ref[0:1, :], ln_ref[1:2, :], ln_ref[2:3, :]
    u = jnp.concatenate([_rms(halo, ln2), _rms(h, ln2)], axis=0).astype(BF16)

    n_chunks = d_ff // fc
    sub = lax.broadcasted_iota(jnp.int32, (8, 1), 0)

    def project(c):
        for half in range(2):
            col = half * d_ff + c * fc
            up_refs[c % 2][half][...] = _dot(u, wup_ref[:, col:col + fc])

    def conv(c, half):
        ref = up_refs[c % 2][half]
        col = half * d_ff + c * fc
        halo_up = ref[0:FFN_HALO, :]
        last = lambda k: ref[FFN_HALO + tm - 8 * k:FFN_HALO + tm - 8 * (k - 1), :]
        wrap1 = pltpu.roll(jnp.where(sub == 7, halo_up, last(1)), 1, axis=0)
        wrap2 = pltpu.roll(jnp.where(sub == 7, pltpu.roll(halo_up, 1, axis=0), last(2)), 1, axis=0)
        x0 = ref[FFN_HALO:FFN_HALO + tm, :]
        x1 = jnp.concatenate([wrap1, ref[FFN_HALO:FFN_HALO + tm - 8, :]], axis=0)
        x2 = jnp.concatenate([wrap2, wrap1, ref[FFN_HALO:FFN_HALO + tm - 16, :]], axis=0)
        tap = lambda j: cw_ref[j:j + 1, col:col + fc]
        return cb_ref[:, col:col + fc] + tap(0) * x2 + tap(1) * x1 + tap(2) * x0

    project(0)
    for c in range(n_chunks):
        if c + 1 < n_chunks:
            project(c + 1)
        a = conv(c, 0)
        b = conv(c, 1)
        act_ref[:, c * fc:(c + 1) * fc] = (a * _sigmoid(a) * b).astype(BF16)
    h2 = h + _dot(act_ref[...], wdn_ref[...])
    gate = _sigmoid(_dot(_rms(h2, ln3).astype(BF16), wpg_ref[...]))
    h3 = h2 + gate * _dot(slabs(p_ref, p_ref.shape[2] // V).astype(BF16), wpp_ref[...])
    out = _rms(h3, lnf)
    for a in range(V):
        o_ref[0, :, a * D:(a + 1) * D] = out[a * 8:(a + 1) * 8, :]


FFN_TM = 512
FFN_FC = 256


def _ffn_call(h2d, p2d, lns, wup, cw, cb, wdn, wpg, wpp, seq):
    T, D = h2d.shape
    tm, fc = FFN_TM, FFN_FC
    assert CONV_WIDTH == 3 and seq % tm == 0 and wdn.shape[0] % fc == 0
    tiles_per_seq = seq // tm
    runs = lambda x: x.reshape(T // tm, 8, (tm // 8) * x.shape[1])
    run_spec = lambda w: pl.BlockSpec((1, 8, (tm // 8) * w), lambda i: (i, 0, 0))
    full = lambda a: pl.BlockSpec(a.shape, lambda i: (0,) * a.ndim, pipeline_mode=pl.Buffered(1))
    halo = pl.BlockSpec((FFN_HALO, D), lambda i: (jnp.maximum(i * (tm // FFN_HALO) - 1, 0), 0))
    out = pl.pallas_call(
        functools.partial(_ffn_kernel, tiles_per_seq=tiles_per_seq, fc=fc),
        grid=(T // tm,),
        in_specs=[run_spec(D), halo, run_spec(p2d.shape[1]), full(lns), full(wup), full(cw), full(cb),
                  full(wdn), full(wpg), full(wpp)],
        out_specs=run_spec(D),
        out_shape=jax.ShapeDtypeStruct((T // tm, 8, (tm // 8) * D), F32),
        scratch_shapes=[pltpu.VMEM((FFN_HALO + tm, fc), F32)] * 4 + [pltpu.VMEM((tm, wdn.shape[0]), BF16)],
        compiler_params=pltpu.CompilerParams(
            dimension_semantics=("arbitrary",), vmem_limit_bytes=VMEM_LIMIT),
        name="ffn",
    )(runs(h2d), h2d, runs(p2d), lns, wup, cw, cb, wdn, wpg, wpp)
    return out.reshape(T, D)


def _prep_proj_weights(w_in, mu_wag, w1, a1, g1):
    D = w_in.shape[0]
    sizes = (RW_WIDTH, RW_WIDTH, RW_WIDTH, NSA_WIDTH) + (NSA_KV_WIDTH,) * 6 + (3 * NSA_Q_HEADS, D, D)
    offs = np.concatenate([[0], np.cumsum(sizes)])
    part = lambda i, j: w_in[:, offs[i]:offs[j]]
    mw, ma, mg = mu_wag[0][:, None], mu_wag[1][:, None], mu_wag[2][:, None]
    zg = jnp.zeros((D, RW_GATE_PAD - RW_GATE_LORA), F32)
    rw = jnp.concatenate([
        part(0, 3),
        (1.0 - mw) * w1, (1.0 - ma) * a1,
        mw * w1, ma * a1,
        (1.0 - mg) * g1, zg,
        mg * g1, zg], axis=1)
    nsa = jnp.concatenate([part(3, 11), jnp.zeros((D, 256 - 3 * NSA_Q_HEADS), F32)], axis=1)
    gates = part(11, 13)
    return jnp.concatenate([rw, nsa, gates], axis=1).astype(BF16)


def _prep_rwkv_weights(w2, a2, g2):
    z = jnp.zeros_like(w2)
    w2a2 = jnp.concatenate([jnp.concatenate([w2, z], axis=1),
                            jnp.concatenate([z, a2], axis=1)], axis=0).astype(BF16)
    g2p = jnp.concatenate([g2, jnp.zeros((RW_GATE_PAD - RW_GATE_LORA, RW_WIDTH), F32)],
                          axis=0).astype(BF16)
    return w2a2, g2p


def _rwkv_from_proj(rw3, P):
    w2a2, g2p = _prep_rwkv_weights(P['rw_w2'][0], P['rw_a2'][0], P['rw_g2'][0])
    vecs = jnp.stack([P['rw_w0'][0], P['rw_a0'][0], P['rw_k_k'][0], P['rw_k_a'][0],
                      P['rw_r_k'][0].reshape(-1), P['rw_lnx_g'][0], P['rw_lnx_b'][0],
                      jnp.zeros((RW_WIDTH,), F32)], axis=0)
    return _rwkv_call(rw3, P['rw_mu_rkv'][0], vecs, w2a2, g2p)


def kernel(x, p, ln1_g, w_in, rw_mu_rkv, rw_mu_wag, rw_w0, rw_w1, rw_w2, rw_a0, rw_a1, rw_a2, rw_g1, rw_g2, rw_k_k, rw_k_a, rw_r_k, rw_lnx_g, rw_lnx_b, nsa_cmp_pos, nsa_cmp_w1, nsa_cmp_w2, w_out_a, w_out_b, w_out, ln2_g, w_up, conv_w, conv_b, w_down, ln3_g, w_ple_gate, w_ple_proj, ln_f_g):
    B, S, D = x.shape
    T = B * S
    assert w_in.shape[0] == 1, "single-layer block"
    P = dict(rw_mu_rkv=rw_mu_rkv, rw_w0=rw_w0, rw_w2=rw_w2, rw_a0=rw_a0, rw_a2=rw_a2, rw_g2=rw_g2,
             rw_k_k=rw_k_k, rw_k_a=rw_k_a, rw_r_k=rw_r_k, rw_lnx_g=rw_lnx_g, rw_lnx_b=rw_lnx_b,
             nsa_cmp_pos=nsa_cmp_pos, nsa_cmp_w1=nsa_cmp_w1, nsa_cmp_w2=nsa_cmp_w2)
    h = x.reshape(T, D)
    w_all = _prep_proj_weights(w_in[0], rw_mu_wag[0], rw_w1[0], rw_a1[0], rw_g1[0])
    rw, nsa, gates = _proj_call(h, ln1_g[0][None], w_all)
    ya = _rwkv_from_proj(rw.reshape(B, S, RW_COLS), P)
    ybt = _nsa2_from_proj(nsa.reshape(B, S, NSA_COLS), P)
    h1 = _merge_call(h, ya.reshape(T, RW_WIDTH), ybt, gates,
                     w_out_a[0].astype(BF16), w_out_b[0].astype(BF16), w_out[0].astype(BF16))
    lns = jnp.stack([ln2_g[0], ln3_g[0], ln_f_g], axis=0)
    out = _ffn_call(h1, p[0].reshape(T, -1), lns, w_up[0].astype(BF16), conv_w[0], conv_b[0][None],
                    w_down[0].astype(BF16), w_ple_gate[0].astype(BF16), w_ple_proj[0].astype(BF16), S)
    return out.reshape(B, S, D)
```

```python
import functools

import numpy as np
import jax
import jax.numpy as jnp
from jax import lax
from jax.experimental import pallas as pl
from jax.experimental.pallas import tpu as pltpu

F32 = jnp.float32
BF16 = jnp.bfloat16

HEAD_DIM = 64
NORM_EPS = 1e-6
NEG_INF = -1e30

RW_HEADS = 8
RW_WIDTH = RW_HEADS * HEAD_DIM
RW_DECAY_LORA = 64
RW_AAA_LORA = 64
RW_GATE_LORA = 160
RW_LNX_EPS = 64e-5
RW_CHUNK = 64
RW_GROUP = 4
RW_GROUP_W = RW_GROUP * HEAD_DIM
RW_GATE_PAD = 256

NSA_Q_HEADS = 8
NSA_KV_HEADS = 2
NSA_GROUP = NSA_Q_HEADS // NSA_KV_HEADS
NSA_WIDTH = NSA_Q_HEADS * HEAD_DIM
NSA_KV_WIDTH = NSA_KV_HEADS * HEAD_DIM
CMP_BLOCK = 32
CMP_STRIDE = 16
CMP_HIDDEN = 128
SEL_BLOCK = 64
SEL_TOP = 16
SEL_FORCE_SCORE = 1e4
WINDOW = 512

CONV_WIDTH = 3

RW_COLS = 3 * RW_WIDTH + 2 * 128 + 2 * RW_GATE_PAD
GATE_COLS = 2 * 1024

VMEM_LIMIT = 56 * 1024 * 1024


def _dot(a, b):
    return jnp.dot(a, b, preferred_element_type=F32)


def _dot_nt(a, b):
    return lax.dot_general(a, b, (((1,), (1,)), ((), ())), preferred_element_type=F32)


def _dot_tn(a, b):
    return lax.dot_general(a, b, (((0,), (0,)), ((), ())), preferred_element_type=F32)


def _split2(x):
    hi = x.astype(BF16)
    lo = (x - hi.astype(F32)).astype(BF16)
    return hi, lo


def _split3(x):
    hi = x.astype(BF16)
    r1 = x - hi.astype(F32)
    mid = r1.astype(BF16)
    lo = (r1 - mid.astype(F32)).astype(BF16)
    return hi, mid, lo


def _sigmoid(x):
    return 1.0 / (1.0 + jnp.exp(-x))


def _softplus(x):
    return jnp.maximum(x, 0.0) + jnp.log(1.0 + jnp.exp(-jnp.abs(x)))


PROJ_TM = 512
PROJ_CHUNK = 768
KVC_COLS = 2 * NSA_KV_WIDTH
KEY_COLS = NSA_KV_HEADS * 128
NSA_GATE_ROWS = 32
T_ROWS = NSA_WIDTH + 2 * NSA_KV_WIDTH + NSA_GATE_ROWS


def _proj_kernel(x_ref, g_ref, w_ref, wt_ref, rw_ref, kvc_ref, ksa_ref, kwa_ref, gate_ref,
                 qt_ref, vst_ref, vwt_ref, gt_ref, *, tiles_per_seq):
    tm = x_ref.shape[0]
    x = x_ref[...]
    ms = jnp.mean(x * x, axis=-1, keepdims=True)
    u = (x * lax.rsqrt(ms + NORM_EPS) * g_ref[...]).astype(BF16)

    col = 0
    for o_ref in (rw_ref, kvc_ref):
        width = o_ref.shape[-1]
        for c in range(0, width, PROJ_CHUNK):
            hi = min(c + PROJ_CHUNK, width)
            o_ref[:, c:hi] = _dot(u, w_ref[:, col + c:col + hi]).astype(o_ref.dtype)
        col += width

    s0 = (pl.program_id(0) % tiles_per_seq) * tm
    blk = (s0 + lax.broadcasted_iota(jnp.int32, (tm, KEY_COLS), 0)) // SEL_BLOCK
    lane = lax.broadcasted_iota(jnp.int32, (tm, KEY_COLS), 1) % 128
    onehot = jnp.where(lane - HEAD_DIM == blk, 1.0, 0.0)
    ks = _dot(u, w_ref[:, col:col + KEY_COLS]) + onehot
    kw = _dot(u, w_ref[:, col + KEY_COLS:col + 2 * KEY_COLS])
    for hk in range(NSA_KV_HEADS):
        ksa_ref[0, hk] = ks[:, hk * 128:(hk + 1) * 128].astype(ksa_ref.dtype)
        kwa_ref[0, hk] = kw[:, hk * 128:(hk + 1) * 128].astype(kwa_ref.dtype)
    col += 2 * KEY_COLS

    width = gate_ref.shape[-1]
    for c in range(0, width, PROJ_CHUNK):
        hi = min(c + PROJ_CHUNK, width)
        gate_ref[:, c:hi] = _dot(u, w_ref[:, col + c:col + hi]).astype(gate_ref.dtype)

    t = _dot_nt(wt_ref[...], u)
    row = 0
    for o_ref in (qt_ref, vst_ref, vwt_ref, gt_ref):
        n = o_ref.shape[1]
        o_ref[0] = t[row:row + n].astype(o_ref.dtype)
        row += n


def _proj_call(x2, g, w_all, wt_all, seq):
    T, D = x2.shape
    tm = PROJ_TM
    B = T // seq
    tps = seq // tm
    rows = lambda w: pl.BlockSpec((tm, w), lambda i: (i, 0))
    full = lambda a: pl.BlockSpec(a.shape, lambda i: (0,) * a.ndim)
    keys = pl.BlockSpec((1, NSA_KV_HEADS, tm, 128), lambda i: (i // tps, 0, i % tps, 0))
    tcols = lambda n: pl.BlockSpec((1, n, tm), lambda i: (i // tps, 0, i % tps))
    sds = jax.ShapeDtypeStruct
    return pl.pallas_call(
        functools.partial(_proj_kernel, tiles_per_seq=tps),
        grid=(T // tm,),
        in_specs=[rows(D), full(g), full(w_all), full(wt_all)],
        out_specs=[rows(RW_COLS), rows(KVC_COLS), keys, keys, rows(GATE_COLS),
                   tcols(NSA_WIDTH), tcols(NSA_KV_WIDTH), tcols(NSA_KV_WIDTH), tcols(NSA_GATE_ROWS)],
        out_shape=[sds((T, RW_COLS), BF16), sds((T, KVC_COLS), BF16),
                   sds((B, NSA_KV_HEADS, seq, 128), BF16), sds((B, NSA_KV_HEADS, seq, 128), BF16),
                   sds((T, GATE_COLS), BF16),
                   sds((B, NSA_WIDTH, seq), BF16), sds((B, NSA_KV_WIDTH, seq), BF16),
                   sds((B, NSA_KV_WIDTH, seq), BF16), sds((B, NSA_GATE_ROWS, seq), BF16)],
        compiler_params=pltpu.CompilerParams(
            dimension_semantics=("arbitrary",), vmem_limit_bytes=VMEM_LIMIT),
        name="proj",
    )(x2, g, w_all, wt_all)


def _rwkv_kernel(x_ref, mu_ref, vec_ref, w2a2_ref, g2_ref, o_ref, state_ref, prev_ref):
    C = RW_CHUNK
    GW = RW_GROUP_W
    W = RW_WIDTH
    NB = x_ref.shape[0]
    R = NB * C
    t_idx = pl.program_id(1)

    @pl.when(t_idx == 0)
    def _():
        state_ref[...] = jnp.zeros_like(state_ref)
        prev_ref[...] = jnp.zeros_like(prev_ref)

    x = x_ref[...].reshape(R, RW_COLS).astype(F32)
    rolled = pltpu.roll(x, 1, axis=0)
    row8 = lax.broadcasted_iota(jnp.int32, (8, 1), 0)
    pieces = []
    for bi in range(NB):
        pieces.append(jnp.where(row8 == 0, prev_ref[bi, 0:1, :], rolled[bi * C:bi * C + 8]))
        pieces.append(rolled[bi * C + 8:(bi + 1) * C])
        prev_ref[bi, 0:1, :] = x[(bi + 1) * C - 1:(bi + 1) * C, :]
    xs = jnp.concatenate(pieces, axis=0)

    mu = mu_ref[...]
    w0, a0, k_k, k_a, r_k, lnx_g, lnx_b = (vec_ref[i:i + 1, :] for i in range(7))

    def lerp(j):
        cur = x[:, j * W:(j + 1) * W]
        return cur + (xs[:, j * W:(j + 1) * W] - cur) * mu[j:j + 1, :]

    r, k, v = lerp(0), lerp(1), lerp(2)
    o = 3 * W
    pre_a = x[:, o:o + 128] + xs[:, o + 128:o + 256]
    lane = lax.broadcasted_iota(jnp.int32, (R, 128), 1)
    h_a = jnp.where(lane < RW_DECAY_LORA, jnp.tanh(pre_a), pre_a)
    lwa = _dot(h_a.astype(BF16), w2a2_ref[...])
    w = -_softplus(-(w0 + lwa[:, :W])) - 0.5
    ld = -jnp.exp(w)
    a = _sigmoid(a0 + lwa[:, W:])
    o += 256
    pre_g = x[:, o:o + RW_GATE_PAD] + xs[:, o + RW_GATE_PAD:o + 2 * RW_GATE_PAD]
    g = _dot(_sigmoid(pre_g).astype(BF16), g2_ref[...])

    gr = lax.broadcasted_iota(jnp.int32, (GW, GW), 0) // HEAD_DIM
    gc = lax.broadcasted_iota(jnp.int32, (GW, GW), 1) // HEAD_DIM
    blk = gr == gc
    ones_bd = jnp.where(blk, 1.0, 0.0).astype(BF16)

    def headsums(zs):
        parts = []
        for z in zs:
            hi, lo = _split2(z)
            parts += [hi[:, :GW], hi[:, GW:], lo[:, :GW], lo[:, GW:]]
        s = _dot(jnp.concatenate(parts, axis=0), ones_bd)
        outs = []
        for i in range(len(zs)):
            q = s[4 * R * i:4 * R * (i + 1)]
            q = q[:2 * R] + q[2 * R:]
            outs.append(jnp.concatenate([q[:R], q[R:]], axis=1))
        return outs

    kkr = k * k_k
    k2 = k * (1.0 + (a - 1.0) * k_a)
    kk_ss, bonus = headsums([kkr * kkr, r * k2 * r_k])
    kk = kkr / jnp.maximum(jnp.sqrt(kk_ss), 1e-12)
    b = kk * a

    tr = lax.broadcasted_iota(jnp.int32, (R, R), 0)
    tc = lax.broadcasted_iota(jnp.int32, (R, R), 1)
    tri = jnp.where((tr >= tc) & (tr // C == tc // C), 1.0, 0.0).astype(BF16)
    l_inc = _dot(tri, jnp.concatenate(_split3(ld), axis=1))
    l_inc = l_inc[:, :W] + l_inc[:, W:2 * W] + l_inc[:, 2 * W:]

    t_n = lax.broadcasted_iota(jnp.int32, (C, GW), 0)
    s_n = lax.broadcasted_iota(jnp.int32, (C, GW), 1) % HEAD_DIM
    strict = t_n > s_n
    incl = t_n >= s_n
    eye_n = jnp.where(t_n == s_n, 1.0, 0.0)

    def bd(z):
        z4 = jnp.concatenate([z] * RW_GROUP, axis=0)
        return jnp.where(blk, z4, 0.0).astype(BF16)

    e_neg = jnp.exp(-l_inc)
    r_hat = r * jnp.exp(l_inc)
    a_hat = -kk * jnp.exp(l_inc - ld)
    b_hat = b * e_neg
    k_hat = k2 * e_neg

    n_grp = W // GW
    chains = [(bi, gi) for bi in range(NB) for gi in range(n_grp)]
    cut = lambda z, c: z[c[0] * C:(c[0] + 1) * C, c[1] * GW:(c[1] + 1) * GW]
    each = lambda f, *lists: [f(*args) for args in zip(*lists)]

    a_h = [cut(a_hat, c).astype(BF16) for c in chains]
    r_h = [cut(r_hat, c).astype(BF16) for c in chains]
    ar = each(lambda x1, x2: jnp.concatenate([x1, x2], axis=0), a_h, r_h)
    m1 = each(_dot_nt, ar, [bd(cut(b_hat, c)) for c in chains])
    m2 = each(_dot_nt, ar, [bd(cut(k_hat, c)) for c in chains])
    m_ab = [jnp.where(strict, m[:C], 0.0) for m in m1]
    m_rb = [jnp.where(incl, m[C:], 0.0) for m in m1]
    m_ak = [jnp.where(strict, m[:C], 0.0) for m in m2]
    m_rk = [jnp.where(incl, m[C:], 0.0) for m in m2]

    tinv = [eye_n + m for m in m_ab]
    p = each(lambda m: _dot(m.astype(BF16), bd(m)), m_ab)
    power = 2
    while 2 * power < C:
        tp = each(lambda t, q: _dot(jnp.concatenate([t, q], axis=0).astype(BF16), bd(q)), tinv, p)
        tinv = each(lambda t, x1: t + x1[:C], tinv, tp)
        p = [x1[C:] for x1 in tp]
        power *= 2
    tinv = each(lambda t, q: t + _dot(t.astype(BF16), bd(q)), tinv, p)

    s_old = [state_ref[i * GW:(i + 1) * GW, :] for i in range(len(chains))]
    s_bf = [s.astype(BF16) for s in s_old]
    vg = [cut(v, c) for c in chains]
    bd_v = [bd(x1) for x1 in vg]
    xz = each(lambda x1, s, m, bv: _dot_nt(x1, s) + _dot(m.astype(BF16), bv), a_h, s_bf, m_ak, bd_v)
    u = each(lambda t, x1: _dot(t.astype(BF16), bd(x1)), tinv, xz)
    y = each(lambda x1, s, mb, mk, uu, bv:
             _dot_nt(x1, s) + _dot(jnp.concatenate([mb, mk], axis=1).astype(BF16),
                                   jnp.concatenate([bd(uu), bv], axis=0)),
             r_h, s_bf, m_rb, m_rk, u, bd_v)
    new_states = []
    for c, uu, vv, s in zip(chains, u, vg, s_old):
        lg = cut(l_inc, c)
        lc = lg[C - 1:C, :]
        e_tail = jnp.exp(lc - lg)
        upd = _dot_tn(jnp.concatenate([uu, vv], axis=0).astype(BF16),
                      jnp.concatenate([cut(b, c) * e_tail, cut(k2, c) * e_tail], axis=0).astype(BF16))
        new_states.append(s * jnp.exp(lc) + jnp.where(blk, upd, 0.0))
    state_ref[...] = jnp.concatenate(new_states, axis=0)
    y_rows = [jnp.concatenate(y[bi * n_grp:(bi + 1) * n_grp], axis=1) for bi in range(NB)]
    y = jnp.concatenate(y_rows, axis=0)
    mean = headsums([y])[0] * (1.0 / HEAD_DIM)
    yc = y - mean
    var = headsums([yc * yc])[0] * (1.0 / HEAD_DIM)
    yn = yc * lax.rsqrt(var + RW_LNX_EPS) * lnx_g + lnx_b
    yn = yn + bonus * v
    o_ref[...] = (yn * g).reshape(NB, C, W).astype(o_ref.dtype)


RW_SEQS_PER_STEP = 4


def _rwkv_call(rw3, mu, vecs, w2a2, g2p):
    B, S, _ = rw3.shape
    C = RW_CHUNK
    nb = RW_SEQS_PER_STEP if B % RW_SEQS_PER_STEP == 0 else 1
    n_groups = RW_WIDTH // RW_GROUP_W
    return pl.pallas_call(
        _rwkv_kernel,
        grid=(B // nb, S // C),
        in_specs=[
            pl.BlockSpec((nb, C, RW_COLS), lambda b, t: (b, t, 0)),
            pl.BlockSpec(mu.shape, lambda b, t: (0, 0)),
            pl.BlockSpec(vecs.shape, lambda b, t: (0, 0)),
            pl.BlockSpec(w2a2.shape, lambda b, t: (0, 0)),
            pl.BlockSpec(g2p.shape, lambda b, t: (0, 0)),
        ],
        out_specs=pl.BlockSpec((nb, C, RW_WIDTH), lambda b, t: (b, t, 0)),
        out_shape=jax.ShapeDtypeStruct((B, S, RW_WIDTH), BF16),
        scratch_shapes=[
            pltpu.VMEM((nb * n_groups * RW_GROUP_W, RW_GROUP_W), F32),
            pltpu.VMEM((nb, 8, RW_COLS), F32),
        ],
        compiler_params=pltpu.CompilerParams(
            dimension_semantics=("arbitrary", "arbitrary"), vmem_limit_bytes=VMEM_LIMIT),
        name="rwkv",
    )(rw3, mu, vecs, w2a2, g2p)


def _compress_kernel(kv_ref, wab_ref, pos_ref, w1_ref, w2_ref, w2t_ref, kcb_ref, vcbt_ref):
    n_half = kv_ref.shape[1]
    for j in range(2):
        pab = _dot(kv_ref[0], wab_ref[j])
        half = NSA_KV_HEADS * CMP_HIDDEN
        pa, pb = pab[:, :half], pab[:, half:]
        pb = pltpu.roll(pb, n_half - 1, axis=0)
        pos_term = _dot(pos_ref[j], w1_ref[j])[0:1]
        hid = pa + pb + jnp.concatenate([pos_term] * NSA_KV_HEADS, axis=1)
        act = (hid * _sigmoid(hid)).astype(BF16)
        for hk in range(NSA_KV_HEADS):
            a_h = act[:, hk * CMP_HIDDEN:(hk + 1) * CMP_HIDDEN]
            if j == 0:
                kcb_ref[0, hk] = _dot(a_h, w2_ref[j]).astype(kcb_ref.dtype)
            else:
                vcbt_ref[0, hk] = _dot_nt(w2t_ref[j], a_h).astype(vcbt_ref.dtype)


def _compress_call(kv, wab, pos8, w1, w2, w2t):
    B, n_half, width = kv.shape
    full = lambda a: pl.BlockSpec(a.shape, lambda b: (0,) * a.ndim)
    return pl.pallas_call(
        _compress_kernel,
        grid=(B,),
        in_specs=[pl.BlockSpec((1, n_half, width), lambda b: (b, 0, 0)),
                  full(wab), full(pos8), full(w1), full(w2), full(w2t)],
        out_specs=[pl.BlockSpec((1, NSA_KV_HEADS, n_half, HEAD_DIM), lambda b: (b, 0, 0, 0)),
                   pl.BlockSpec((1, NSA_KV_HEADS, HEAD_DIM, n_half), lambda b: (b, 0, 0, 0))],
        out_shape=[jax.ShapeDtypeStruct((B, NSA_KV_HEADS, n_half, HEAD_DIM), BF16),
                   jax.ShapeDtypeStruct((B, NSA_KV_HEADS, HEAD_DIM, n_half), BF16)],
        compiler_params=pltpu.CompilerParams(
            dimension_semantics=("arbitrary",), vmem_limit_bytes=VMEM_LIMIT),
        name="nsa_compress",
    )(kv, wab, pos8, w1, w2, w2t)


NSA2_TQ = 256
NSA_AUG = 128
NSA_BIAS_ROWS = 32


def _nsa2_kernel(qt_ref, gt_ref, ks_ref, vst_ref, kw_ref, vwt_ref, kcb_ref, vcbt_ref, ovt_ref,
                 tri_ref, wbias_ref, o_ref, *, seq):
    tq = NSA2_TQ
    G = NSA_GROUP
    R = G * tq
    n_half = kcb_ref.shape[2]
    n_cmp = n_half - 1
    n_sel = seq // SEL_BLOCK
    n_top = min(SEL_TOP, n_sel)
    n_wchunks = WINDOW // tq + 1
    step = pl.program_id(1)
    q0 = step * tq

    t_lane = q0 + lax.broadcasted_iota(jnp.int32, (1, R), 1) % tq
    gates = _sigmoid(gt_ref[0].astype(F32))
    row32 = lax.broadcasted_iota(jnp.int32, (NSA_AUG - HEAD_DIM - NSA_BIAS_ROWS, tq), 0)
    pad_rows = jnp.where(row32 == 0, NEG_INF, 0.0).astype(BF16)

    hrow = lambda hk, g: slice((hk * G + g) * HEAD_DIM, (hk * G + g + 1) * HEAD_DIM)
    o_cmp, qaug = {}, {}
    for hk in range(NSA_KV_HEADS):
        q64 = jnp.concatenate([qt_ref[0, hrow(hk, g), :] for g in range(G)], axis=1)
        q64 = q64 * jnp.asarray(HEAD_DIM ** -0.5, BF16)

        cidx = lax.broadcasted_iota(jnp.int32, (n_half, R), 0)
        cvalid = (cidx * CMP_STRIDE + (CMP_BLOCK - 1) <= t_lane) & (cidx < n_cmp)
        s = jnp.where(cvalid, _dot(kcb_ref[0, hk], q64), NEG_INF)
        m = jnp.max(s, axis=0, keepdims=True)
        e = jnp.where(cvalid, jnp.exp(s - m), 0.0)
        l = jnp.sum(e, axis=0, keepdims=True)
        p_c = e / jnp.where(l > 0.0, l, 1.0)
        o_c = _dot(vcbt_ref[0, hk], p_c.astype(BF16))

        psum = p_c[:, 0:tq]
        for g in range(1, G):
            psum = psum + p_c[:, g * tq:(g + 1) * tq]
        hi, lo = _split2(psum)
        imp2 = _dot(ovt_ref[...], jnp.concatenate([hi, lo], axis=1))
        imp = imp2[:, :tq] + imp2[:, tq:]
        jblk = lax.broadcasted_iota(jnp.int32, (n_sel, tq), 0)
        cur = (q0 + lax.broadcasted_iota(jnp.int32, (n_sel, tq), 1)) // SEL_BLOCK
        forced = (jblk == 0) | (jblk == cur) | (jblk == cur - 1)
        score = jnp.where(forced, SEL_FORCE_SCORE, jnp.where(jblk <= cur, imp, -1.0))
        rank = jnp.zeros((n_sel, tq), F32)
        for j in range(n_sel):
            sj = score[j:j + 1, :]
            ahead = (sj > score) | ((sj == score) & (j < jblk))
            rank = rank + jnp.where(ahead, 1.0, 0.0)
        sel_bias = jnp.where(rank < n_top, 0.0, NEG_INF).astype(BF16)
        for g in range(G):
            o_cmp[hk, g] = o_c[:, g * tq:(g + 1) * tq]
            qaug[hk, g] = jnp.concatenate([q64[:, g * tq:(g + 1) * tq], sel_bias, pad_rows], axis=0)

    chains = [(hk, g) for hk in range(NSA_KV_HEADS) for g in range(G)]

    def softmax_pv(s_list, vt_of, carry=None):
        m_blk = [jnp.max(s, axis=0, keepdims=True) for s in s_list]
        if carry is None:
            m_new = m_blk
        else:
            m_new = [jnp.maximum(c[0], mb) for c, mb in zip(carry, m_blk)]
        p = [jnp.exp(s - mn) for s, mn in zip(s_list, m_new)]
        l_blk = [jnp.sum(x, axis=0, keepdims=True) for x in p]
        pv = [_dot(vt_of(c), x.astype(BF16)) for c, x in zip(chains, p)]
        if carry is None:
            return [(mn, lb, a) for mn, lb, a in zip(m_new, l_blk, pv)]
        alpha = [jnp.exp(c[0] - mn) for c, mn in zip(carry, m_new)]
        return [(mn, c[1] * al + lb, c[2] * al + a)
                for c, mn, al, lb, a in zip(carry, m_new, alpha, l_blk, pv)]

    kw_rows = [kw_ref[0, hk, pl.ds(step, n_wchunks)].reshape(n_wchunks * tq, NSA_AUG)
               for hk in range(NSA_KV_HEADS)]
    vw_cols = [jnp.concatenate([vwt_ref[0, hk, step + w] for w in range(n_wchunks)], axis=1)
               for hk in range(NSA_KV_HEADS)]
    wbias = wbias_ref[...]
    s_win = [_dot(kw_rows[hk], qaug[hk, g]) + wbias for hk, g in chains]
    win = softmax_pv(s_win, lambda c: vw_cols[c[0]])

    tri = tri_ref[...]
    s_diag = [_dot(ks_ref[0, hk, step], qaug[hk, g]) + tri for hk, g in chains]
    carry = softmax_pv(s_diag, lambda c: vst_ref[0, c[0], step])

    def body(j, flat):
        carry = [tuple(flat[3 * i:3 * i + 3]) for i in range(len(chains))]
        s_j = [_dot(ks_ref[0, hk, j], qaug[hk, g]) for hk, g in chains]
        new = softmax_pv(s_j, lambda c: vst_ref[0, c[0], j], carry)
        return tuple(x for c in new for x in c)

    flat = lax.fori_loop(0, step, body, tuple(x for c in carry for x in c))
    sel = [tuple(flat[3 * i:3 * i + 3]) for i in range(len(chains))]

    for i, (hk, g) in enumerate(chains):
        gate = lambda j: gates[(hk * G + g) * 3 + j:(hk * G + g) * 3 + j + 1, :]
        out = (gate(0) * o_cmp[hk, g] + gate(1) * (sel[i][2] / sel[i][1])
               + gate(2) * (win[i][2] / win[i][1]))
        o_ref[0, hrow(hk, g), :] = out.astype(o_ref.dtype)


def _nsa2_call(qt, gt, ks5, vst5, kw5, vwt5, kcb, vcbt, ovt, tri, wbias, seq):
    B = qt.shape[0]
    tq = NSA2_TQ
    per_b = lambda a: pl.BlockSpec((1,) + a.shape[1:], lambda b, i: (b,) + (0,) * (a.ndim - 1))
    full = lambda a: pl.BlockSpec(a.shape, lambda b, i: (0,) * a.ndim)
    return pl.pallas_call(
        functools.partial(_nsa2_kernel, seq=seq),
        grid=(B, seq // tq),
        in_specs=[pl.BlockSpec((1, NSA_WIDTH, tq), lambda b, i: (b, 0, i)),
                  pl.BlockSpec((1, gt.shape[1], tq), lambda b, i: (b, 0, i)),
                  per_b(ks5), per_b(vst5), per_b(kw5), per_b(vwt5), per_b(kcb), per_b(vcbt),
                  full(ovt), full(tri), full(wbias)],
        out_specs=pl.BlockSpec((1, NSA_WIDTH, tq), lambda b, i: (b, 0, i)),
        out_shape=jax.ShapeDtypeStruct((B, NSA_WIDTH, seq), BF16),
        compiler_params=pltpu.CompilerParams(
            dimension_semantics=("arbitrary", "arbitrary"), vmem_limit_bytes=VMEM_LIMIT),
        name="nsa_attention",
    )(qt, gt, ks5, vst5, kw5, vwt5, kcb, vcbt, ovt, tri, wbias)


def _nsa_from_proj(kvc, ksa, kwa, qt, vst, vwt, gt, P):
    B, Hk, S, _ = ksa.shape
    tq = NSA2_TQ
    Dh = HEAD_DIM
    n_sel = S // SEL_BLOCK
    assert S % tq == 0 and WINDOW % tq == 0 and n_sel == NSA_BIAS_ROWS
    n_half = S // CMP_STRIDE
    kv = kvc.reshape(B, n_half, CMP_STRIDE * KVC_COLS)
    w1 = P['nsa_cmp_w1'][0]
    w1h = w1.reshape(2, 2, CMP_STRIDE, HEAD_DIM, CMP_HIDDEN)
    pick = np.zeros((2, 2 * Hk, Hk), np.float32)
    for j in range(2):
        for hk in range(Hk):
            pick[j, j * Hk + hk, hk] = 1.0
    wab = jnp.einsum('jaldn,jch->jlcdahn', w1h, jnp.asarray(pick)).reshape(
        2, CMP_STRIDE * KVC_COLS, 2 * Hk * CMP_HIDDEN).astype(BF16)
    pos8 = jnp.broadcast_to(P['nsa_cmp_pos'][0].reshape(2, 1, CMP_BLOCK * HEAD_DIM),
                            (2, 8, CMP_BLOCK * HEAD_DIM)).astype(BF16)
    w2 = P['nsa_cmp_w2'][0].astype(BF16)
    kcb, vcbt = _compress_call(kv, wab, pos8, w1.astype(BF16), w2, jnp.swapaxes(w2, 1, 2))

    ks5 = ksa.reshape(B, Hk, S // tq, tq, NSA_AUG)
    Sp = S + WINDOW
    pad_row = np.zeros((WINDOW, NSA_AUG), np.float32)
    pad_row[:, Dh + n_sel] = 1.0
    kw5 = jnp.concatenate([jnp.broadcast_to(jnp.asarray(pad_row, BF16), (B, Hk, WINDOW, NSA_AUG)), kwa],
                          axis=2).reshape(B, Hk, Sp // tq, tq, NSA_AUG)
    vst5 = vst.reshape(B, Hk, Dh, S // tq, tq).swapaxes(2, 3)
    vwt5 = jnp.concatenate([jnp.zeros((B, Hk * Dh, WINDOW), BF16), vwt], axis=2).reshape(
        B, Hk, Dh, Sp // tq, tq).swapaxes(2, 3)

    n_cmp = (S - CMP_BLOCK) // CMP_STRIDE + 1
    cmp_start = np.arange(n_half) * CMP_STRIDE
    sel_start = np.arange(n_sel) * SEL_BLOCK
    overlap = ((cmp_start[:, None] <= sel_start[None, :] + SEL_BLOCK - 1)
               & (cmp_start[:, None] + CMP_BLOCK - 1 >= sel_start[None, :])
               & (np.arange(n_half)[:, None] < n_cmp)).astype(np.float32)
    tri = np.where(np.arange(tq)[:, None] <= np.arange(tq)[None, :], 0.0, NEG_INF).astype(np.float32)
    wbias = np.concatenate([NEG_INF - tri, np.zeros((WINDOW - tq, tq), np.float32), tri], axis=0)
    return _nsa2_call(qt, gt, ks5, vst5, kw5, vwt5, kcb, vcbt, jnp.asarray(overlap.T, BF16),
                      jnp.asarray(tri), jnp.asarray(wbias), S)


def _merge_kernel(x_ref, ya_ref, ybt_ref, gate_ref, wa_ref, wb_ref, wo_ref, o_ref):
    D = x_ref.shape[-1]
    ta = _dot(ya_ref[...], wa_ref[...])
    tb = _dot_tn(ybt_ref[0], wb_ref[...])
    ga = _sigmoid(gate_ref[:, :D].astype(F32))
    gb = _sigmoid(gate_ref[:, D:].astype(F32))
    mix = (ga * ta + gb * tb).astype(BF16)
    o_ref[...] = x_ref[...] + _dot(mix, wo_ref[...])


def _merge_call(x2, ya2, ybt, gates, wa, wb, wo, tm=512):
    T, D = x2.shape
    tiles_per_seq = ybt.shape[2] // tm
    row = lambda w: pl.BlockSpec((tm, w), lambda i: (i, 0))
    full = lambda a: pl.BlockSpec(a.shape, lambda i: (0,) * a.ndim)
    ybt_spec = pl.BlockSpec((1, ybt.shape[1], tm), lambda i: (i // tiles_per_seq, 0, i % tiles_per_seq))
    return pl.pallas_call(
        _merge_kernel,
        grid=(T // tm,),
        in_specs=[row(D), row(ya2.shape[1]), ybt_spec, row(gates.shape[1]),
                  full(wa), full(wb), full(wo)],
        out_specs=row(D),
        out_shape=jax.ShapeDtypeStruct((T, D), F32),
        compiler_params=pltpu.CompilerParams(
            dimension_semantics=("arbitrary",), vmem_limit_bytes=VMEM_LIMIT),
        name="merge",
    )(x2, ya2, ybt, gates, wa, wb, wo)


FFN_HALO = 8


def _rms(x, g):
    return x * lax.rsqrt(jnp.mean(x * x, axis=-1, keepdims=True) + NORM_EPS) * g


def _ffn_kernel(h_ref, halo_ref, p_ref, ln_ref, wup_ref, cw_ref, cb_ref, wdn_ref, wpg_ref, wpp_ref,
                o_ref, up0a_ref, up0b_ref, up1a_ref, up1b_ref, act_ref, *, tiles_per_seq, fc):
    up_refs = ((up0a_ref, up0b_ref), (up1a_ref, up1b_ref))
    tm = act_ref.shape[0]
    D = halo_ref.shape[1]
    V = tm // 8
    d_ff = wdn_ref.shape[0]
    slabs = lambda ref, w: jnp.swapaxes(ref[0], 0, 1).reshape(tm, w)
    h = slabs(h_ref, D)
    first = (pl.program_id(0) % tiles_per_seq) == 0
    halo = jnp.where(first, 0.0, halo_ref[...])
    ln2, ln3, lnf = ln_ref[0:1, :], ln_ref[1:2, :], ln_ref[2:3, :]
    u = jnp.concatenate([_rms(halo, ln2), _rms(h, ln2)], axis=0).astype(BF16)

    n_chunks = d_ff // fc
    sub = lax.broadcasted_iota(jnp.int32, (8, 1), 0)

    def project(c):
        for half in range(2):
            col = half * d_ff + c * fc
            up_refs[c % 2][half][...] = _dot(u, wup_ref[:, col:col + fc])

    def conv(c, half):
        ref = up_refs[c % 2][half]
        col = half * d_ff + c * fc
        halo_up = ref[0:FFN_HALO, :]
        last = lambda k: ref[FFN_HALO + tm - 8 * k:FFN_HALO + tm - 8 * (k - 1), :]
        wrap1 = pltpu.roll(jnp.where(sub == 7, halo_up, last(1)), 1, axis=0)
        wrap2 = pltpu.roll(jnp.where(sub == 7, pltpu.roll(halo_up, 1, axis=0), last(2)), 1, axis=0)
        x0 = ref[FFN_HALO:FFN_HALO + tm, :]
        x1 = jnp.concatenate([wrap1, ref[FFN_HALO:FFN_HALO + tm - 8, :]], axis=0)
        x2 = jnp.concatenate([wrap2, wrap1, ref[FFN_HALO:FFN_HALO + tm - 16, :]], axis=0)
        tap = lambda j: cw_ref[j:j + 1, col:col + fc]
        return cb_ref[:, col:col + fc] + tap(0) * x2 + tap(1) * x1 + tap(2) * x0

    project(0)
    for c in range(n_chunks):
        if c + 1 < n_chunks:
            project(c + 1)
        a = conv(c, 0)
        b = conv(c, 1)
        act_ref[:, c * fc:(c + 1) * fc] = (a * _sigmoid(a) * b).astype(BF16)
    h2 = h + _dot(act_ref[...], wdn_ref[...])
    gate = _sigmoid(_dot(_rms(h2, ln3).astype(BF16), wpg_ref[...]))
    h3 = h2 + gate * _dot(slabs(p_ref, p_ref.shape[3]).astype(BF16), wpp_ref[...])
    o_ref[0] = jnp.swapaxes(_rms(h3, lnf).reshape(V, 8, D), 0, 1)


FFN_TM = 512
FFN_FC = 256


def _ffn_call(h2d, p2d, lns, wup, cw, cb, wdn, wpg, wpp, seq):
    T, D = h2d.shape
    tm, fc = FFN_TM, FFN_FC
    assert CONV_WIDTH == 3 and seq % tm == 0 and wdn.shape[0] % fc == 0
    tiles_per_seq = seq // tm
    runs = lambda x: x.reshape(T // tm, 8, tm // 8, x.shape[1])
    run_spec = lambda w: pl.BlockSpec((1, 8, tm // 8, w), lambda i: (i, 0, 0, 0))
    full = lambda a: pl.BlockSpec(a.shape, lambda i: (0,) * a.ndim, pipeline_mode=pl.Buffered(1))
    halo = pl.BlockSpec((FFN_HALO, D), lambda i: (jnp.maximum(i * (tm // FFN_HALO) - 1, 0), 0))
    out = pl.pallas_call(
        functools.partial(_ffn_kernel, tiles_per_seq=tiles_per_seq, fc=fc),
        grid=(T // tm,),
        in_specs=[run_spec(D), halo, run_spec(p2d.shape[1]), full(lns), full(wup), full(cw), full(cb),
                  full(wdn), full(wpg), full(wpp)],
        out_specs=run_spec(D),
        out_shape=jax.ShapeDtypeStruct((T // tm, 8, tm // 8, D), F32),
        scratch_shapes=[pltpu.VMEM((FFN_HALO + tm, fc), F32)] * 4 + [pltpu.VMEM((tm, wdn.shape[0]), BF16)],
        compiler_params=pltpu.CompilerParams(
            dimension_semantics=("arbitrary",), vmem_limit_bytes=VMEM_LIMIT),
        name="ffn",
    )(runs(h2d), h2d, runs(p2d), lns, wup, cw, cb, wdn, wpg, wpp)
    return out.reshape(T, D)


def _prep_proj_weights(w_in, mu_wag, w1, a1, g1):
    D = w_in.shape[0]
    sizes = (RW_WIDTH, RW_WIDTH, RW_WIDTH, NSA_WIDTH) + (NSA_KV_WIDTH,) * 6 + (3 * NSA_Q_HEADS, D, D)
    offs = np.concatenate([[0], np.cumsum(sizes)])
    part = lambda i, j: w_in[:, offs[i]:offs[j]]
    mw, ma, mg = mu_wag[0][:, None], mu_wag[1][:, None], mu_wag[2][:, None]
    zg = jnp.zeros((D, RW_GATE_PAD - RW_GATE_LORA), F32)
    rw = jnp.concatenate([
        part(0, 3),
        (1.0 - mw) * w1, (1.0 - ma) * a1,
        mw * w1, ma * a1,
        (1.0 - mg) * g1, zg,
        mg * g1, zg], axis=1)
    def widen(w):
        w = w.reshape(D, NSA_KV_HEADS, HEAD_DIM)
        return jnp.concatenate([w, jnp.zeros_like(w)], axis=2).reshape(D, KEY_COLS)

    w_rows = jnp.concatenate([rw, part(4, 6), widen(part(6, 7)), widen(part(8, 9)), part(11, 13)], axis=1)
    w_cols = jnp.concatenate([part(3, 4), part(7, 8), part(9, 10), part(10, 11),
                              jnp.zeros((D, NSA_GATE_ROWS - 3 * NSA_Q_HEADS), F32)], axis=1)
    return w_rows.astype(BF16), w_cols.T.astype(BF16)


def _prep_rwkv_weights(w2, a2, g2):
    z = jnp.zeros_like(w2)
    w2a2 = jnp.concatenate([jnp.concatenate([w2, z], axis=1),
                            jnp.concatenate([z, a2], axis=1)], axis=0).astype(BF16)
    g2p = jnp.concatenate([g2, jnp.zeros((RW_GATE_PAD - RW_GATE_LORA, RW_WIDTH), F32)],
                          axis=0).astype(BF16)
    return w2a2, g2p


def _rwkv_from_proj(rw3, P):
    w2a2, g2p = _prep_rwkv_weights(P['rw_w2'][0], P['rw_a2'][0], P['rw_g2'][0])
    vecs = jnp.stack([P['rw_w0'][0], P['rw_a0'][0], P['rw_k_k'][0], P['rw_k_a'][0],
                      P['rw_r_k'][0].reshape(-1), P['rw_lnx_g'][0], P['rw_lnx_b'][0],
                      jnp.zeros((RW_WIDTH,), F32)], axis=0)
    return _rwkv_call(rw3, P['rw_mu_rkv'][0], vecs, w2a2, g2p)


def kernel(x, p, ln1_g, w_in, rw_mu_rkv, rw_mu_wag, rw_w0, rw_w1, rw_w2, rw_a0, rw_a1, rw_a2, rw_g1, rw_g2, rw_k_k, rw_k_a, rw_r_k, rw_lnx_g, rw_lnx_b, nsa_cmp_pos, nsa_cmp_w1, nsa_cmp_w2, w_out_a, w_out_b, w_out, ln2_g, w_up, conv_w, conv_b, w_down, ln3_g, w_ple_gate, w_ple_proj, ln_f_g):
    B, S, D = x.shape
    T = B * S
    assert w_in.shape[0] == 1, "single-layer block"
    P = dict(rw_mu_rkv=rw_mu_rkv, rw_w0=rw_w0, rw_w2=rw_w2, rw_a0=rw_a0, rw_a2=rw_a2, rw_g2=rw_g2,
             rw_k_k=rw_k_k, rw_k_a=rw_k_a, rw_r_k=rw_r_k, rw_lnx_g=rw_lnx_g, rw_lnx_b=rw_lnx_b,
             nsa_cmp_pos=nsa_cmp_pos, nsa_cmp_w1=nsa_cmp_w1, nsa_cmp_w2=nsa_cmp_w2)
    h = x.reshape(T, D)
    w_rows, w_cols = _prep_proj_weights(w_in[0], rw_mu_wag[0], rw_w1[0], rw_a1[0], rw_g1[0])
    rw, kvc, ksa, kwa, gates, qt, vst, vwt, gt = _proj_call(h, ln1_g[0][None], w_rows, w_cols, S)
    ya = _rwkv_from_proj(rw.reshape(B, S, RW_COLS), P)
    ybt = _nsa_from_proj(kvc, ksa, kwa, qt, vst, vwt, gt, P)
    h1 = _merge_call(h, ya.reshape(T, RW_WIDTH), ybt, gates,
                     w_out_a[0].astype(BF16), w_out_b[0].astype(BF16), w_out[0].astype(BF16))
    lns = jnp.stack([ln2_g[0], ln3_g[0], ln_f_g], axis=0)
    out = _ffn_call(h1, p[0].reshape(T, -1), lns, w_up[0].astype(BF16), conv_w[0], conv_b[0][None],
                    w_down[0].astype(BF16), w_ple_gate[0].astype(BF16), w_ple_proj[0].astype(BF16), S)
    return out.reshape(B, S, D)
```

```python
import functools

import numpy as np
import jax
import jax.numpy as jnp
from jax import lax
from jax.experimental import pallas as pl
from jax.experimental.pallas import tpu as pltpu

F32 = jnp.float32
BF16 = jnp.bfloat16

HEAD_DIM = 64
NORM_EPS = 1e-6
NEG_INF = -1e30

RW_HEADS = 8
RW_WIDTH = RW_HEADS * HEAD_DIM
RW_DECAY_LORA = 64
RW_AAA_LORA = 64
RW_GATE_LORA = 160
RW_LNX_EPS = 64e-5
RW_CHUNK = 64
RW_GROUP = 4
RW_GROUP_W = RW_GROUP * HEAD_DIM
RW_GATE_PAD = 256

NSA_Q_HEADS = 8
NSA_KV_HEADS = 2
NSA_GROUP = NSA_Q_HEADS // NSA_KV_HEADS
NSA_WIDTH = NSA_Q_HEADS * HEAD_DIM
NSA_KV_WIDTH = NSA_KV_HEADS * HEAD_DIM
CMP_BLOCK = 32
CMP_STRIDE = 16
CMP_HIDDEN = 128
SEL_BLOCK = 64
SEL_TOP = 16
SEL_FORCE_SCORE = 1e4
WINDOW = 512

CONV_WIDTH = 3

RW_COLS = 3 * RW_WIDTH + 2 * 128 + 2 * RW_GATE_PAD
GATE_COLS = 2 * 1024

VMEM_LIMIT = 56 * 1024 * 1024


def _dot(a, b):
    return jnp.dot(a, b, preferred_element_type=F32)


def _dot_nt(a, b):
    return lax.dot_general(a, b, (((1,), (1,)), ((), ())), preferred_element_type=F32)


def _dot_tn(a, b):
    return lax.dot_general(a, b, (((0,), (0,)), ((), ())), preferred_element_type=F32)


def _split2(x):
    hi = x.astype(BF16)
    lo = (x - hi.astype(F32)).astype(BF16)
    return hi, lo


def _split3(x):
    hi = x.astype(BF16)
    r1 = x - hi.astype(F32)
    mid = r1.astype(BF16)
    lo = (r1 - mid.astype(F32)).astype(BF16)
    return hi, mid, lo


def _sigmoid(x):
    return 1.0 / (1.0 + jnp.exp(-x))


def _softplus(x):
    return jnp.maximum(x, 0.0) + jnp.log(1.0 + jnp.exp(-jnp.abs(x)))


PROJ_TM = 512
PROJ_CHUNK = 768
KVC_COLS = 2 * NSA_KV_WIDTH
KEY_COLS = NSA_KV_HEADS * 128
NSA_GATE_ROWS = 32
T_ROWS = NSA_WIDTH + 2 * NSA_KV_WIDTH + NSA_GATE_ROWS


def _proj_kernel(x_ref, g_ref, w_ref, wt_ref, rw_ref, kvc_ref, ksa_ref, kwa_ref, gate_ref,
                 qt_ref, vst_ref, vwt_ref, gt_ref, *, tiles_per_seq):
    tm = x_ref.shape[0]
    x = x_ref[...]
    ms = jnp.mean(x * x, axis=-1, keepdims=True)
    u = (x * lax.rsqrt(ms + NORM_EPS) * g_ref[...]).astype(BF16)

    col = 0
    for o_ref in (rw_ref, kvc_ref):
        width = o_ref.shape[-1]
        for c in range(0, width, PROJ_CHUNK):
            hi = min(c + PROJ_CHUNK, width)
            o_ref[:, c:hi] = _dot(u, w_ref[:, col + c:col + hi]).astype(o_ref.dtype)
        col += width

    s0 = (pl.program_id(0) % tiles_per_seq) * tm
    blk = (s0 + lax.broadcasted_iota(jnp.int32, (tm, KEY_COLS), 0)) // SEL_BLOCK
    lane = lax.broadcasted_iota(jnp.int32, (tm, KEY_COLS), 1) % 128
    onehot = jnp.where(lane - HEAD_DIM == blk, 1.0, 0.0)
    ks = _dot(u, w_ref[:, col:col + KEY_COLS]) + onehot
    kw = _dot(u, w_ref[:, col + KEY_COLS:col + 2 * KEY_COLS])
    for hk in range(NSA_KV_HEADS):
        ksa_ref[0, hk] = ks[:, hk * 128:(hk + 1) * 128].astype(ksa_ref.dtype)
        kwa_ref[0, hk] = kw[:, hk * 128:(hk + 1) * 128].astype(kwa_ref.dtype)
    col += 2 * KEY_COLS

    width = gate_ref.shape[-1]
    for c in range(0, width, PROJ_CHUNK):
        hi = min(c + PROJ_CHUNK, width)
        gate_ref[:, c:hi] = _dot(u, w_ref[:, col + c:col + hi]).astype(gate_ref.dtype)

    t = _dot_nt(wt_ref[...], u)
    row = 0
    for o_ref in (qt_ref, vst_ref, vwt_ref, gt_ref):
        n = o_ref.shape[1]
        o_ref[0] = t[row:row + n].astype(o_ref.dtype)
        row += n


def _proj_call(x2, g, w_all, wt_all, seq):
    T, D = x2.shape
    tm = PROJ_TM
    B = T // seq
    tps = seq // tm
    rows = lambda w: pl.BlockSpec((tm, w), lambda i: (i, 0))
    full = lambda a: pl.BlockSpec(a.shape, lambda i: (0,) * a.ndim)
    keys = pl.BlockSpec((1, NSA_KV_HEADS, tm, 128), lambda i: (i // tps, 0, i % tps, 0))
    tcols = lambda n: pl.BlockSpec((1, n, tm), lambda i: (i // tps, 0, i % tps))
    sds = jax.ShapeDtypeStruct
    return pl.pallas_call(
        functools.partial(_proj_kernel, tiles_per_seq=tps),
        grid=(T // tm,),
        in_specs=[rows(D), full(g), full(w_all), full(wt_all)],
        out_specs=[rows(RW_COLS), rows(KVC_COLS), keys, keys, rows(GATE_COLS),
                   tcols(NSA_WIDTH), tcols(NSA_KV_WIDTH), tcols(NSA_KV_WIDTH), tcols(NSA_GATE_ROWS)],
        out_shape=[sds((T, RW_COLS), BF16), sds((T, KVC_COLS), BF16),
                   sds((B, NSA_KV_HEADS, seq, 128), BF16), sds((B, NSA_KV_HEADS, seq, 128), BF16),
                   sds((T, GATE_COLS), BF16),
                   sds((B, NSA_WIDTH, seq), BF16), sds((B, NSA_KV_WIDTH, seq), BF16),
                   sds((B, NSA_KV_WIDTH, seq), BF16), sds((B, NSA_GATE_ROWS, seq), BF16)],
        compiler_params=pltpu.CompilerParams(
            dimension_semantics=("arbitrary",), vmem_limit_bytes=VMEM_LIMIT),
        name="proj",
    )(x2, g, w_all, wt_all)


def _rwkv_kernel(x_ref, mu_ref, vec_ref, w2a2_ref, g2_ref, o_ref, state_ref, prev_ref):
    C = RW_CHUNK
    GW = RW_GROUP_W
    W = RW_WIDTH
    NB = x_ref.shape[0]
    R = NB * C
    t_idx = pl.program_id(1)
    ops = {}

    @pl.when(t_idx == 0)
    def _():
        state_ref[...] = jnp.zeros_like(state_ref)
        prev_ref[...] = jnp.zeros_like(prev_ref)

    mu = mu_ref[...]
    w0, a0, k_k, k_a, r_k, lnx_g, lnx_b = (vec_ref[i:i + 1, :] for i in range(7))

    gr = lax.broadcasted_iota(jnp.int32, (GW, GW), 0) // HEAD_DIM
    gc = lax.broadcasted_iota(jnp.int32, (GW, GW), 1) // HEAD_DIM
    blk = gr == gc
    ones_bd = jnp.where(blk, 1.0, 0.0).astype(BF16)

    def headsums(zs):
        parts = []
        for z in zs:
            hi, lo = _split2(z)
            parts += [hi[:, :GW], hi[:, GW:], lo[:, :GW], lo[:, GW:]]
        s = _dot(jnp.concatenate(parts, axis=0), ones_bd)
        outs = []
        for i in range(len(zs)):
            q = s[4 * R * i:4 * R * (i + 1)]
            q = q[:2 * R] + q[2 * R:]
            outs.append(jnp.concatenate([q[:R], q[R:]], axis=1))
        return outs

    t_n = lax.broadcasted_iota(jnp.int32, (C, GW), 0)
    s_n = lax.broadcasted_iota(jnp.int32, (C, GW), 1) % HEAD_DIM
    strict = t_n > s_n
    incl = t_n >= s_n
    eye_n = jnp.where(t_n == s_n, 1.0, 0.0)

    def bd(z):
        z4 = jnp.concatenate([z] * RW_GROUP, axis=0).astype(BF16)
        return jnp.where(blk, z4, jnp.zeros_like(z4))

    def prepare():
        x = x_ref[...].reshape(R, RW_COLS).astype(F32)
        rolled = pltpu.roll(x, 1, axis=0)
        row8 = lax.broadcasted_iota(jnp.int32, (8, 1), 0)
        pieces = []
        for bi in range(NB):
            pieces.append(jnp.where(row8 == 0, prev_ref[bi, 0:1, :], rolled[bi * C:bi * C + 8]))
            pieces.append(rolled[bi * C + 8:(bi + 1) * C])
            prev_ref[bi, 0:1, :] = x[(bi + 1) * C - 1:(bi + 1) * C, :]
        xs = jnp.concatenate(pieces, axis=0)

        def lerp(j):
            cur = x[:, j * W:(j + 1) * W]
            return cur + (xs[:, j * W:(j + 1) * W] - cur) * mu[j:j + 1, :]

        r, k, v = lerp(0), lerp(1), lerp(2)
        o = 3 * W
        pre_a = x[:, o:o + 128] + xs[:, o + 128:o + 256]
        lane = lax.broadcasted_iota(jnp.int32, (R, 128), 1)
        h_a = jnp.where(lane < RW_DECAY_LORA, jnp.tanh(pre_a), pre_a)
        lwa = _dot(h_a.astype(BF16), w2a2_ref[...])
        o += 256
        pre_g = x[:, o:o + RW_GATE_PAD] + xs[:, o + RW_GATE_PAD:o + 2 * RW_GATE_PAD]
        g = _dot(_sigmoid(pre_g).astype(BF16), g2_ref[...])
        w = -_softplus(-(w0 + lwa[:, :W])) - 0.5
        ld = -jnp.exp(w)
        a = _sigmoid(a0 + lwa[:, W:])
        kkr = k * k_k
        k2 = k * (1.0 + (a - 1.0) * k_a)
        kk_ss, bonus = headsums([kkr * kkr, r * k2 * r_k])
        kk = kkr / jnp.maximum(jnp.sqrt(kk_ss), 1e-12)
        b = kk * a
        tr = lax.broadcasted_iota(jnp.int32, (R, R), 0)
        tc = lax.broadcasted_iota(jnp.int32, (R, R), 1)
        tri = jnp.where((tr >= tc) & (tr // C == tc // C), 1.0, 0.0).astype(BF16)
        l_inc = _dot(tri, jnp.concatenate(_split3(ld), axis=1))
        l_inc = l_inc[:, :W] + l_inc[:, W:2 * W] + l_inc[:, 2 * W:]
        l_end = [l_inc[(bi + 1) * C - 1:(bi + 1) * C] for bi in range(NB)]
        e_neg = jnp.exp(-l_inc)
        e_tail = jnp.concatenate([jnp.exp(l_end[bi] - l_inc[bi * C:(bi + 1) * C]) for bi in range(NB)],
                                 axis=0)
        new_opb = (-kk * jnp.exp(l_inc - ld), r * jnp.exp(l_inc), b * e_neg, k2 * e_neg,
                   b * e_tail, k2 * e_tail)
        ops['matmul'] = tuple(z.astype(BF16) for z in new_opb)
        ops['output'] = (v, bonus, g)
        ops['decay'] = jnp.concatenate([jnp.broadcast_to(jnp.exp(le), (8, W)) for le in l_end], axis=0)

    def consume():
        n_grp = W // GW
        chains = [(bi, gi) for bi in range(NB) for gi in range(n_grp)]
        cut = lambda z, c: z[c[0] * C:(c[0] + 1) * C, c[1] * GW:(c[1] + 1) * GW]
        each = lambda f, *lists: [f(*args) for args in zip(*lists)]

        p_a_hat, p_r_hat, p_b_hat, p_k_hat, p_b_tail, p_k_tail = ops['matmul']
        p_v, p_bonus, p_g = ops['output']
        p_elc = ops['decay']
        a_h = [cut(p_a_hat, c) for c in chains]
        r_h = [cut(p_r_hat, c) for c in chains]
        ar = each(lambda x1, x2: jnp.concatenate([x1, x2], axis=0), a_h, r_h)
        m1 = each(_dot_nt, ar, [bd(cut(p_b_hat, c)) for c in chains])
        m2 = each(_dot_nt, ar, [bd(cut(p_k_hat, c)) for c in chains])
        m_ab = [jnp.where(strict, m[:C], 0.0) for m in m1]
        m_rb = [jnp.where(incl, m[C:], 0.0) for m in m1]
        m_ak = [jnp.where(strict, m[:C], 0.0) for m in m2]
        m_rk = [jnp.where(incl, m[C:], 0.0) for m in m2]

        tinv = [eye_n + m for m in m_ab]
        p = each(lambda m: _dot(m.astype(BF16), bd(m)), m_ab)
        power = 2
        while 2 * power < C:
            tp = each(lambda t, q: _dot(jnp.concatenate([t, q], axis=0).astype(BF16), bd(q)), tinv, p)
            tinv = each(lambda t, x1: t + x1[:C], tinv, tp)
            p = [x1[C:] for x1 in tp]
            power *= 2
        tinv = each(lambda t, q: t + _dot(t.astype(BF16), bd(q)), tinv, p)

        s_old = [state_ref[i * GW:(i + 1) * GW, :] for i in range(len(chains))]
        s_bf = [s.astype(BF16) for s in s_old]
        vg = [cut(p_v, c) for c in chains]
        bd_v = [bd(x1) for x1 in vg]
        xz = each(lambda x1, s, m, bv: _dot_nt(x1, s) + _dot(m.astype(BF16), bv), a_h, s_bf, m_ak, bd_v)
        u = each(lambda t, x1: _dot(t.astype(BF16), bd(x1)), tinv, xz)
        y = each(lambda x1, s, mb, mk, uu, bv:
                 _dot_nt(x1, s) + _dot(jnp.concatenate([mb, mk], axis=1).astype(BF16),
                                       jnp.concatenate([bd(uu), bv], axis=0)),
                 r_h, s_bf, m_rb, m_rk, u, bd_v)
        new_states = []
        for c, uu, vv, s in zip(chains, u, vg, s_old):
            upd = _dot_tn(jnp.concatenate([uu, vv], axis=0).astype(BF16),
                          jnp.concatenate([cut(p_b_tail, c), cut(p_k_tail, c)], axis=0))
            decay = p_elc[c[0] * 8:c[0] * 8 + 1, c[1] * GW:(c[1] + 1) * GW]
            new_states.append(s * decay + jnp.where(blk, upd, 0.0))
        state_ref[...] = jnp.concatenate(new_states, axis=0)
        y_rows = [jnp.concatenate(y[bi * n_grp:(bi + 1) * n_grp], axis=1) for bi in range(NB)]
        y = jnp.concatenate(y_rows, axis=0)
        mean = headsums([y])[0] * (1.0 / HEAD_DIM)
        yc = y - mean
        var = headsums([yc * yc])[0] * (1.0 / HEAD_DIM)
        yn = yc * lax.rsqrt(var + RW_LNX_EPS) * lnx_g + lnx_b
        yn = yn + p_bonus * p_v
        o_ref[...] = (yn * p_g).reshape(NB, C, W).astype(o_ref.dtype)

    prepare()
    consume()


RW_SEQS_PER_STEP = 4


def _rwkv_call(rw3, mu, vecs, w2a2, g2p):
    B, S, _ = rw3.shape
    C = RW_CHUNK
    nb = RW_SEQS_PER_STEP if B % RW_SEQS_PER_STEP == 0 else 1
    n_groups = RW_WIDTH // RW_GROUP_W
    return pl.pallas_call(
        _rwkv_kernel,
        grid=(B // nb, S // C),
        in_specs=[
            pl.BlockSpec((nb, C, RW_COLS), lambda b, t: (b, t, 0)),
            pl.BlockSpec(mu.shape, lambda b, t: (0, 0)),
            pl.BlockSpec(vecs.shape, lambda b, t: (0, 0)),
            pl.BlockSpec(w2a2.shape, lambda b, t: (0, 0)),
            pl.BlockSpec(g2p.shape, lambda b, t: (0, 0)),
        ],
        out_specs=pl.BlockSpec((nb, C, RW_WIDTH), lambda b, t: (b, t, 0)),
        out_shape=jax.ShapeDtypeStruct((B, S, RW_WIDTH), BF16),
        scratch_shapes=[
            pltpu.VMEM((nb * n_groups * RW_GROUP_W, RW_GROUP_W), F32),
            pltpu.VMEM((nb, 8, RW_COLS), F32),
        ],
        compiler_params=pltpu.CompilerParams(
            dimension_semantics=("arbitrary", "arbitrary"), vmem_limit_bytes=VMEM_LIMIT),
        name="rwkv",
    )(rw3, mu, vecs, w2a2, g2p)


def _compress_kernel(kv_ref, wab_ref, pos_ref, w1_ref, w2_ref, w2t_ref, kcb_ref, vcbt_ref):
    n_half = kv_ref.shape[1]
    for j in range(2):
        pab = _dot(kv_ref[0], wab_ref[j])
        half = NSA_KV_HEADS * CMP_HIDDEN
        pa, pb = pab[:, :half], pab[:, half:]
        pb = pltpu.roll(pb, n_half - 1, axis=0)
        pos_term = _dot(pos_ref[j], w1_ref[j])[0:1]
        hid = pa + pb + jnp.concatenate([pos_term] * NSA_KV_HEADS, axis=1)
        act = (hid * _sigmoid(hid)).astype(BF16)
        for hk in range(NSA_KV_HEADS):
            a_h = act[:, hk * CMP_HIDDEN:(hk + 1) * CMP_HIDDEN]
            if j == 0:
                kcb_ref[0, hk] = _dot(a_h, w2_ref[j]).astype(kcb_ref.dtype)
            else:
                vcbt_ref[0, hk] = _dot_nt(w2t_ref[j], a_h).astype(vcbt_ref.dtype)


def _compress_call(kv, wab, pos8, w1, w2, w2t):
    B, n_half, width = kv.shape
    full = lambda a: pl.BlockSpec(a.shape, lambda b: (0,) * a.ndim)
    return pl.pallas_call(
        _compress_kernel,
        grid=(B,),
        in_specs=[pl.BlockSpec((1, n_half, width), lambda b: (b, 0, 0)),
                  full(wab), full(pos8), full(w1), full(w2), full(w2t)],
        out_specs=[pl.BlockSpec((1, NSA_KV_HEADS, n_half, HEAD_DIM), lambda b: (b, 0, 0, 0)),
                   pl.BlockSpec((1, NSA_KV_HEADS, HEAD_DIM, n_half), lambda b: (b, 0, 0, 0))],
        out_shape=[jax.ShapeDtypeStruct((B, NSA_KV_HEADS, n_half, HEAD_DIM), BF16),
                   jax.ShapeDtypeStruct((B, NSA_KV_HEADS, HEAD_DIM, n_half), BF16)],
        compiler_params=pltpu.CompilerParams(
            dimension_semantics=("arbitrary",), vmem_limit_bytes=VMEM_LIMIT),
        name="nsa_compress",
    )(kv, wab, pos8, w1, w2, w2t)


NSA2_TQ = 256
NSA_AUG = 128
NSA_BIAS_ROWS = 32
LOG2_E = 1.4426950408889634


def _nsa2_kernel(qt_ref, gt_ref, ks_ref, vst_ref, kw_ref, vwt_ref, kcb_ref, vcbt_ref, ovt_ref,
                 tri_ref, wbias_ref, o_ref, *, seq):
    tq = NSA2_TQ
    G = NSA_GROUP
    R = G * tq
    n_half = kcb_ref.shape[2]
    n_cmp = n_half - 1
    n_sel = seq // SEL_BLOCK
    n_top = min(SEL_TOP, n_sel)
    n_wchunks = WINDOW // tq + 1
    step = pl.program_id(1)
    q0 = step * tq

    t_lane = q0 + lax.broadcasted_iota(jnp.int32, (1, R), 1) % tq
    gates = _sigmoid(gt_ref[0].astype(F32))
    pad_rows = jnp.zeros((NSA_AUG - HEAD_DIM - NSA_BIAS_ROWS, tq), BF16)

    hrow = lambda hk, g: slice((hk * G + g) * HEAD_DIM, (hk * G + g + 1) * HEAD_DIM)
    o_cmp, qaug = {}, {}
    for hk in range(NSA_KV_HEADS):
        q64 = jnp.concatenate([qt_ref[0, hrow(hk, g), :] for g in range(G)], axis=1)
        q64 = q64 * jnp.asarray(HEAD_DIM ** -0.5, BF16)

        cidx = lax.broadcasted_iota(jnp.int32, (n_half, R), 0)
        cvalid = (cidx * CMP_STRIDE + (CMP_BLOCK - 1) <= t_lane) & (cidx < n_cmp)
        s = jnp.where(cvalid, _dot(kcb_ref[0, hk], q64), NEG_INF)
        m = jnp.max(s, axis=0, keepdims=True)
        e = jnp.where(cvalid, jnp.exp(s - m), 0.0)
        l = jnp.sum(e, axis=0, keepdims=True)
        p_c = e / jnp.where(l > 0.0, l, 1.0)
        o_c = _dot(vcbt_ref[0, hk], p_c.astype(BF16))

        psum = p_c[:, 0:tq]
        for g in range(1, G):
            psum = psum + p_c[:, g * tq:(g + 1) * tq]
        hi, lo = _split2(psum)
        imp2 = _dot(ovt_ref[...], jnp.concatenate([hi, lo], axis=1))
        imp = imp2[:, :tq] + imp2[:, tq:]
        jblk = lax.broadcasted_iota(jnp.int32, (n_sel, tq), 0)
        cur = (q0 + lax.broadcasted_iota(jnp.int32, (n_sel, tq), 1)) // SEL_BLOCK
        forced = (jblk == 0) | (jblk == cur) | (jblk == cur - 1)
        score = jnp.where(forced, SEL_FORCE_SCORE, jnp.where(jblk <= cur, imp, -1.0))
        rank = jnp.zeros((n_sel, tq), F32)
        for j in range(n_sel):
            sj = score[j:j + 1, :]
            ahead = (sj > score) | ((sj == score) & (j < jblk))
            rank = rank + jnp.where(ahead, 1.0, 0.0)
        sel_bias = jnp.where(rank < n_top, 0.0, NEG_INF).astype(BF16)
        q_l2 = (q64.astype(F32) * LOG2_E).astype(BF16)
        for g in range(G):
            o_cmp[hk, g] = o_c[:, g * tq:(g + 1) * tq]
            qaug[hk, g] = jnp.concatenate([q_l2[:, g * tq:(g + 1) * tq], sel_bias, pad_rows], axis=0)

    chains = [(hk, g) for hk in range(NSA_KV_HEADS) for g in range(G)]

    def softmax_pv(s_list, vt_of, carry=None):
        m_blk = [jnp.max(s, axis=0, keepdims=True) for s in s_list]
        if carry is None:
            m_new = m_blk
        else:
            m_new = [jnp.maximum(c[0], mb) for c, mb in zip(carry, m_blk)]
        p = [jnp.exp2(s - mn) for s, mn in zip(s_list, m_new)]
        l_blk = [jnp.sum(x, axis=0, keepdims=True) for x in p]
        pv = [_dot(vt_of(c), x.astype(BF16)) for c, x in zip(chains, p)]
        if carry is None:
            return [(mn, lb, a) for mn, lb, a in zip(m_new, l_blk, pv)]
        alpha = [jnp.exp2(c[0] - mn) for c, mn in zip(carry, m_new)]
        return [(mn, c[1] * al + lb, c[2] * al + a)
                for c, mn, al, lb, a in zip(carry, m_new, alpha, l_blk, pv)]

    def vcols(ref, hk, j):
        return ref[0, hk * HEAD_DIM:(hk + 1) * HEAD_DIM, pl.ds(pl.multiple_of(j * tq, tq), tq)]

    wchunk = [jnp.maximum(step - (n_wchunks - 1) + w, 0) for w in range(n_wchunks)]
    kw_rows = [jnp.concatenate([kw_ref[0, hk, j] for j in wchunk], axis=0) for hk in range(NSA_KV_HEADS)]
    vw_cols = [jnp.concatenate([vcols(vwt_ref, hk, j) for j in wchunk], axis=1)
               for hk in range(NSA_KV_HEADS)]
    wbias = wbias_ref[jnp.minimum(step, n_wchunks - 1)]
    s_win = [_dot(kw_rows[hk], qaug[hk, g]) + wbias for hk, g in chains]
    win = softmax_pv(s_win, lambda c: vw_cols[c[0]])

    tri = tri_ref[...]
    s_diag = [_dot(ks_ref[0, hk, step], qaug[hk, g]) + tri for hk, g in chains]
    carry = softmax_pv(s_diag, lambda c: vcols(vst_ref, c[0], step))

    def body(j, flat):
        carry = [tuple(flat[3 * i:3 * i + 3]) for i in range(len(chains))]
        s_j = [_dot(ks_ref[0, hk, j], qaug[hk, g]) for hk, g in chains]
        new = softmax_pv(s_j, lambda c: vcols(vst_ref, c[0], j), carry)
        return tuple(x for c in new for x in c)

    flat = lax.fori_loop(0, step, body, tuple(x for c in carry for x in c))
    sel = [tuple(flat[3 * i:3 * i + 3]) for i in range(len(chains))]

    for i, (hk, g) in enumerate(chains):
        gate = lambda j: gates[(hk * G + g) * 3 + j:(hk * G + g) * 3 + j + 1, :]
        out = (gate(0) * o_cmp[hk, g] + gate(1) * (sel[i][2] / sel[i][1])
               + gate(2) * (win[i][2] / win[i][1]))
        o_ref[0, hrow(hk, g), :] = out.astype(o_ref.dtype)


def _nsa2_call(qt, gt, ks5, vst, kw5, vwt, kcb, vcbt, ovt, tri, wbias, seq):
    B = qt.shape[0]
    tq = NSA2_TQ
    per_b = lambda a: pl.BlockSpec((1,) + a.shape[1:], lambda b, i: (b,) + (0,) * (a.ndim - 1))
    full = lambda a: pl.BlockSpec(a.shape, lambda b, i: (0,) * a.ndim)
    return pl.pallas_call(
        functools.partial(_nsa2_kernel, seq=seq),
        grid=(B, seq // tq),
        in_specs=[pl.BlockSpec((1, NSA_WIDTH, tq), lambda b, i: (b, 0, i)),
                  pl.BlockSpec((1, gt.shape[1], tq), lambda b, i: (b, 0, i)),
                  per_b(ks5), per_b(vst), per_b(kw5), per_b(vwt), per_b(kcb), per_b(vcbt),
                  full(ovt), full(tri), full(wbias)],
        out_specs=pl.BlockSpec((1, NSA_WIDTH, tq), lambda b, i: (b, 0, i)),
        out_shape=jax.ShapeDtypeStruct((B, NSA_WIDTH, seq), BF16),
        compiler_params=pltpu.CompilerParams(
            dimension_semantics=("arbitrary", "arbitrary"), vmem_limit_bytes=VMEM_LIMIT),
        name="nsa_attention",
    )(qt, gt, ks5, vst, kw5, vwt, kcb, vcbt, ovt, tri, wbias)


def _nsa_from_proj(kvc, ksa, kwa, qt, vst, vwt, gt, P):
    B, Hk, S, _ = ksa.shape
    tq = NSA2_TQ
    Dh = HEAD_DIM
    n_sel = S // SEL_BLOCK
    assert S % tq == 0 and WINDOW % tq == 0 and n_sel == NSA_BIAS_ROWS
    n_half = S // CMP_STRIDE
    kv = kvc.reshape(B, n_half, CMP_STRIDE * KVC_COLS)
    w1 = P['nsa_cmp_w1'][0]
    w1h = w1.reshape(2, 2, CMP_STRIDE, HEAD_DIM, CMP_HIDDEN)
    pick = np.zeros((2, 2 * Hk, Hk), np.float32)
    for j in range(2):
        for hk in range(Hk):
            pick[j, j * Hk + hk, hk] = 1.0
    wab = jnp.einsum('jaldn,jch->jlcdahn', w1h, jnp.asarray(pick)).reshape(
        2, CMP_STRIDE * KVC_COLS, 2 * Hk * CMP_HIDDEN).astype(BF16)
    pos8 = jnp.broadcast_to(P['nsa_cmp_pos'][0].reshape(2, 1, CMP_BLOCK * HEAD_DIM),
                            (2, 8, CMP_BLOCK * HEAD_DIM)).astype(BF16)
    w2 = P['nsa_cmp_w2'][0].astype(BF16)
    kcb, vcbt = _compress_call(kv, wab, pos8, w1.astype(BF16), w2, jnp.swapaxes(w2, 1, 2))

    ks5 = ksa.reshape(B, Hk, S // tq, tq, NSA_AUG)
    kw5 = kwa.reshape(B, Hk, S // tq, tq, NSA_AUG)

    n_cmp = (S - CMP_BLOCK) // CMP_STRIDE + 1
    cmp_start = np.arange(n_half) * CMP_STRIDE
    sel_start = np.arange(n_sel) * SEL_BLOCK
    overlap = ((cmp_start[:, None] <= sel_start[None, :] + SEL_BLOCK - 1)
               & (cmp_start[:, None] + CMP_BLOCK - 1 >= sel_start[None, :])
               & (np.arange(n_half)[:, None] < n_cmp)).astype(np.float32)
    tri = np.where(np.arange(tq)[:, None] <= np.arange(tq)[None, :], 0.0, NEG_INF).astype(np.float32)
    n_w = WINDOW // tq
    masked = np.full((tq, tq), NEG_INF, np.float32)
    clear = np.zeros((tq, tq), np.float32)
    wbias = np.stack([np.concatenate([masked] * (n_w - v) + [clear] * v + [tri], axis=0) for v in range(n_w)]
                     + [np.concatenate([NEG_INF - tri] + [clear] * (n_w - 1) + [tri], axis=0)])
    return _nsa2_call(qt, gt, ks5, vst, kw5, vwt, kcb, vcbt, jnp.asarray(overlap.T, BF16),
                      jnp.asarray(tri), jnp.asarray(wbias), S)


def _merge_kernel(x_ref, ya_ref, ybt_ref, gate_ref, wa_ref, wb_ref, wo_ref, o_ref):
    D = x_ref.shape[-1]
    ta = _dot(ya_ref[...], wa_ref[...])
    tb = _dot_tn(ybt_ref[0], wb_ref[...])
    ga = _sigmoid(gate_ref[:, :D].astype(F32))
    gb = _sigmoid(gate_ref[:, D:].astype(F32))
    mix = (ga * ta + gb * tb).astype(BF16)
    o_ref[...] = x_ref[...] + _dot(mix, wo_ref[...])


def _merge_call(x2, ya2, ybt, gates, wa, wb, wo, tm=512):
    T, D = x2.shape
    tiles_per_seq = ybt.shape[2] // tm
    row = lambda w: pl.BlockSpec((tm, w), lambda i: (i, 0))
    full = lambda a: pl.BlockSpec(a.shape, lambda i: (0,) * a.ndim)
    ybt_spec = pl.BlockSpec((1, ybt.shape[1], tm), lambda i: (i // tiles_per_seq, 0, i % tiles_per_seq))
    return pl.pallas_call(
        _merge_kernel,
        grid=(T // tm,),
        in_specs=[row(D), row(ya2.shape[1]), ybt_spec, row(gates.shape[1]),
                  full(wa), full(wb), full(wo)],
        out_specs=row(D),
        out_shape=jax.ShapeDtypeStruct((T, D), F32),
        compiler_params=pltpu.CompilerParams(
            dimension_semantics=("arbitrary",), vmem_limit_bytes=VMEM_LIMIT),
        name="merge",
    )(x2, ya2, ybt, gates, wa, wb, wo)


FFN_HALO = 8


def _rms(x, g):
    return x * lax.rsqrt(jnp.mean(x * x, axis=-1, keepdims=True) + NORM_EPS) * g


def _ffn_kernel(h_ref, halo_ref, p_ref, ln_ref, wup_ref, cw_ref, cb_ref, wdn_ref, wpg_ref, wpp_ref,
                o_ref, up0a_ref, up0b_ref, up1a_ref, up1b_ref, act_ref, *, tiles_per_seq, fc):
    up_refs = ((up0a_ref, up0b_ref), (up1a_ref, up1b_ref))
    tm = act_ref.shape[0]
    D = halo_ref.shape[1]
    V = tm // 8
    d_ff = wdn_ref.shape[0]
    slabs = lambda ref, w: jnp.swapaxes(ref[0], 0, 1).reshape(tm, w)
    h = slabs(h_ref, D)
    first = (pl.program_id(0) % tiles_per_seq) == 0
    halo = jnp.where(first, 0.0, halo_ref[...])
    ln2, ln3, lnf = ln_ref[0:1, :], ln_ref[1:2, :], ln_ref[2:3, :]
    u = jnp.concatenate([_rms(halo, ln2), _rms(h, ln2)], axis=0).astype(BF16)

    n_chunks = d_ff // fc
    sub = lax.broadcasted_iota(jnp.int32, (8, 1), 0)

    def project(c):
        for half in range(2):
            col = half * d_ff + c * fc
            up_refs[c % 2][half][...] = _dot(u, wup_ref[:, col:col + fc])

    def conv(c, half):
        ref = up_refs[c % 2][half]
        col = half * d_ff + c * fc
        halo_up = ref[0:FFN_HALO, :]
        last = lambda k: ref[FFN_HALO + tm - 8 * k:FFN_HALO + tm - 8 * (k - 1), :]
        wrap1 = pltpu.roll(jnp.where(sub == 7, halo_up, last(1)), 1, axis=0)
        wrap2 = pltpu.roll(jnp.where(sub == 7, pltpu.roll(halo_up, 1, axis=0), last(2)), 1, axis=0)
        x0 = ref[FFN_HALO:FFN_HALO + tm, :]
        x1 = jnp.concatenate([wrap1, ref[FFN_HALO:FFN_HALO + tm - 8, :]], axis=0)
        x2 = jnp.concatenate([wrap2, wrap1, ref[FFN_HALO:FFN_HALO + tm - 16, :]], axis=0)
        tap = lambda j: cw_ref[j:j + 1, col:col + fc]
        return cb_ref[:, col:col + fc] + tap(0) * x2 + tap(1) * x1 + tap(2) * x0

    project(0)
    for c in range(n_chunks):
        if c + 1 < n_chunks:
            project(c + 1)
        a = conv(c, 0)
        b = conv(c, 1)
        act_ref[:, c * fc:(c + 1) * fc] = (a * _sigmoid(a) * b).astype(BF16)
    h2 = h + _dot(act_ref[...], wdn_ref[...])
    gate = _sigmoid(_dot(_rms(h2, ln3).astype(BF16), wpg_ref[...]))
    h3 = h2 + gate * _dot(slabs(p_ref, p_ref.shape[3]).astype(BF16), wpp_ref[...])
    o_ref[0] = jnp.swapaxes(_rms(h3, lnf).reshape(V, 8, D), 0, 1)


FFN_TM = 512
FFN_FC = 256


def _ffn_call(h2d, p2d, lns, wup, cw, cb, wdn, wpg, wpp, seq):
    T, D = h2d.shape
    tm, fc = FFN_TM, FFN_FC
    assert CONV_WIDTH == 3 and seq % tm == 0 and wdn.shape[0] % fc == 0
    tiles_per_seq = seq // tm
    runs = lambda x: x.reshape(T // tm, 8, tm // 8, x.shape[1])
    run_spec = lambda w: pl.BlockSpec((1, 8, tm // 8, w), lambda i: (i, 0, 0, 0))
    full = lambda a: pl.BlockSpec(a.shape, lambda i: (0,) * a.ndim, pipeline_mode=pl.Buffered(1))
    halo = pl.BlockSpec((FFN_HALO, D), lambda i: (jnp.maximum(i * (tm // FFN_HALO) - 1, 0), 0))
    out = pl.pallas_call(
        functools.partial(_ffn_kernel, tiles_per_seq=tiles_per_seq, fc=fc),
        grid=(T // tm,),
        in_specs=[run_spec(D), halo, run_spec(p2d.shape[1]), full(lns), full(wup), full(cw), full(cb),
                  full(wdn), full(wpg), full(wpp)],
        out_specs=run_spec(D),
        out_shape=jax.ShapeDtypeStruct((T // tm, 8, tm // 8, D), F32),
        scratch_shapes=[pltpu.VMEM((FFN_HALO + tm, fc), F32)] * 4 + [pltpu.VMEM((tm, wdn.shape[0]), BF16)],
        compiler_params=pltpu.CompilerParams(
            dimension_semantics=("arbitrary",), vmem_limit_bytes=VMEM_LIMIT),
        name="ffn",
    )(runs(h2d), h2d, runs(p2d), lns, wup, cw, cb, wdn, wpg, wpp)
    return out.reshape(T, D)


def _prep_proj_weights(w_in, mu_wag, w1, a1, g1):
    D = w_in.shape[0]
    sizes = (RW_WIDTH, RW_WIDTH, RW_WIDTH, NSA_WIDTH) + (NSA_KV_WIDTH,) * 6 + (3 * NSA_Q_HEADS, D, D)
    offs = np.concatenate([[0], np.cumsum(sizes)])
    part = lambda i, j: w_in[:, offs[i]:offs[j]]
    mw, ma, mg = mu_wag[0][:, None], mu_wag[1][:, None], mu_wag[2][:, None]
    zg = jnp.zeros((D, RW_GATE_PAD - RW_GATE_LORA), F32)
    rw = jnp.concatenate([
        part(0, 3),
        (1.0 - mw) * w1, (1.0 - ma) * a1,
        mw * w1, ma * a1,
        (1.0 - mg) * g1, zg,
        mg * g1, zg], axis=1)
    def widen(w):
        w = w.reshape(D, NSA_KV_HEADS, HEAD_DIM)
        return jnp.concatenate([w, jnp.zeros_like(w)], axis=2).reshape(D, KEY_COLS)

    w_rows = jnp.concatenate([rw, part(4, 6), widen(part(6, 7)), widen(part(8, 9)), part(11, 13)], axis=1)
    w_cols = jnp.concatenate([part(3, 4), part(7, 8), part(9, 10), part(10, 11),
                              jnp.zeros((D, NSA_GATE_ROWS - 3 * NSA_Q_HEADS), F32)], axis=1)
    return w_rows.astype(BF16), w_cols.T.astype(BF16)


def _prep_rwkv_weights(w2, a2, g2):
    z = jnp.zeros_like(w2)
    w2a2 = jnp.concatenate([jnp.concatenate([w2, z], axis=1),
                            jnp.concatenate([z, a2], axis=1)], axis=0).astype(BF16)
    g2p = jnp.concatenate([g2, jnp.zeros((RW_GATE_PAD - RW_GATE_LORA, RW_WIDTH), F32)],
                          axis=0).astype(BF16)
    return w2a2, g2p


def _rwkv_from_proj(rw3, P):
    w2a2, g2p = _prep_rwkv_weights(P['rw_w2'][0], P['rw_a2'][0], P['rw_g2'][0])
    vecs = jnp.stack([P['rw_w0'][0], P['rw_a0'][0], P['rw_k_k'][0], P['rw_k_a'][0],
                      P['rw_r_k'][0].reshape(-1), P['rw_lnx_g'][0], P['rw_lnx_b'][0],
                      jnp.zeros((RW_WIDTH,), F32)], axis=0)
    return _rwkv_call(rw3, P['rw_mu_rkv'][0], vecs, w2a2, g2p)


def kernel(x, p, ln1_g, w_in, rw_mu_rkv, rw_mu_wag, rw_w0, rw_w1, rw_w2, rw_a0, rw_a1, rw_a2, rw_g1, rw_g2, rw_k_k, rw_k_a, rw_r_k, rw_lnx_g, rw_lnx_b, nsa_cmp_pos, nsa_cmp_w1, nsa_cmp_w2, w_out_a, w_out_b, w_out, ln2_g, w_up, conv_w, conv_b, w_down, ln3_g, w_ple_gate, w_ple_proj, ln_f_g):
    B, S, D = x.shape
    T = B * S
    assert w_in.shape[0] == 1, "single-layer block"
    P = dict(rw_mu_rkv=rw_mu_rkv, rw_w0=rw_w0, rw_w2=rw_w2, rw_a0=rw_a0, rw_a2=rw_a2, rw_g2=rw_g2,
             rw_k_k=rw_k_k, rw_k_a=rw_k_a, rw_r_k=rw_r_k, rw_lnx_g=rw_lnx_g, rw_lnx_b=rw_lnx_b,
             nsa_cmp_pos=nsa_cmp_pos, nsa_cmp_w1=nsa_cmp_w1, nsa_cmp_w2=nsa_cmp_w2)
    h = x.reshape(T, D)
    w_rows, w_cols = _prep_proj_weights(w_in[0], rw_mu_wag[0], rw_w1[0], rw_a1[0], rw_g1[0])
    rw, kvc, ksa, kwa, gates, qt, vst, vwt, gt = _proj_call(h, ln1_g[0][None], w_rows, w_cols, S)
    ya = _rwkv_from_proj(rw.reshape(B, S, RW_COLS), P)
    ybt = _nsa_from_proj(kvc, ksa, kwa, qt, vst, vwt, gt, P)
    h1 = _merge_call(h, ya.reshape(T, RW_WIDTH), ybt, gates,
                     w_out_a[0].astype(BF16), w_out_b[0].astype(BF16), w_out[0].astype(BF16))
    lns = jnp.stack([ln2_g[0], ln3_g[0], ln_f_g], axis=0)
    out = _ffn_call(h1, p[0].reshape(T, -1), lns, w_up[0].astype(BF16), conv_w[0], conv_b[0][None],
                    w_down[0].astype(BF16), w_ple_gate[0].astype(BF16), w_ple_proj[0].astype(BF16), S)
    return out.reshape(B, S, D)
```

```python
import functools

import numpy as np
import jax
import jax.numpy as jnp
from jax import lax
from jax.experimental import pallas as pl
from jax.experimental.pallas import tpu as pltpu

F32 = jnp.float32
BF16 = jnp.bfloat16

HEAD_DIM = 64
NORM_EPS = 1e-6
NEG_INF = -1e30

RW_HEADS = 8
RW_WIDTH = RW_HEADS * HEAD_DIM
RW_DECAY_LORA = 64
RW_AAA_LORA = 64
RW_GATE_LORA = 160
RW_LNX_EPS = 64e-5
RW_CHUNK = 64
RW_GROUP = 4
RW_GROUP_W = RW_GROUP * HEAD_DIM
RW_GATE_PAD = 256

NSA_Q_HEADS = 8
NSA_KV_HEADS = 2
NSA_GROUP = NSA_Q_HEADS // NSA_KV_HEADS
NSA_WIDTH = NSA_Q_HEADS * HEAD_DIM
NSA_KV_WIDTH = NSA_KV_HEADS * HEAD_DIM
CMP_BLOCK = 32
CMP_STRIDE = 16
CMP_HIDDEN = 128
SEL_BLOCK = 64
SEL_TOP = 16
SEL_FORCE_SCORE = 1e4
WINDOW = 512

CONV_WIDTH = 3

RW_COLS = 3 * RW_WIDTH + 2 * 128 + 2 * RW_GATE_PAD
GATE_COLS = 2 * 1024

VMEM_LIMIT = 56 * 1024 * 1024


def _dot(a, b):
    return jnp.dot(a, b, preferred_element_type=F32)


def _dot_nt(a, b):
    return lax.dot_general(a, b, (((1,), (1,)), ((), ())), preferred_element_type=F32)


def _dot_tn(a, b):
    return lax.dot_general(a, b, (((0,), (0,)), ((), ())), preferred_element_type=F32)


def _split2(x):
    hi = x.astype(BF16)
    lo = (x - hi.astype(F32)).astype(BF16)
    return hi, lo


def _split3(x):
    hi = x.astype(BF16)
    r1 = x - hi.astype(F32)
    mid = r1.astype(BF16)
    lo = (r1 - mid.astype(F32)).astype(BF16)
    return hi, mid, lo


def _sigmoid(x):
    return 1.0 / (1.0 + jnp.exp(-x))


def _softplus(x):
    return jnp.maximum(x, 0.0) + jnp.log(1.0 + jnp.exp(-jnp.abs(x)))


PROJ_TM = 512
PROJ_CHUNK = 768
KVC_COLS = 2 * NSA_KV_WIDTH
KEY_COLS = NSA_KV_HEADS * 128
NSA_GATE_ROWS = 32
T_ROWS = NSA_WIDTH + 2 * NSA_KV_WIDTH + NSA_GATE_ROWS


def _proj_kernel(x_ref, g_ref, w_ref, wt_ref, rw_ref, kvc_ref, ksa_ref, kwa_ref, gate_ref,
                 qt_ref, vst_ref, vwt_ref, gt_ref, *, tiles_per_seq):
    tm = x_ref.shape[0]
    x = x_ref[...]
    ms = jnp.mean(x * x, axis=-1, keepdims=True)
    u = (x * lax.rsqrt(ms + NORM_EPS) * g_ref[...]).astype(BF16)

    col = 0
    for o_ref in (rw_ref, kvc_ref):
        width = o_ref.shape[-1]
        for c in range(0, width, PROJ_CHUNK):
            hi = min(c + PROJ_CHUNK, width)
            o_ref[:, c:hi] = _dot(u, w_ref[:, col + c:col + hi]).astype(o_ref.dtype)
        col += width

    s0 = (pl.program_id(0) % tiles_per_seq) * tm
    blk = (s0 + lax.broadcasted_iota(jnp.int32, (tm, KEY_COLS), 0)) // SEL_BLOCK
    lane = lax.broadcasted_iota(jnp.int32, (tm, KEY_COLS), 1) % 128
    onehot = jnp.where(lane - HEAD_DIM == blk, 1.0, 0.0)
    ks = _dot(u, w_ref[:, col:col + KEY_COLS]) + onehot
    kw = _dot(u, w_ref[:, col + KEY_COLS:col + 2 * KEY_COLS])
    for hk in range(NSA_KV_HEADS):
        ksa_ref[0, hk] = ks[:, hk * 128:(hk + 1) * 128].astype(ksa_ref.dtype)
        kwa_ref[0, hk] = kw[:, hk * 128:(hk + 1) * 128].astype(kwa_ref.dtype)
    col += 2 * KEY_COLS

    width = gate_ref.shape[-1]
    for c in range(0, width, PROJ_CHUNK):
        hi = min(c + PROJ_CHUNK, width)
        gate_ref[:, c:hi] = _dot(u, w_ref[:, col + c:col + hi]).astype(gate_ref.dtype)

    t = _dot_nt(wt_ref[...], u)
    row = 0
    for o_ref in (qt_ref, vst_ref, vwt_ref, gt_ref):
        n = o_ref.shape[1]
        o_ref[0] = t[row:row + n].astype(o_ref.dtype)
        row += n


def _proj_call(x2, g, w_all, wt_all, seq):
    T, D = x2.shape
    tm = PROJ_TM
    B = T // seq
    tps = seq // tm
    rows = lambda w: pl.BlockSpec((tm, w), lambda i: (i, 0))
    full = lambda a: pl.BlockSpec(a.shape, lambda i: (0,) * a.ndim)
    keys = pl.BlockSpec((1, NSA_KV_HEADS, tm, 128), lambda i: (i // tps, 0, i % tps, 0))
    tcols = lambda n: pl.BlockSpec((1, n, tm), lambda i: (i // tps, 0, i % tps))
    sds = jax.ShapeDtypeStruct
    return pl.pallas_call(
        functools.partial(_proj_kernel, tiles_per_seq=tps),
        grid=(T // tm,),
        in_specs=[rows(D), full(g), full(w_all), full(wt_all)],
        out_specs=[rows(RW_COLS), rows(KVC_COLS), keys, keys, rows(GATE_COLS),
                   tcols(NSA_WIDTH), tcols(NSA_KV_WIDTH), tcols(NSA_KV_WIDTH), tcols(NSA_GATE_ROWS)],
        out_shape=[sds((T, RW_COLS), BF16), sds((T, KVC_COLS), BF16),
                   sds((B, NSA_KV_HEADS, seq, 128), BF16), sds((B, NSA_KV_HEADS, seq, 128), BF16),
                   sds((T, GATE_COLS), BF16),
                   sds((B, NSA_WIDTH, seq), BF16), sds((B, NSA_KV_WIDTH, seq), BF16),
                   sds((B, NSA_KV_WIDTH, seq), BF16), sds((B, NSA_GATE_ROWS, seq), BF16)],
        compiler_params=pltpu.CompilerParams(
            dimension_semantics=("arbitrary",), vmem_limit_bytes=VMEM_LIMIT),
        name="proj",
    )(x2, g, w_all, wt_all)


def _rwkv_kernel(x_ref, mu_ref, vec_ref, w2a2_ref, g2_ref, o_ref, state_ref, prev_ref):
    C = RW_CHUNK
    GW = RW_GROUP_W
    W = RW_WIDTH
    NB = x_ref.shape[0]
    R = NB * C
    t_idx = pl.program_id(1)
    ops = {}

    @pl.when(t_idx == 0)
    def _():
        state_ref[...] = jnp.zeros_like(state_ref)
        prev_ref[...] = jnp.zeros_like(prev_ref)

    mu = mu_ref[...]
    w0, a0, k_k, k_a, r_k, lnx_g, lnx_b = (vec_ref[i:i + 1, :] for i in range(7))

    gr = lax.broadcasted_iota(jnp.int32, (GW, GW), 0) // HEAD_DIM
    gc = lax.broadcasted_iota(jnp.int32, (GW, GW), 1) // HEAD_DIM
    blk = gr == gc
    ones_bd = jnp.where(blk, 1.0, 0.0).astype(BF16)

    def headsums(zs):
        parts = []
        for z in zs:
            zb = z.astype(BF16)
            parts += [zb[:, :GW], zb[:, GW:]]
        s = _dot(jnp.concatenate(parts, axis=0), ones_bd)
        return [jnp.concatenate([s[2 * R * i:2 * R * i + R], s[2 * R * i + R:2 * R * (i + 1)]], axis=1)
                for i in range(len(zs))]

    t_n = lax.broadcasted_iota(jnp.int32, (C, GW), 0)
    s_n = lax.broadcasted_iota(jnp.int32, (C, GW), 1) % HEAD_DIM
    strict = t_n > s_n
    incl = t_n >= s_n
    eye_n = jnp.where(t_n == s_n, 1.0, 0.0)

    def bd(z):
        z4 = jnp.concatenate([z] * RW_GROUP, axis=0).astype(BF16)
        return jnp.where(blk, z4, jnp.zeros_like(z4))

    def prepare():
        x = x_ref[...].reshape(R, RW_COLS).astype(F32)
        rolled = pltpu.roll(x, 1, axis=0)
        row8 = lax.broadcasted_iota(jnp.int32, (8, 1), 0)
        pieces = []
        for bi in range(NB):
            pieces.append(jnp.where(row8 == 0, prev_ref[bi, 0:1, :], rolled[bi * C:bi * C + 8]))
            pieces.append(rolled[bi * C + 8:(bi + 1) * C])
            prev_ref[bi, 0:1, :] = x[(bi + 1) * C - 1:(bi + 1) * C, :]
        xs = jnp.concatenate(pieces, axis=0)

        def lerp(j):
            cur = x[:, j * W:(j + 1) * W]
            return cur + (xs[:, j * W:(j + 1) * W] - cur) * mu[j:j + 1, :]

        r, k, v = lerp(0), lerp(1), lerp(2)
        o = 3 * W
        pre_a = x[:, o:o + 128] + xs[:, o + 128:o + 256]
        lane = lax.broadcasted_iota(jnp.int32, (R, 128), 1)
        h_a = jnp.where(lane < RW_DECAY_LORA, jnp.tanh(pre_a), pre_a)
        lwa = _dot(h_a.astype(BF16), w2a2_ref[...])
        o += 256
        pre_g = x[:, o:o + RW_GATE_PAD] + xs[:, o + RW_GATE_PAD:o + 2 * RW_GATE_PAD]
        g = _dot(_sigmoid(pre_g).astype(BF16), g2_ref[...])
        w = -_softplus(-(w0 + lwa[:, :W])) - 0.5
        ld = -jnp.exp(w)
        a = _sigmoid(a0 + lwa[:, W:])
        kkr = k * k_k
        k2 = k * (1.0 + (a - 1.0) * k_a)
        kk_ss, bonus = headsums([kkr * kkr, r * k2 * r_k])
        kk = kkr / jnp.maximum(jnp.sqrt(kk_ss), 1e-12)
        b = kk * a
        tr = lax.broadcasted_iota(jnp.int32, (R, R), 0)
        tc = lax.broadcasted_iota(jnp.int32, (R, R), 1)
        tri = jnp.where((tr >= tc) & (tr // C == tc // C), 1.0, 0.0).astype(BF16)
        l_inc = _dot(tri, jnp.concatenate(_split2(ld), axis=1))
        l_inc = l_inc[:, :W] + l_inc[:, W:]
        l_end = [l_inc[(bi + 1) * C - 1:(bi + 1) * C] for bi in range(NB)]
        e_neg = jnp.exp(-l_inc)
        e_tail = jnp.concatenate([jnp.exp(l_end[bi] - l_inc[bi * C:(bi + 1) * C]) for bi in range(NB)],
                                 axis=0)
        new_opb = (-kk * jnp.exp(l_inc - ld), r * jnp.exp(l_inc), b * e_neg, k2 * e_neg,
                   b * e_tail, k2 * e_tail)
        ops['matmul'] = tuple(z.astype(BF16) for z in new_opb)
        ops['output'] = (v, bonus, g)
        ops['decay'] = jnp.concatenate([jnp.broadcast_to(jnp.exp(le), (8, W)) for le in l_end], axis=0)

    def consume():
        n_grp = W // GW
        chains = [(bi, gi) for bi in range(NB) for gi in range(n_grp)]
        cut = lambda z, c: z[c[0] * C:(c[0] + 1) * C, c[1] * GW:(c[1] + 1) * GW]
        each = lambda f, *lists: [f(*args) for args in zip(*lists)]

        p_a_hat, p_r_hat, p_b_hat, p_k_hat, p_b_tail, p_k_tail = ops['matmul']
        p_v, p_bonus, p_g = ops['output']
        p_elc = ops['decay']
        a_h = [cut(p_a_hat, c) for c in chains]
        r_h = [cut(p_r_hat, c) for c in chains]
        ar = each(lambda x1, x2: jnp.concatenate([x1, x2], axis=0), a_h, r_h)
        m1 = each(_dot_nt, ar, [bd(cut(p_b_hat, c)) for c in chains])
        m2 = each(_dot_nt, ar, [bd(cut(p_k_hat, c)) for c in chains])
        m_ab = [jnp.where(strict, m[:C], 0.0) for m in m1]
        m_rb = [jnp.where(incl, m[C:], 0.0) for m in m1]
        m_ak = [jnp.where(strict, m[:C], 0.0) for m in m2]
        m_rk = [jnp.where(incl, m[C:], 0.0) for m in m2]

        tinv = [eye_n + m for m in m_ab]
        p = each(lambda m: _dot(m.astype(BF16), bd(m)), m_ab)
        power = 2
        while 2 * power < C:
            tp = each(lambda t, q: _dot(jnp.concatenate([t, q], axis=0).astype(BF16), bd(q)), tinv, p)
            tinv = each(lambda t, x1: t + x1[:C], tinv, tp)
            p = [x1[C:] for x1 in tp]
            power *= 2
        tinv = each(lambda t, q: t + _dot(t.astype(BF16), bd(q)), tinv, p)

        s_old = [state_ref[i * GW:(i + 1) * GW, :] for i in range(len(chains))]
        s_bf = [s.astype(BF16) for s in s_old]
        vg = [cut(p_v, c) for c in chains]
        bd_v = [bd(x1) for x1 in vg]
        xz = each(lambda x1, s, m, bv: _dot_nt(x1, s) + _dot(m.astype(BF16), bv), a_h, s_bf, m_ak, bd_v)
        u = each(lambda t, x1: _dot(t.astype(BF16), bd(x1)), tinv, xz)
        y = each(lambda x1, s, mb, mk, uu, bv:
                 _dot_nt(x1, s) + _dot(jnp.concatenate([mb, mk], axis=1).astype(BF16),
                                       jnp.concatenate([bd(uu), bv], axis=0)),
                 r_h, s_bf, m_rb, m_rk, u, bd_v)
        new_states = []
        for c, uu, vv, s in zip(chains, u, vg, s_old):
            upd = _dot_tn(jnp.concatenate([uu, vv], axis=0).astype(BF16),
                          jnp.concatenate([cut(p_b_tail, c), cut(p_k_tail, c)], axis=0))
            decay = p_elc[c[0] * 8:c[0] * 8 + 1, c[1] * GW:(c[1] + 1) * GW]
            new_states.append(s * decay + jnp.where(blk, upd, 0.0))
        state_ref[...] = jnp.concatenate(new_states, axis=0)
        y_rows = [jnp.concatenate(y[bi * n_grp:(bi + 1) * n_grp], axis=1) for bi in range(NB)]
        y = jnp.concatenate(y_rows, axis=0)
        mean = headsums([y])[0] * (1.0 / HEAD_DIM)
        yc = y - mean
        var = headsums([yc * yc])[0] * (1.0 / HEAD_DIM)
        yn = yc * lax.rsqrt(var + RW_LNX_EPS) * lnx_g + lnx_b
        yn = yn + p_bonus * p_v
        o_ref[...] = (yn * p_g).reshape(NB, C, W).astype(o_ref.dtype)

    prepare()
    consume()


RW_SEQS_PER_STEP = 8


def _rwkv_call(rw3, mu, vecs, w2a2, g2p):
    B, S, _ = rw3.shape
    C = RW_CHUNK
    nb = RW_SEQS_PER_STEP if B % RW_SEQS_PER_STEP == 0 else 1
    n_groups = RW_WIDTH // RW_GROUP_W
    return pl.pallas_call(
        _rwkv_kernel,
        grid=(B // nb, S // C),
        in_specs=[
            pl.BlockSpec((nb, C, RW_COLS), lambda b, t: (b, t, 0)),
            pl.BlockSpec(mu.shape, lambda b, t: (0, 0)),
            pl.BlockSpec(vecs.shape, lambda b, t: (0, 0)),
            pl.BlockSpec(w2a2.shape, lambda b, t: (0, 0)),
            pl.BlockSpec(g2p.shape, lambda b, t: (0, 0)),
        ],
        out_specs=pl.BlockSpec((nb, C, RW_WIDTH), lambda b, t: (b, t, 0)),
        out_shape=jax.ShapeDtypeStruct((B, S, RW_WIDTH), BF16),
        scratch_shapes=[
            pltpu.VMEM((nb * n_groups * RW_GROUP_W, RW_GROUP_W), F32),
            pltpu.VMEM((nb, 8, RW_COLS), F32),
        ],
        compiler_params=pltpu.CompilerParams(
            dimension_semantics=("arbitrary", "arbitrary"), vmem_limit_bytes=VMEM_LIMIT),
        name="rwkv",
    )(rw3, mu, vecs, w2a2, g2p)


def _compress_kernel(kv_ref, wab_ref, pos_ref, w1_ref, w2_ref, w2t_ref, kcb_ref, vcbt_ref):
    n_half = kv_ref.shape[1]
    for j in range(2):
        pab = _dot(kv_ref[0], wab_ref[j])
        half = NSA_KV_HEADS * CMP_HIDDEN
        pa, pb = pab[:, :half], pab[:, half:]
        pb = pltpu.roll(pb, n_half - 1, axis=0)
        pos_term = _dot(pos_ref[j], w1_ref[j])[0:1]
        hid = pa + pb + jnp.concatenate([pos_term] * NSA_KV_HEADS, axis=1)
        act = (hid * _sigmoid(hid)).astype(BF16)
        for hk in range(NSA_KV_HEADS):
            a_h = act[:, hk * CMP_HIDDEN:(hk + 1) * CMP_HIDDEN]
            if j == 0:
                kcb_ref[0, hk] = _dot(a_h, w2_ref[j]).astype(kcb_ref.dtype)
            else:
                vcbt_ref[0, hk] = _dot_nt(w2t_ref[j], a_h).astype(vcbt_ref.dtype)


def _compress_call(kv, wab, pos8, w1, w2, w2t):
    B, n_half, width = kv.shape
    full = lambda a: pl.BlockSpec(a.shape, lambda b: (0,) * a.ndim)
    return pl.pallas_call(
        _compress_kernel,
        grid=(B,),
        in_specs=[pl.BlockSpec((1, n_half, width), lambda b: (b, 0, 0)),
                  full(wab), full(pos8), full(w1), full(w2), full(w2t)],
        out_specs=[pl.BlockSpec((1, NSA_KV_HEADS, n_half, HEAD_DIM), lambda b: (b, 0, 0, 0)),
                   pl.BlockSpec((1, NSA_KV_HEADS, HEAD_DIM, n_half), lambda b: (b, 0, 0, 0))],
        out_shape=[jax.ShapeDtypeStruct((B, NSA_KV_HEADS, n_half, HEAD_DIM), BF16),
                   jax.ShapeDtypeStruct((B, NSA_KV_HEADS, HEAD_DIM, n_half), BF16)],
        compiler_params=pltpu.CompilerParams(
            dimension_semantics=("arbitrary",), vmem_limit_bytes=VMEM_LIMIT),
        name="nsa_compress",
    )(kv, wab, pos8, w1, w2, w2t)


NSA2_TQ = 256
NSA_AUG = 128
NSA_BIAS_ROWS = 32
LOG2_E = 1.4426950408889634


def _nsa2_kernel(qt_ref, gt_ref, ks_ref, vst_ref, kw_ref, vwt_ref, kcb_ref, vcbt_ref, ovt_ref,
                 tri_ref, wbias_ref, o_ref, *, seq):
    tq = NSA2_TQ
    G = NSA_GROUP
    R = G * tq
    n_half = kcb_ref.shape[2]
    n_cmp = n_half - 1
    n_sel = seq // SEL_BLOCK
    n_top = min(SEL_TOP, n_sel)
    n_wchunks = WINDOW // tq + 1
    step = pl.program_id(1)
    q0 = step * tq

    t_lane = q0 + lax.broadcasted_iota(jnp.int32, (1, R), 1) % tq
    gates = _sigmoid(gt_ref[0].astype(F32))
    pad_rows = jnp.zeros((NSA_AUG - HEAD_DIM - NSA_BIAS_ROWS, tq), BF16)

    hrow = lambda hk, g: slice((hk * G + g) * HEAD_DIM, (hk * G + g + 1) * HEAD_DIM)
    o_cmp, qaug = {}, {}
    for hk in range(NSA_KV_HEADS):
        q64 = jnp.concatenate([qt_ref[0, hrow(hk, g), :] for g in range(G)], axis=1)
        q64 = q64 * jnp.asarray(HEAD_DIM ** -0.5, BF16)

        cidx = lax.broadcasted_iota(jnp.int32, (n_half, R), 0)
        cvalid = (cidx * CMP_STRIDE + (CMP_BLOCK - 1) <= t_lane) & (cidx < n_cmp)
        s = jnp.where(cvalid, _dot(kcb_ref[0, hk], q64), NEG_INF)
        m = jnp.max(s, axis=0, keepdims=True)
        e = jnp.where(cvalid, jnp.exp(s - m), 0.0)
        l = jnp.sum(e, axis=0, keepdims=True)
        p_c = e / jnp.where(l > 0.0, l, 1.0)
        o_c = _dot(vcbt_ref[0, hk], p_c.astype(BF16))

        psum = p_c[:, 0:tq]
        for g in range(1, G):
            psum = psum + p_c[:, g * tq:(g + 1) * tq]
        hi, lo = _split2(psum)
        imp2 = _dot(ovt_ref[...], jnp.concatenate([hi, lo], axis=1))
        imp = imp2[:, :tq] + imp2[:, tq:]
        jblk = lax.broadcasted_iota(jnp.int32, (n_sel, tq), 0)
        cur = (q0 + lax.broadcasted_iota(jnp.int32, (n_sel, tq), 1)) // SEL_BLOCK
        forced = (jblk == 0) | (jblk == cur) | (jblk == cur - 1)
        score = jnp.where(forced, SEL_FORCE_SCORE, jnp.where(jblk <= cur, imp, -1.0))
        rank = jnp.zeros((n_sel, tq), F32)
        for j in range(n_sel):
            sj = score[j:j + 1, :]
            ahead = (sj > score) | ((sj == score) & (j < jblk))
            rank = rank + jnp.where(ahead, 1.0, 0.0)
        sel_bias = jnp.where(rank < n_top, 0.0, NEG_INF).astype(BF16)
        q_l2 = (q64.astype(F32) * LOG2_E).astype(BF16)
        for g in range(G):
            o_cmp[hk, g] = o_c[:, g * tq:(g + 1) * tq]
            qaug[hk, g] = jnp.concatenate([q_l2[:, g * tq:(g + 1) * tq], sel_bias, pad_rows], axis=0)

    chains = [(hk, g) for hk in range(NSA_KV_HEADS) for g in range(G)]

    def softmax_pv(s_list, vt_of, carry=None):
        m_blk = [jnp.max(s, axis=0, keepdims=True) for s in s_list]
        if carry is None:
            m_new = m_blk
        else:
            m_new = [jnp.maximum(c[0], mb) for c, mb in zip(carry, m_blk)]
        p = [jnp.exp2(s - mn) for s, mn in zip(s_list, m_new)]
        l_blk = [jnp.sum(x, axis=0, keepdims=True) for x in p]
        pv = [_dot(vt_of(c), x.astype(BF16)) for c, x in zip(chains, p)]
        if carry is None:
            return [(mn, lb, a) for mn, lb, a in zip(m_new, l_blk, pv)]
        alpha = [jnp.exp2(c[0] - mn) for c, mn in zip(carry, m_new)]
        return [(mn, c[1] * al + lb, c[2] * al + a)
                for c, mn, al, lb, a in zip(carry, m_new, alpha, l_blk, pv)]

    def vcols(ref, hk, j):
        return ref[0, hk * HEAD_DIM:(hk + 1) * HEAD_DIM, pl.ds(pl.multiple_of(j * tq, tq), tq)]

    wchunk = [jnp.maximum(step - (n_wchunks - 1) + w, 0) for w in range(n_wchunks)]
    kw_rows = [jnp.concatenate([kw_ref[0, hk, j] for j in wchunk], axis=0) for hk in range(NSA_KV_HEADS)]
    vw_cols = [jnp.concatenate([vcols(vwt_ref, hk, j) for j in wchunk], axis=1)
               for hk in range(NSA_KV_HEADS)]
    wbias = wbias_ref[jnp.minimum(step, n_wchunks - 1)]
    s_win = [_dot(kw_rows[hk], qaug[hk, g]) + wbias for hk, g in chains]
    win = softmax_pv(s_win, lambda c: vw_cols[c[0]])

    tri = tri_ref[...]
    s_diag = [_dot(ks_ref[0, hk, step], qaug[hk, g]) + tri for hk, g in chains]
    carry = softmax_pv(s_diag, lambda c: vcols(vst_ref, c[0], step))

    def body(j, flat):
        carry = [tuple(flat[3 * i:3 * i + 3]) for i in range(len(chains))]
        s_j = [_dot(ks_ref[0, hk, j], qaug[hk, g]) for hk, g in chains]
        new = softmax_pv(s_j, lambda c: vcols(vst_ref, c[0], j), carry)
        return tuple(x for c in new for x in c)

    flat = lax.fori_loop(0, step, body, tuple(x for c in carry for x in c))
    sel = [tuple(flat[3 * i:3 * i + 3]) for i in range(len(chains))]

    for i, (hk, g) in enumerate(chains):
        gate = lambda j: gates[(hk * G + g) * 3 + j:(hk * G + g) * 3 + j + 1, :]
        out = (gate(0) * o_cmp[hk, g] + gate(1) * (sel[i][2] / sel[i][1])
               + gate(2) * (win[i][2] / win[i][1]))
        o_ref[0, hrow(hk, g), :] = out.astype(o_ref.dtype)


def _nsa2_call(qt, gt, ks5, vst, kw5, vwt, kcb, vcbt, ovt, tri, wbias, seq):
    B = qt.shape[0]
    tq = NSA2_TQ
    per_b = lambda a: pl.BlockSpec((1,) + a.shape[1:], lambda b, i: (b,) + (0,) * (a.ndim - 1))
    full = lambda a: pl.BlockSpec(a.shape, lambda b, i: (0,) * a.ndim)
    return pl.pallas_call(
        functools.partial(_nsa2_kernel, seq=seq),
        grid=(B, seq // tq),
        in_specs=[pl.BlockSpec((1, NSA_WIDTH, tq), lambda b, i: (b, 0, i)),
                  pl.BlockSpec((1, gt.shape[1], tq), lambda b, i: (b, 0, i)),
                  per_b(ks5), per_b(vst), per_b(kw5), per_b(vwt), per_b(kcb), per_b(vcbt),
                  full(ovt), full(tri), full(wbias)],
        out_specs=pl.BlockSpec((1, NSA_WIDTH, tq), lambda b, i: (b, 0, i)),
        out_shape=jax.ShapeDtypeStruct((B, NSA_WIDTH, seq), BF16),
        compiler_params=pltpu.CompilerParams(
            dimension_semantics=("arbitrary", "arbitrary"), vmem_limit_bytes=VMEM_LIMIT),
        name="nsa_attention",
    )(qt, gt, ks5, vst, kw5, vwt, kcb, vcbt, ovt, tri, wbias)


def _nsa_from_proj(kvc, ksa, kwa, qt, vst, vwt, gt, P):
    B, Hk, S, _ = ksa.shape
    tq = NSA2_TQ
    Dh = HEAD_DIM
    n_sel = S // SEL_BLOCK
    assert S % tq == 0 and WINDOW % tq == 0 and n_sel == NSA_BIAS_ROWS
    n_half = S // CMP_STRIDE
    kv = kvc.reshape(B, n_half, CMP_STRIDE * KVC_COLS)
    w1 = P['nsa_cmp_w1'][0]
    w1h = w1.reshape(2, 2, CMP_STRIDE, HEAD_DIM, CMP_HIDDEN)
    pick = np.zeros((2, 2 * Hk, Hk), np.float32)
    for j in range(2):
        for hk in range(Hk):
            pick[j, j * Hk + hk, hk] = 1.0
    wab = jnp.einsum('jaldn,jch->jlcdahn', w1h, jnp.asarray(pick)).reshape(
        2, CMP_STRIDE * KVC_COLS, 2 * Hk * CMP_HIDDEN).astype(BF16)
    pos8 = jnp.broadcast_to(P['nsa_cmp_pos'][0].reshape(2, 1, CMP_BLOCK * HEAD_DIM),
                            (2, 8, CMP_BLOCK * HEAD_DIM)).astype(BF16)
    w2 = P['nsa_cmp_w2'][0].astype(BF16)
    kcb, vcbt = _compress_call(kv, wab, pos8, w1.astype(BF16), w2, jnp.swapaxes(w2, 1, 2))

    ks5 = ksa.reshape(B, Hk, S // tq, tq, NSA_AUG)
    kw5 = kwa.reshape(B, Hk, S // tq, tq, NSA_AUG)

    n_cmp = (S - CMP_BLOCK) // CMP_STRIDE + 1
    cmp_start = np.arange(n_half) * CMP_STRIDE
    sel_start = np.arange(n_sel) * SEL_BLOCK
    overlap = ((cmp_start[:, None] <= sel_start[None, :] + SEL_BLOCK - 1)
               & (cmp_start[:, None] + CMP_BLOCK - 1 >= sel_start[None, :])
               & (np.arange(n_half)[:, None] < n_cmp)).astype(np.float32)
    tri = np.where(np.arange(tq)[:, None] <= np.arange(tq)[None, :], 0.0, NEG_INF).astype(np.float32)
    n_w = WINDOW // tq
    masked = np.full((tq, tq), NEG_INF, np.float32)
    clear = np.zeros((tq, tq), np.float32)
    wbias = np.stack([np.concatenate([masked] * (n_w - v) + [clear] * v + [tri], axis=0) for v in range(n_w)]
                     + [np.concatenate([NEG_INF - tri] + [clear] * (n_w - 1) + [tri], axis=0)])
    return _nsa2_call(qt, gt, ks5, vst, kw5, vwt, kcb, vcbt, jnp.asarray(overlap.T, BF16),
                      jnp.asarray(tri), jnp.asarray(wbias), S)


def _merge_kernel(x_ref, ya_ref, ybt_ref, gate_ref, wa_ref, wb_ref, wo_ref, o_ref):
    D = x_ref.shape[-1]
    ta = _dot(ya_ref[...], wa_ref[...])
    tb = _dot_tn(ybt_ref[0], wb_ref[...])
    ga = _sigmoid(gate_ref[:, :D].astype(F32))
    gb = _sigmoid(gate_ref[:, D:].astype(F32))
    mix = (ga * ta + gb * tb).astype(BF16)
    o_ref[...] = x_ref[...] + _dot(mix, wo_ref[...])


def _merge_call(x2, ya2, ybt, gates, wa, wb, wo, tm=512):
    T, D = x2.shape
    tiles_per_seq = ybt.shape[2] // tm
    row = lambda w: pl.BlockSpec((tm, w), lambda i: (i, 0))
    full = lambda a: pl.BlockSpec(a.shape, lambda i: (0,) * a.ndim)
    ybt_spec = pl.BlockSpec((1, ybt.shape[1], tm), lambda i: (i // tiles_per_seq, 0, i % tiles_per_seq))
    return pl.pallas_call(
        _merge_kernel,
        grid=(T // tm,),
        in_specs=[row(D), row(ya2.shape[1]), ybt_spec, row(gates.shape[1]),
                  full(wa), full(wb), full(wo)],
        out_specs=row(D),
        out_shape=jax.ShapeDtypeStruct((T, D), F32),
        compiler_params=pltpu.CompilerParams(
            dimension_semantics=("arbitrary",), vmem_limit_bytes=VMEM_LIMIT),
        name="merge",
    )(x2, ya2, ybt, gates, wa, wb, wo)


FFN_HALO = 8


def _rms(x, g):
    return x * lax.rsqrt(jnp.mean(x * x, axis=-1, keepdims=True) + NORM_EPS) * g


def _ffn_kernel(h_ref, halo_ref, p_ref, ln_ref, wup_ref, cw_ref, cb_ref, wdn_ref, wpg_ref, wpp_ref,
                o_ref, up0a_ref, up0b_ref, up1a_ref, up1b_ref, act_ref, *, tiles_per_seq, fc):
    up_refs = ((up0a_ref, up0b_ref), (up1a_ref, up1b_ref))
    tm = act_ref.shape[0]
    D = halo_ref.shape[1]
    V = tm // 8
    d_ff = wdn_ref.shape[0]
    slabs = lambda ref, w: jnp.swapaxes(ref[0], 0, 1).reshape(tm, w)
    h = slabs(h_ref, D)
    first = (pl.program_id(0) % tiles_per_seq) == 0
    halo = jnp.where(first, 0.0, halo_ref[...])
    ln2, ln3, lnf = ln_ref[0:1, :], ln_ref[1:2, :], ln_ref[2:3, :]
    u = jnp.concatenate([_rms(halo, ln2), _rms(h, ln2)], axis=0).astype(BF16)

    n_chunks = d_ff // fc
    sub = lax.broadcasted_iota(jnp.int32, (8, 1), 0)

    def project(c):
        for half in range(2):
            col = half * d_ff + c * fc
            up_refs[c % 2][half][...] = _dot(u, wup_ref[:, col:col + fc])

    def conv(c, half):
        ref = up_refs[c % 2][half]
        col = half * d_ff + c * fc
        halo_up = ref[0:FFN_HALO, :]
        last = lambda k: ref[FFN_HALO + tm - 8 * k:FFN_HALO + tm - 8 * (k - 1), :]
        wrap1 = pltpu.roll(jnp.where(sub == 7, halo_up, last(1)), 1, axis=0)
        wrap2 = pltpu.roll(jnp.where(sub == 7, pltpu.roll(halo_up, 1, axis=0), last(2)), 1, axis=0)
        x0 = ref[FFN_HALO:FFN_HALO + tm, :]
        x1 = jnp.concatenate([wrap1, ref[FFN_HALO:FFN_HALO + tm - 8, :]], axis=0)
        x2 = jnp.concatenate([wrap2, wrap1, ref[FFN_HALO:FFN_HALO + tm - 16, :]], axis=0)
        tap = lambda j: cw_ref[j:j + 1, col:col + fc]
        return cb_ref[:, col:col + fc] + tap(0) * x2 + tap(1) * x1 + tap(2) * x0

    project(0)
    for c in range(n_chunks):
        if c + 1 < n_chunks:
            project(c + 1)
        a = conv(c, 0)
        b = conv(c, 1)
        act_ref[:, c * fc:(c + 1) * fc] = (a * _sigmoid(a) * b).astype(BF16)
    h2 = h + _dot(act_ref[...], wdn_ref[...])
    gate = _sigmoid(_dot(_rms(h2, ln3).astype(BF16), wpg_ref[...]))
    h3 = h2 + gate * _dot(slabs(p_ref, p_ref.shape[3]).astype(BF16), wpp_ref[...])
    o_ref[0] = jnp.swapaxes(_rms(h3, lnf).reshape(V, 8, D), 0, 1)


FFN_TM = 512
FFN_FC = 256


def _ffn_call(h2d, p2d, lns, wup, cw, cb, wdn, wpg, wpp, seq):
    T, D = h2d.shape
    tm, fc = FFN_TM, FFN_FC
    assert CONV_WIDTH == 3 and seq % tm == 0 and wdn.shape[0] % fc == 0
    tiles_per_seq = seq // tm
    runs = lambda x: x.reshape(T // tm, 8, tm // 8, x.shape[1])
    run_spec = lambda w: pl.BlockSpec((1, 8, tm // 8, w), lambda i: (i, 0, 0, 0))
    full = lambda a: pl.BlockSpec(a.shape, lambda i: (0,) * a.ndim, pipeline_mode=pl.Buffered(1))
    halo = pl.BlockSpec((FFN_HALO, D), lambda i: (jnp.maximum(i * (tm // FFN_HALO) - 1, 0), 0))
    out = pl.pallas_call(
        functools.partial(_ffn_kernel, tiles_per_seq=tiles_per_seq, fc=fc),
        grid=(T // tm,),
        in_specs=[run_spec(D), halo, run_spec(p2d.shape[1]), full(lns), full(wup), full(cw), full(cb),
                  full(wdn), full(wpg), full(wpp)],
        out_specs=run_spec(D),
        out_shape=jax.ShapeDtypeStruct((T // tm, 8, tm // 8, D), F32),
        scratch_shapes=[pltpu.VMEM((FFN_HALO + tm, fc), F32)] * 4 + [pltpu.VMEM((tm, wdn.shape[0]), BF16)],
        compiler_params=pltpu.CompilerParams(
            dimension_semantics=("arbitrary",), vmem_limit_bytes=VMEM_LIMIT),
        name="ffn",
    )(runs(h2d), h2d, runs(p2d), lns, wup, cw, cb, wdn, wpg, wpp)
    return out.reshape(T, D)


def _prep_proj_weights(w_in, mu_wag, w1, a1, g1):
    D = w_in.shape[0]
    sizes = (RW_WIDTH, RW_WIDTH, RW_WIDTH, NSA_WIDTH) + (NSA_KV_WIDTH,) * 6 + (3 * NSA_Q_HEADS, D, D)
    offs = np.concatenate([[0], np.cumsum(sizes)])
    part = lambda i, j: w_in[:, offs[i]:offs[j]]
    mw, ma, mg = mu_wag[0][:, None], mu_wag[1][:, None], mu_wag[2][:, None]
    zg = jnp.zeros((D, RW_GATE_PAD - RW_GATE_LORA), F32)
    rw = jnp.concatenate([
        part(0, 3),
        (1.0 - mw) * w1, (1.0 - ma) * a1,
        mw * w1, ma * a1,
        (1.0 - mg) * g1, zg,
        mg * g1, zg], axis=1)
    def widen(w):
        w = w.reshape(D, NSA_KV_HEADS, HEAD_DIM)
        return jnp.concatenate([w, jnp.zeros_like(w)], axis=2).reshape(D, KEY_COLS)

    w_rows = jnp.concatenate([rw, part(4, 6), widen(part(6, 7)), widen(part(8, 9)), part(11, 13)], axis=1)
    w_cols = jnp.concatenate([part(3, 4), part(7, 8), part(9, 10), part(10, 11),
                              jnp.zeros((D, NSA_GATE_ROWS - 3 * NSA_Q_HEADS), F32)], axis=1)
    return w_rows.astype(BF16), w_cols.T.astype(BF16)


def _prep_rwkv_weights(w2, a2, g2):
    z = jnp.zeros_like(w2)
    w2a2 = jnp.concatenate([jnp.concatenate([w2, z], axis=1),
                            jnp.concatenate([z, a2], axis=1)], axis=0).astype(BF16)
    g2p = jnp.concatenate([g2, jnp.zeros((RW_GATE_PAD - RW_GATE_LORA, RW_WIDTH), F32)],
                          axis=0).astype(BF16)
    return w2a2, g2p


def _rwkv_from_proj(rw3, P):
    w2a2, g2p = _prep_rwkv_weights(P['rw_w2'][0], P['rw_a2'][0], P['rw_g2'][0])
    vecs = jnp.stack([P['rw_w0'][0], P['rw_a0'][0], P['rw_k_k'][0], P['rw_k_a'][0],
                      P['rw_r_k'][0].reshape(-1), P['rw_lnx_g'][0], P['rw_lnx_b'][0],
                      jnp.zeros((RW_WIDTH,), F32)], axis=0)
    return _rwkv_call(rw3, P['rw_mu_rkv'][0], vecs, w2a2, g2p)


def kernel(x, p, ln1_g, w_in, rw_mu_rkv, rw_mu_wag, rw_w0, rw_w1, rw_w2, rw_a0, rw_a1, rw_a2, rw_g1, rw_g2, rw_k_k, rw_k_a, rw_r_k, rw_lnx_g, rw_lnx_b, nsa_cmp_pos, nsa_cmp_w1, nsa_cmp_w2, w_out_a, w_out_b, w_out, ln2_g, w_up, conv_w, conv_b, w_down, ln3_g, w_ple_gate, w_ple_proj, ln_f_g):
    B, S, D = x.shape
    T = B * S
    assert w_in.shape[0] == 1, "single-layer block"
    P = dict(rw_mu_rkv=rw_mu_rkv, rw_w0=rw_w0, rw_w2=rw_w2, rw_a0=rw_a0, rw_a2=rw_a2, rw_g2=rw_g2,
             rw_k_k=rw_k_k, rw_k_a=rw_k_a, rw_r_k=rw_r_k, rw_lnx_g=rw_lnx_g, rw_lnx_b=rw_lnx_b,
             nsa_cmp_pos=nsa_cmp_pos, nsa_cmp_w1=nsa_cmp_w1, nsa_cmp_w2=nsa_cmp_w2)
    h = x.reshape(T, D)
    w_rows, w_cols = _prep_proj_weights(w_in[0], rw_mu_wag[0], rw_w1[0], rw_a1[0], rw_g1[0])
    rw, kvc, ksa, kwa, gates, qt, vst, vwt, gt = _proj_call(h, ln1_g[0][None], w_rows, w_cols, S)
    ya = _rwkv_from_proj(rw.reshape(B, S, RW_COLS), P)
    ybt = _nsa_from_proj(kvc, ksa, kwa, qt, vst, vwt, gt, P)
    h1 = _merge_call(h, ya.reshape(T, RW_WIDTH), ybt, gates,
                     w_out_a[0].astype(BF16), w_out_b[0].astype(BF16), w_out[0].astype(BF16))
    lns = jnp.stack([ln2_g[0], ln3_g[0], ln_f_g], axis=0)
    out = _ffn_call(h1, p[0].reshape(T, -1), lns, w_up[0].astype(BF16), conv_w[0], conv_b[0][None],
                    w_down[0].astype(BF16), w_ple_gate[0].astype(BF16), w_ple_proj[0].astype(BF16), S)
    return out.reshape(B, S, D)
```

```python
import functools

import numpy as np
import jax
import jax.numpy as jnp
from jax import lax
from jax.experimental import pallas as pl
from jax.experimental.pallas import tpu as pltpu

F32 = jnp.float32
BF16 = jnp.bfloat16

HEAD_DIM = 64
NORM_EPS = 1e-6
NEG_INF = -1e30

RW_HEADS = 8
RW_WIDTH = RW_HEADS * HEAD_DIM
RW_DECAY_LORA = 64
RW_AAA_LORA = 64
RW_GATE_LORA = 160
RW_LNX_EPS = 64e-5
RW_CHUNK = 64
RW_GROUP = 4
RW_GROUP_W = RW_GROUP * HEAD_DIM
RW_GATE_PAD = 256

NSA_Q_HEADS = 8
NSA_KV_HEADS = 2
NSA_GROUP = NSA_Q_HEADS // NSA_KV_HEADS
NSA_WIDTH = NSA_Q_HEADS * HEAD_DIM
NSA_KV_WIDTH = NSA_KV_HEADS * HEAD_DIM
CMP_BLOCK = 32
CMP_STRIDE = 16
CMP_HIDDEN = 128
SEL_BLOCK = 64
SEL_TOP = 16
SEL_FORCE_SCORE = 1e4
WINDOW = 512

CONV_WIDTH = 3

RW_COLS = 3 * RW_WIDTH + 2 * 128 + 2 * RW_GATE_PAD
GATE_COLS = 2 * 1024

VMEM_LIMIT = 56 * 1024 * 1024


def _dot(a, b):
    return jnp.dot(a, b, preferred_element_type=F32)


def _dot_nt(a, b):
    return lax.dot_general(a, b, (((1,), (1,)), ((), ())), preferred_element_type=F32)


def _dot_tn(a, b):
    return lax.dot_general(a, b, (((0,), (0,)), ((), ())), preferred_element_type=F32)


def _split2(x):
    hi = x.astype(BF16)
    lo = (x - hi.astype(F32)).astype(BF16)
    return hi, lo


def _split3(x):
    hi = x.astype(BF16)
    r1 = x - hi.astype(F32)
    mid = r1.astype(BF16)
    lo = (r1 - mid.astype(F32)).astype(BF16)
    return hi, mid, lo


def _sigmoid(x):
    return 1.0 / (1.0 + jnp.exp(-x))


LOG2_E = 1.4426950408889634
DECAY_SCALE_LOG2 = 0.6065306597126334 * LOG2_E


PROJ_TM = 512
PROJ_CHUNK = 768
KVC_COLS = 2 * NSA_KV_WIDTH
KEY_COLS = NSA_KV_HEADS * 128
NSA_GATE_ROWS = 32
T_ROWS = NSA_WIDTH + 2 * NSA_KV_WIDTH + NSA_GATE_ROWS


def _proj_kernel(x_ref, g_ref, w_ref, wt_ref, rw_ref, kvc_ref, ksa_ref, kwa_ref, gate_ref,
                 qt_ref, vst_ref, vwt_ref, gt_ref, *, tiles_per_seq):
    tm = x_ref.shape[0]
    x = x_ref[...]
    ms = jnp.mean(x * x, axis=-1, keepdims=True)
    u = (x * lax.rsqrt(ms + NORM_EPS) * g_ref[...]).astype(BF16)

    col = 0
    for o_ref in (rw_ref, kvc_ref):
        width = o_ref.shape[-1]
        for c in range(0, width, PROJ_CHUNK):
            hi = min(c + PROJ_CHUNK, width)
            o_ref[:, c:hi] = _dot(u, w_ref[:, col + c:col + hi]).astype(o_ref.dtype)
        col += width

    s0 = (pl.program_id(0) % tiles_per_seq) * tm
    blk = (s0 + lax.broadcasted_iota(jnp.int32, (tm, KEY_COLS), 0)) // SEL_BLOCK
    lane = lax.broadcasted_iota(jnp.int32, (tm, KEY_COLS), 1) % 128
    onehot = jnp.where(lane - HEAD_DIM == blk, 1.0, 0.0)
    ks = _dot(u, w_ref[:, col:col + KEY_COLS]) + onehot
    kw = _dot(u, w_ref[:, col + KEY_COLS:col + 2 * KEY_COLS])
    for hk in range(NSA_KV_HEADS):
        ksa_ref[0, hk] = ks[:, hk * 128:(hk + 1) * 128].astype(ksa_ref.dtype)
        kwa_ref[0, hk] = kw[:, hk * 128:(hk + 1) * 128].astype(kwa_ref.dtype)
    col += 2 * KEY_COLS

    width = gate_ref.shape[-1]
    for c in range(0, width, PROJ_CHUNK):
        hi = min(c + PROJ_CHUNK, width)
        gate_ref[:, c:hi] = _dot(u, w_ref[:, col + c:col + hi]).astype(gate_ref.dtype)

    t = _dot_nt(wt_ref[...], u)
    row = 0
    for o_ref in (qt_ref, vst_ref, vwt_ref, gt_ref):
        n = o_ref.shape[1]
        o_ref[0] = t[row:row + n].astype(o_ref.dtype)
        row += n


def _proj_call(x2, g, w_all, wt_all, seq):
    T, D = x2.shape
    tm = PROJ_TM
    B = T // seq
    tps = seq // tm
    rows = lambda w: pl.BlockSpec((tm, w), lambda i: (i, 0))
    full = lambda a: pl.BlockSpec(a.shape, lambda i: (0,) * a.ndim)
    keys = pl.BlockSpec((1, NSA_KV_HEADS, tm, 128), lambda i: (i // tps, 0, i % tps, 0))
    tcols = lambda n: pl.BlockSpec((1, n, tm), lambda i: (i // tps, 0, i % tps))
    sds = jax.ShapeDtypeStruct
    return pl.pallas_call(
        functools.partial(_proj_kernel, tiles_per_seq=tps),
        grid=(T // tm,),
        in_specs=[rows(D), full(g), full(w_all), full(wt_all)],
        out_specs=[rows(RW_COLS), rows(KVC_COLS), keys, keys, rows(GATE_COLS),
                   tcols(NSA_WIDTH), tcols(NSA_KV_WIDTH), tcols(NSA_KV_WIDTH), tcols(NSA_GATE_ROWS)],
        out_shape=[sds((T, RW_COLS), BF16), sds((T, KVC_COLS), BF16),
                   sds((B, NSA_KV_HEADS, seq, 128), BF16), sds((B, NSA_KV_HEADS, seq, 128), BF16),
                   sds((T, GATE_COLS), BF16),
                   sds((B, NSA_WIDTH, seq), BF16), sds((B, NSA_KV_WIDTH, seq), BF16),
                   sds((B, NSA_KV_WIDTH, seq), BF16), sds((B, NSA_GATE_ROWS, seq), BF16)],
        compiler_params=pltpu.CompilerParams(
            dimension_semantics=("arbitrary",), vmem_limit_bytes=VMEM_LIMIT),
        name="proj",
    )(x2, g, w_all, wt_all)


def _rwkv_kernel(x_ref, mu_ref, vec_ref, w2a2_ref, g2_ref, o_ref, state_ref, prev_ref):
    C = RW_CHUNK
    GW = RW_GROUP_W
    W = RW_WIDTH
    NB = x_ref.shape[0]
    R = NB * C
    t_idx = pl.program_id(1)
    ops = {}

    @pl.when(t_idx == 0)
    def _():
        state_ref[...] = jnp.zeros_like(state_ref)
        prev_ref[...] = jnp.zeros_like(prev_ref)

    mu = mu_ref[...]
    w0, a0, k_k, k_a, r_k, lnx_g, lnx_b = (vec_ref[i:i + 1, :] for i in range(7))

    gr = lax.broadcasted_iota(jnp.int32, (GW, GW), 0) // HEAD_DIM
    gc = lax.broadcasted_iota(jnp.int32, (GW, GW), 1) // HEAD_DIM
    blk = gr == gc
    ones_bd = jnp.where(blk, 1.0, 0.0).astype(BF16)

    def headsums(zs):
        parts = []
        for z in zs:
            zb = z.astype(BF16)
            parts += [zb[:, :GW], zb[:, GW:]]
        s = _dot(jnp.concatenate(parts, axis=0), ones_bd)
        return [jnp.concatenate([s[2 * R * i:2 * R * i + R], s[2 * R * i + R:2 * R * (i + 1)]], axis=1)
                for i in range(len(zs))]

    t_n = lax.broadcasted_iota(jnp.int32, (C, GW), 0)
    s_n = lax.broadcasted_iota(jnp.int32, (C, GW), 1) % HEAD_DIM
    strict = t_n > s_n
    incl = t_n >= s_n
    eye_n = jnp.where(t_n == s_n, 1.0, 0.0)

    def bd(z):
        z4 = jnp.concatenate([z.astype(F32)] * RW_GROUP, axis=0)
        return jnp.where(blk, z4, 0.0).astype(BF16)

    def prepare():
        x = x_ref[...].reshape(R, RW_COLS).astype(F32)
        rolled = pltpu.roll(x, 1, axis=0)
        row8 = lax.broadcasted_iota(jnp.int32, (8, 1), 0)
        pieces = []
        for bi in range(NB):
            pieces.append(jnp.where(row8 == 0, prev_ref[bi, 0:1, :], rolled[bi * C:bi * C + 8]))
            pieces.append(rolled[bi * C + 8:(bi + 1) * C])
            prev_ref[bi, 0:1, :] = x[(bi + 1) * C - 1:(bi + 1) * C, :]
        xs = jnp.concatenate(pieces, axis=0)

        def lerp(j):
            cur = x[:, j * W:(j + 1) * W]
            return cur + (xs[:, j * W:(j + 1) * W] - cur) * mu[j:j + 1, :]

        r, k, v = lerp(0), lerp(1), lerp(2)
        o = 3 * W
        pre_a = x[:, o:o + 128] + xs[:, o + 128:o + 256]
        lane = lax.broadcasted_iota(jnp.int32, (R, 128), 1)
        h_a = jnp.where(lane < RW_DECAY_LORA, jnp.tanh(pre_a), pre_a)
        lwa = _dot(h_a.astype(BF16), w2a2_ref[...])
        o += 256
        pre_g = x[:, o:o + RW_GATE_PAD] + xs[:, o + RW_GATE_PAD:o + 2 * RW_GATE_PAD]
        g = _dot(_sigmoid(pre_g).astype(BF16), g2_ref[...])
        ld = (-DECAY_SCALE_LOG2) * _sigmoid(w0 + lwa[:, :W])
        a = _sigmoid(a0 + lwa[:, W:])
        kkr = k * k_k
        k2 = k * (1.0 + (a - 1.0) * k_a)
        kk_ss, bonus = headsums([kkr * kkr, r * k2 * r_k])
        kk = kkr / jnp.maximum(jnp.sqrt(kk_ss), 1e-12)
        b = kk * a
        tr = lax.broadcasted_iota(jnp.int32, (R, R), 0)
        tc = lax.broadcasted_iota(jnp.int32, (R, R), 1)
        tri = jnp.where((tr >= tc) & (tr // C == tc // C), 1.0, 0.0).astype(BF16)
        l_inc = _dot(tri, jnp.concatenate(_split2(ld), axis=1))
        l_inc = l_inc[:, :W] + l_inc[:, W:]
        l_end = [l_inc[(bi + 1) * C - 1:(bi + 1) * C] for bi in range(NB)]
        e_neg = jnp.exp2(-l_inc)
        e_tail = jnp.concatenate([jnp.exp2(l_end[bi] - l_inc[bi * C:(bi + 1) * C]) for bi in range(NB)],
                                 axis=0)
        new_opb = (-kk * jnp.exp2(l_inc - ld), r * jnp.exp2(l_inc), b * e_neg, k2 * e_neg,
                   b * e_tail, k2 * e_tail)
        cast = (True, True, False, False, True, True)
        ops['matmul'] = tuple(z.astype(BF16) if c else z for z, c in zip(new_opb, cast))
        ops['output'] = (v, bonus, g)
        ops['decay'] = jnp.concatenate([jnp.broadcast_to(jnp.exp2(le), (8, W)) for le in l_end], axis=0)

    def consume():
        n_grp = W // GW
        chains = [(bi, gi) for bi in range(NB) for gi in range(n_grp)]
        cut = lambda z, c: z[c[0] * C:(c[0] + 1) * C, c[1] * GW:(c[1] + 1) * GW]
        each = lambda f, *lists: [f(*args) for args in zip(*lists)]

        p_a_hat, p_r_hat, p_b_hat, p_k_hat, p_b_tail, p_k_tail = ops['matmul']
        p_v, p_bonus, p_g = ops['output']
        p_elc = ops['decay']
        a_h = [cut(p_a_hat, c) for c in chains]
        r_h = [cut(p_r_hat, c) for c in chains]
        ar = each(lambda x1, x2: jnp.concatenate([x1, x2], axis=0), a_h, r_h)
        m1 = each(_dot_nt, ar, [bd(cut(p_b_hat, c)) for c in chains])
        m2 = each(_dot_nt, ar, [bd(cut(p_k_hat, c)) for c in chains])
        m_ab = [jnp.where(strict, m[:C], 0.0) for m in m1]
        m_rb = [jnp.where(incl, m[C:], 0.0) for m in m1]
        m_ak = [jnp.where(strict, m[:C], 0.0) for m in m2]
        m_rk = [jnp.where(incl, m[C:], 0.0) for m in m2]

        tinv = [eye_n + m for m in m_ab]
        p = each(lambda m: _dot(m.astype(BF16), bd(m)), m_ab)
        power = 2
        while 2 * power < C:
            tp = each(lambda t, q: _dot(jnp.concatenate([t, q], axis=0).astype(BF16), bd(q)), tinv, p)
            tinv = each(lambda t, x1: t + x1[:C], tinv, tp)
            p = [x1[C:] for x1 in tp]
            power *= 2
        tinv = each(lambda t, q: t + _dot(t.astype(BF16), bd(q)), tinv, p)

        s_old = [state_ref[i * GW:(i + 1) * GW, :] for i in range(len(chains))]
        s_bf = [s.astype(BF16) for s in s_old]
        vg = [cut(p_v, c) for c in chains]
        bd_v = [bd(x1) for x1 in vg]
        xz = each(lambda x1, s, m, bv: _dot_nt(x1, s) + _dot(m.astype(BF16), bv), a_h, s_bf, m_ak, bd_v)
        u = each(lambda t, x1: _dot(t.astype(BF16), bd(x1)), tinv, xz)
        y = each(lambda x1, s, mb, mk, uu, bv:
                 _dot_nt(x1, s) + _dot(jnp.concatenate([mb, mk], axis=1).astype(BF16),
                                       jnp.concatenate([bd(uu), bv], axis=0)),
                 r_h, s_bf, m_rb, m_rk, u, bd_v)
        new_states = []
        for c, uu, vv, s in zip(chains, u, vg, s_old):
            upd = _dot_tn(jnp.concatenate([uu, vv], axis=0).astype(BF16),
                          jnp.concatenate([cut(p_b_tail, c), cut(p_k_tail, c)], axis=0))
            decay = p_elc[c[0] * 8:c[0] * 8 + 1, c[1] * GW:(c[1] + 1) * GW]
            new_states.append(s * decay + jnp.where(blk, upd, 0.0))
        state_ref[...] = jnp.concatenate(new_states, axis=0)
        y_rows = [jnp.concatenate(y[bi * n_grp:(bi + 1) * n_grp], axis=1) for bi in range(NB)]
        y = jnp.concatenate(y_rows, axis=0)
        mean = headsums([y])[0] * (1.0 / HEAD_DIM)
        yc = y - mean
        var = headsums([yc * yc])[0] * (1.0 / HEAD_DIM)
        yn = yc * lax.rsqrt(var + RW_LNX_EPS) * lnx_g + lnx_b
        yn = yn + p_bonus * p_v
        o_ref[...] = (yn * p_g).reshape(NB, C, W).astype(o_ref.dtype)

    prepare()
    consume()


RW_SEQS_PER_STEP = 8


def _rwkv_call(rw3, mu, vecs, w2a2, g2p):
    B, S, _ = rw3.shape
    C = RW_CHUNK
    nb = RW_SEQS_PER_STEP if B % RW_SEQS_PER_STEP == 0 else 1
    n_groups = RW_WIDTH // RW_GROUP_W
    return pl.pallas_call(
        _rwkv_kernel,
        grid=(B // nb, S // C),
        in_specs=[
            pl.BlockSpec((nb, C, RW_COLS), lambda b, t: (b, t, 0)),
            pl.BlockSpec(mu.shape, lambda b, t: (0, 0)),
            pl.BlockSpec(vecs.shape, lambda b, t: (0, 0)),
            pl.BlockSpec(w2a2.shape, lambda b, t: (0, 0)),
            pl.BlockSpec(g2p.shape, lambda b, t: (0, 0)),
        ],
        out_specs=pl.BlockSpec((nb, C, RW_WIDTH), lambda b, t: (b, t, 0)),
        out_shape=jax.ShapeDtypeStruct((B, S, RW_WIDTH), BF16),
        scratch_shapes=[
            pltpu.VMEM((nb * n_groups * RW_GROUP_W, RW_GROUP_W), F32),
            pltpu.VMEM((nb, 8, RW_COLS), F32),
        ],
        compiler_params=pltpu.CompilerParams(
            dimension_semantics=("arbitrary", "arbitrary"), vmem_limit_bytes=VMEM_LIMIT),
        name="rwkv",
    )(rw3, mu, vecs, w2a2, g2p)


def _compress_kernel(kv_ref, wab_ref, pos_ref, w1_ref, w2_ref, w2t_ref, kcb_ref, vcbt_ref):
    n_half = kv_ref.shape[1]
    for j in range(2):
        pab = _dot(kv_ref[0], wab_ref[j])
        half = NSA_KV_HEADS * CMP_HIDDEN
        pa, pb = pab[:, :half], pab[:, half:]
        pb = pltpu.roll(pb, n_half - 1, axis=0)
        pos_term = _dot(pos_ref[j], w1_ref[j])[0:1]
        hid = pa + pb + jnp.concatenate([pos_term] * NSA_KV_HEADS, axis=1)
        act = (hid * _sigmoid(hid)).astype(BF16)
        for hk in range(NSA_KV_HEADS):
            a_h = act[:, hk * CMP_HIDDEN:(hk + 1) * CMP_HIDDEN]
            if j == 0:
                kcb_ref[0, hk] = _dot(a_h, w2_ref[j]).astype(kcb_ref.dtype)
            else:
                vcbt_ref[0, hk] = _dot_nt(w2t_ref[j], a_h).astype(vcbt_ref.dtype)


def _compress_call(kv, wab, pos8, w1, w2, w2t):
    B, n_half, width = kv.shape
    full = lambda a: pl.BlockSpec(a.shape, lambda b: (0,) * a.ndim)
    return pl.pallas_call(
        _compress_kernel,
        grid=(B,),
        in_specs=[pl.BlockSpec((1, n_half, width), lambda b: (b, 0, 0)),
                  full(wab), full(pos8), full(w1), full(w2), full(w2t)],
        out_specs=[pl.BlockSpec((1, NSA_KV_HEADS, n_half, HEAD_DIM), lambda b: (b, 0, 0, 0)),
                   pl.BlockSpec((1, NSA_KV_HEADS, HEAD_DIM, n_half), lambda b: (b, 0, 0, 0))],
        out_shape=[jax.ShapeDtypeStruct((B, NSA_KV_HEADS, n_half, HEAD_DIM), BF16),
                   jax.ShapeDtypeStruct((B, NSA_KV_HEADS, HEAD_DIM, n_half), BF16)],
        compiler_params=pltpu.CompilerParams(
            dimension_semantics=("arbitrary",), vmem_limit_bytes=VMEM_LIMIT),
        name="nsa_compress",
    )(kv, wab, pos8, w1, w2, w2t)


NSA2_TQ = 256
NSA_AUG = 128
NSA_BIAS_ROWS = 32


def _nsa2_kernel(qt_ref, gt_ref, ks_ref, vst_ref, kw_ref, vwt_ref, kcb_ref, vcbt_ref, ovt_ref,
                 tri_ref, wbias_ref, o_ref, *, seq):
    tq = NSA2_TQ
    G = NSA_GROUP
    R = G * tq
    n_half = kcb_ref.shape[2]
    n_cmp = n_half - 1
    n_sel = seq // SEL_BLOCK
    n_top = min(SEL_TOP, n_sel)
    n_wchunks = WINDOW // tq + 1
    step = pl.program_id(1)
    q0 = step * tq

    t_lane = q0 + lax.broadcasted_iota(jnp.int32, (1, R), 1) % tq
    gates = _sigmoid(gt_ref[0].astype(F32))
    pad_rows = jnp.zeros((NSA_AUG - HEAD_DIM - NSA_BIAS_ROWS, tq), BF16)

    hrow = lambda hk, g: slice((hk * G + g) * HEAD_DIM, (hk * G + g + 1) * HEAD_DIM)
    o_cmp, qaug = {}, {}
    for hk in range(NSA_KV_HEADS):
        q64 = jnp.concatenate([qt_ref[0, hrow(hk, g), :] for g in range(G)], axis=1)
        q64 = q64 * jnp.asarray(HEAD_DIM ** -0.5, BF16)

        cidx = lax.broadcasted_iota(jnp.int32, (n_half, R), 0)
        cvalid = (cidx * CMP_STRIDE + (CMP_BLOCK - 1) <= t_lane) & (cidx < n_cmp)
        s = jnp.where(cvalid, _dot(kcb_ref[0, hk], q64), NEG_INF)
        m = jnp.max(s, axis=0, keepdims=True)
        e = jnp.where(cvalid, jnp.exp(s - m), 0.0)
        l = jnp.sum(e, axis=0, keepdims=True)
        p_c = e / jnp.where(l > 0.0, l, 1.0)
        o_c = _dot(vcbt_ref[0, hk], p_c.astype(BF16))

        psum = p_c[:, 0:tq]
        for g in range(1, G):
            psum = psum + p_c[:, g * tq:(g + 1) * tq]
        hi, lo = _split2(psum)
        imp2 = _dot(ovt_ref[...], jnp.concatenate([hi, lo], axis=1))
        imp = imp2[:, :tq] + imp2[:, tq:]
        jblk = lax.broadcasted_iota(jnp.int32, (n_sel, tq), 0)
        cur = (q0 + lax.broadcasted_iota(jnp.int32, (n_sel, tq), 1)) // SEL_BLOCK
        forced = (jblk == 0) | (jblk == cur) | (jblk == cur - 1)
        score = jnp.where(forced, SEL_FORCE_SCORE, jnp.where(jblk <= cur, imp, -1.0))
        n_grp = n_sel // 8
        rows = [score[8 * q:8 * q + 8] for q in range(n_grp)]
        ranks = [jnp.zeros((8, tq), F32)] * n_grp
        sub8 = lax.broadcasted_iota(jnp.int32, (8, tq), 0)
        for j in range(n_sel):
            sj = score[j:j + 1, :]
            for q in range(n_grp):
                wins = jnp.where(sj > rows[q], 1.0, 0.0)
                wins_ties = jnp.where(sj >= rows[q], 1.0, 0.0)
                if 8 * q + 7 < j:
                    ahead = wins
                elif 8 * q > j:
                    ahead = wins_ties
                else:
                    ahead = jnp.where(sub8 > j - 8 * q, wins_ties, wins)
                ranks[q] = ranks[q] + ahead
        rank = jnp.concatenate(ranks, axis=0)
        sel_bias = jnp.where(rank < n_top, 0.0, NEG_INF).astype(BF16)
        q_l2 = (q64.astype(F32) * LOG2_E).astype(BF16)
        for g in range(G):
            o_cmp[hk, g] = o_c[:, g * tq:(g + 1) * tq]
            qaug[hk, g] = jnp.concatenate([q_l2[:, g * tq:(g + 1) * tq], sel_bias, pad_rows], axis=0)

    chains = [(hk, g) for hk in range(NSA_KV_HEADS) for g in range(G)]

    def softmax_pv(s_list, vt_of, carry=None):
        m_blk = [jnp.max(s, axis=0, keepdims=True) for s in s_list]
        if carry is None:
            m_new = m_blk
        else:
            m_new = [jnp.maximum(c[0], mb) for c, mb in zip(carry, m_blk)]
        p = [jnp.exp2(s - mn) for s, mn in zip(s_list, m_new)]
        l_blk = [jnp.sum(x, axis=0, keepdims=True) for x in p]
        pv = [_dot(vt_of(c), x.astype(BF16)) for c, x in zip(chains, p)]
        if carry is None:
            return [(mn, lb, a) for mn, lb, a in zip(m_new, l_blk, pv)]
        alpha = [jnp.exp2(c[0] - mn) for c, mn in zip(carry, m_new)]
        return [(mn, c[1] * al + lb, c[2] * al + a)
                for c, mn, al, lb, a in zip(carry, m_new, alpha, l_blk, pv)]

    def vcols(ref, hk, j):
        return ref[0, hk * HEAD_DIM:(hk + 1) * HEAD_DIM, pl.ds(pl.multiple_of(j * tq, tq), tq)]

    wchunk = [jnp.maximum(step - (n_wchunks - 1) + w, 0) for w in range(n_wchunks)]
    kw_rows = [jnp.concatenate([kw_ref[0, hk, j] for j in wchunk], axis=0) for hk in range(NSA_KV_HEADS)]
    vw_cols = [jnp.concatenate([vcols(vwt_ref, hk, j) for j in wchunk], axis=1)
               for hk in range(NSA_KV_HEADS)]
    wbias = wbias_ref[jnp.minimum(step, n_wchunks - 1)]
    s_win = [_dot(kw_rows[hk], qaug[hk, g]) + wbias for hk, g in chains]
    win = softmax_pv(s_win, lambda c: vw_cols[c[0]])

    tri = tri_ref[...]
    s_diag = [_dot(ks_ref[0, hk, step], qaug[hk, g]) + tri for hk, g in chains]
    carry = softmax_pv(s_diag, lambda c: vcols(vst_ref, c[0], step))

    def body(j, flat):
        carry = [tuple(flat[3 * i:3 * i + 3]) for i in range(len(chains))]
        s_j = [_dot(ks_ref[0, hk, j], qaug[hk, g]) for hk, g in chains]
        new = softmax_pv(s_j, lambda c: vcols(vst_ref, c[0], j), carry)
        return tuple(x for c in new for x in c)

    flat = lax.fori_loop(0, step, body, tuple(x for c in carry for x in c))
    sel = [tuple(flat[3 * i:3 * i + 3]) for i in range(len(chains))]

    for i, (hk, g) in enumerate(chains):
        gate = lambda j: gates[(hk * G + g) * 3 + j:(hk * G + g) * 3 + j + 1, :]
        out = (gate(0) * o_cmp[hk, g] + gate(1) * (sel[i][2] / sel[i][1])
               + gate(2) * (win[i][2] / win[i][1]))
        o_ref[0, hrow(hk, g), :] = out.astype(o_ref.dtype)


def _nsa2_call(qt, gt, ks5, vst, kw5, vwt, kcb, vcbt, ovt, tri, wbias, seq):
    B = qt.shape[0]
    tq = NSA2_TQ
    per_b = lambda a: pl.BlockSpec((1,) + a.shape[1:], lambda b, i: (b,) + (0,) * (a.ndim - 1))
    full = lambda a: pl.BlockSpec(a.shape, lambda b, i: (0,) * a.ndim)
    return pl.pallas_call(
        functools.partial(_nsa2_kernel, seq=seq),
        grid=(B, seq // tq),
        in_specs=[pl.BlockSpec((1, NSA_WIDTH, tq), lambda b, i: (b, 0, i)),
                  pl.BlockSpec((1, gt.shape[1], tq), lambda b, i: (b, 0, i)),
                  per_b(ks5), per_b(vst), per_b(kw5), per_b(vwt), per_b(kcb), per_b(vcbt),
                  full(ovt), full(tri), full(wbias)],
        out_specs=pl.BlockSpec((1, NSA_WIDTH, tq), lambda b, i: (b, 0, i)),
        out_shape=jax.ShapeDtypeStruct((B, NSA_WIDTH, seq), BF16),
        compiler_params=pltpu.CompilerParams(
            dimension_semantics=("arbitrary", "arbitrary"), vmem_limit_bytes=VMEM_LIMIT),
        name="nsa_attention",
    )(qt, gt, ks5, vst, kw5, vwt, kcb, vcbt, ovt, tri, wbias)


def _nsa_from_proj(kvc, ksa, kwa, qt, vst, vwt, gt, P):
    B, Hk, S, _ = ksa.shape
    tq = NSA2_TQ
    Dh = HEAD_DIM
    n_sel = S // SEL_BLOCK
    assert S % tq == 0 and WINDOW % tq == 0 and n_sel == NSA_BIAS_ROWS
    n_half = S // CMP_STRIDE
    kv = kvc.reshape(B, n_half, CMP_STRIDE * KVC_COLS)
    w1 = P['nsa_cmp_w1'][0]
    w1h = w1.reshape(2, 2, CMP_STRIDE, HEAD_DIM, CMP_HIDDEN)
    pick = np.zeros((2, 2 * Hk, Hk), np.float32)
    for j in range(2):
        for hk in range(Hk):
            pick[j, j * Hk + hk, hk] = 1.0
    wab = jnp.einsum('jaldn,jch->jlcdahn', w1h, jnp.asarray(pick)).reshape(
        2, CMP_STRIDE * KVC_COLS, 2 * Hk * CMP_HIDDEN).astype(BF16)
    pos8 = jnp.broadcast_to(P['nsa_cmp_pos'][0].reshape(2, 1, CMP_BLOCK * HEAD_DIM),
                            (2, 8, CMP_BLOCK * HEAD_DIM)).astype(BF16)
    w2 = P['nsa_cmp_w2'][0].astype(BF16)
    kcb, vcbt = _compress_call(kv, wab, pos8, w1.astype(BF16), w2, jnp.swapaxes(w2, 1, 2))

    ks5 = ksa.reshape(B, Hk, S // tq, tq, NSA_AUG)
    kw5 = kwa.reshape(B, Hk, S // tq, tq, NSA_AUG)

    n_cmp = (S - CMP_BLOCK) // CMP_STRIDE + 1
    cmp_start = np.arange(n_half) * CMP_STRIDE
    sel_start = np.arange(n_sel) * SEL_BLOCK
    overlap = ((cmp_start[:, None] <= sel_start[None, :] + SEL_BLOCK - 1)
               & (cmp_start[:, None] + CMP_BLOCK - 1 >= sel_start[None, :])
               & (np.arange(n_half)[:, None] < n_cmp)).astype(np.float32)
    tri = np.where(np.arange(tq)[:, None] <= np.arange(tq)[None, :], 0.0, NEG_INF).astype(np.float32)
    n_w = WINDOW // tq
    masked = np.full((tq, tq), NEG_INF, np.float32)
    clear = np.zeros((tq, tq), np.float32)
    wbias = np.stack([np.concatenate([masked] * (n_w - v) + [clear] * v + [tri], axis=0) for v in range(n_w)]
                     + [np.concatenate([NEG_INF - tri] + [clear] * (n_w - 1) + [tri], axis=0)])
    return _nsa2_call(qt, gt, ks5, vst, kw5, vwt, kcb, vcbt, jnp.asarray(overlap.T, BF16),
                      jnp.asarray(tri), jnp.asarray(wbias), S)


def _merge_kernel(x_ref, ya_ref, ybt_ref, gate_ref, wa_ref, wb_ref, wo_ref, o_ref):
    D = x_ref.shape[-1]
    ta = _dot(ya_ref[...], wa_ref[...])
    tb = _dot_tn(ybt_ref[0], wb_ref[...])
    ga = _sigmoid(gate_ref[:, :D].astype(F32))
    gb = _sigmoid(gate_ref[:, D:].astype(F32))
    mix = (ga * ta + gb * tb).astype(BF16)
    o_ref[...] = x_ref[...] + _dot(mix, wo_ref[...])


def _merge_call(x2, ya2, ybt, gates, wa, wb, wo, tm=512):
    T, D = x2.shape
    tiles_per_seq = ybt.shape[2] // tm
    row = lambda w: pl.BlockSpec((tm, w), lambda i: (i, 0))
    full = lambda a: pl.BlockSpec(a.shape, lambda i: (0,) * a.ndim)
    ybt_spec = pl.BlockSpec((1, ybt.shape[1], tm), lambda i: (i // tiles_per_seq, 0, i % tiles_per_seq))
    return pl.pallas_call(
        _merge_kernel,
        grid=(T // tm,),
        in_specs=[row(D), row(ya2.shape[1]), ybt_spec, row(gates.shape[1]),
                  full(wa), full(wb), full(wo)],
        out_specs=row(D),
        out_shape=jax.ShapeDtypeStruct((T, D), F32),
        compiler_params=pltpu.CompilerParams(
            dimension_semantics=("arbitrary",), vmem_limit_bytes=VMEM_LIMIT),
        name="merge",
    )(x2, ya2, ybt, gates, wa, wb, wo)


FFN_HALO = 8


def _rms(x, g):
    return x * lax.rsqrt(jnp.mean(x * x, axis=-1, keepdims=True) + NORM_EPS) * g


def _ffn_kernel(h_ref, halo_ref, p_ref, ln_ref, wup_ref, cw_ref, cb_ref, wdn_ref, wpg_ref, wpp_ref,
                o_ref, up0a_ref, up0b_ref, up1a_ref, up1b_ref, act_ref, *, tiles_per_seq, fc):
    up_refs = ((up0a_ref, up0b_ref), (up1a_ref, up1b_ref))
    tm = act_ref.shape[0]
    D = halo_ref.shape[1]
    V = tm // 8
    d_ff = wdn_ref.shape[0]
    slabs = lambda ref, w: jnp.swapaxes(ref[0], 0, 1).reshape(tm, w)
    h = slabs(h_ref, D)
    first = (pl.program_id(0) % tiles_per_seq) == 0
    halo = jnp.where(first, 0.0, halo_ref[...])
    ln2, ln3, lnf = ln_ref[0:1, :], ln_ref[1:2, :], ln_ref[2:3, :]
    u = jnp.concatenate([_rms(halo, ln2), _rms(h, ln2)], axis=0).astype(BF16)

    n_chunks = d_ff // fc
    sub = lax.broadcasted_iota(jnp.int32, (8, 1), 0)

    def project(c):
        for half in range(2):
            col = half * d_ff + c * fc
            up_refs[c % 2][half][...] = _dot(u, wup_ref[:, col:col + fc])

    def conv(c, half):
        ref = up_refs[c % 2][half]
        col = half * d_ff + c * fc
        halo_up = ref[0:FFN_HALO, :]
        last = lambda k: ref[FFN_HALO + tm - 8 * k:FFN_HALO + tm - 8 * (k - 1), :]
        wrap1 = pltpu.roll(jnp.where(sub == 7, halo_up, last(1)), 1, axis=0)
        wrap2 = pltpu.roll(jnp.where(sub == 7, pltpu.roll(halo_up, 1, axis=0), last(2)), 1, axis=0)
        x0 = ref[FFN_HALO:FFN_HALO + tm, :]
        x1 = jnp.concatenate([wrap1, ref[FFN_HALO:FFN_HALO + tm - 8, :]], axis=0)
        x2 = jnp.concatenate([wrap2, wrap1, ref[FFN_HALO:FFN_HALO + tm - 16, :]], axis=0)
        tap = lambda j: cw_ref[j:j + 1, col:col + fc]
        return cb_ref[:, col:col + fc] + tap(0) * x2 + tap(1) * x1 + tap(2) * x0

    project(0)
    for c in range(n_chunks):
        if c + 1 < n_chunks:
            project(c + 1)
        a = conv(c, 0)
        b = conv(c, 1)
        act_ref[:, c * fc:(c + 1) * fc] = (a * _sigmoid(a) * b).astype(BF16)
    h2 = h + _dot(act_ref[...], wdn_ref[...])
    gate = _sigmoid(_dot(_rms(h2, ln3).astype(BF16), wpg_ref[...]))
    h3 = h2 + gate * _dot(slabs(p_ref, p_ref.shape[3]).astype(BF16), wpp_ref[...])
    o_ref[0] = jnp.swapaxes(_rms(h3, lnf).reshape(V, 8, D), 0, 1)


FFN_TM = 512
FFN_FC = 256


def _ffn_call(h2d, p2d, lns, wup, cw, cb, wdn, wpg, wpp, seq):
    T, D = h2d.shape
    tm, fc = FFN_TM, FFN_FC
    assert CONV_WIDTH == 3 and seq % tm == 0 and wdn.shape[0] % fc == 0
    tiles_per_seq = seq // tm
    runs = lambda x: x.reshape(T // tm, 8, tm // 8, x.shape[1])
    run_spec = lambda w: pl.BlockSpec((1, 8, tm // 8, w), lambda i: (i, 0, 0, 0))
    full = lambda a: pl.BlockSpec(a.shape, lambda i: (0,) * a.ndim, pipeline_mode=pl.Buffered(1))
    halo = pl.BlockSpec((FFN_HALO, D), lambda i: (jnp.maximum(i * (tm // FFN_HALO) - 1, 0), 0))
    out = pl.pallas_call(
        functools.partial(_ffn_kernel, tiles_per_seq=tiles_per_seq, fc=fc),
        grid=(T // tm,),
        in_specs=[run_spec(D), halo, run_spec(p2d.shape[1]), full(lns), full(wup), full(cw), full(cb),
                  full(wdn), full(wpg), full(wpp)],
        out_specs=run_spec(D),
        out_shape=jax.ShapeDtypeStruct((T // tm, 8, tm // 8, D), F32),
        scratch_shapes=[pltpu.VMEM((FFN_HALO + tm, fc), F32)] * 4 + [pltpu.VMEM((tm, wdn.shape[0]), BF16)],
        compiler_params=pltpu.CompilerParams(
            dimension_semantics=("arbitrary",), vmem_limit_bytes=VMEM_LIMIT),
        name="ffn",
    )(runs(h2d), h2d, runs(p2d), lns, wup, cw, cb, wdn, wpg, wpp)
    return out.reshape(T, D)


def _prep_proj_weights(w_in, mu_wag, w1, a1, g1):
    D = w_in.shape[0]
    sizes = (RW_WIDTH, RW_WIDTH, RW_WIDTH, NSA_WIDTH) + (NSA_KV_WIDTH,) * 6 + (3 * NSA_Q_HEADS, D, D)
    offs = np.concatenate([[0], np.cumsum(sizes)])
    part = lambda i, j: w_in[:, offs[i]:offs[j]]
    mw, ma, mg = mu_wag[0][:, None], mu_wag[1][:, None], mu_wag[2][:, None]
    zg = jnp.zeros((D, RW_GATE_PAD - RW_GATE_LORA), F32)
    rw = jnp.concatenate([
        part(0, 3),
        (1.0 - mw) * w1, (1.0 - ma) * a1,
        mw * w1, ma * a1,
        (1.0 - mg) * g1, zg,
        mg * g1, zg], axis=1)
    def widen(w):
        w = w.reshape(D, NSA_KV_HEADS, HEAD_DIM)
        return jnp.concatenate([w, jnp.zeros_like(w)], axis=2).reshape(D, KEY_COLS)

    w_rows = jnp.concatenate([rw, part(4, 6), widen(part(6, 7)), widen(part(8, 9)), part(11, 13)], axis=1)
    w_cols = jnp.concatenate([part(3, 4), part(7, 8), part(9, 10), part(10, 11),
                              jnp.zeros((D, NSA_GATE_ROWS - 3 * NSA_Q_HEADS), F32)], axis=1)
    return w_rows.astype(BF16), w_cols.T.astype(BF16)


def _prep_rwkv_weights(w2, a2, g2):
    z = jnp.zeros_like(w2)
    w2a2 = jnp.concatenate([jnp.concatenate([w2, z], axis=1),
                            jnp.concatenate([z, a2], axis=1)], axis=0).astype(BF16)
    g2p = jnp.concatenate([g2, jnp.zeros((RW_GATE_PAD - RW_GATE_LORA, RW_WIDTH), F32)],
                          axis=0).astype(BF16)
    return w2a2, g2p


def _rwkv_from_proj(rw3, P):
    w2a2, g2p = _prep_rwkv_weights(P['rw_w2'][0], P['rw_a2'][0], P['rw_g2'][0])
    vecs = jnp.stack([P['rw_w0'][0], P['rw_a0'][0], P['rw_k_k'][0], P['rw_k_a'][0],
                      P['rw_r_k'][0].reshape(-1), P['rw_lnx_g'][0], P['rw_lnx_b'][0],
                      jnp.zeros((RW_WIDTH,), F32)], axis=0)
    return _rwkv_call(rw3, P['rw_mu_rkv'][0], vecs, w2a2, g2p)


def kernel(x, p, ln1_g, w_in, rw_mu_rkv, rw_mu_wag, rw_w0, rw_w1, rw_w2, rw_a0, rw_a1, rw_a2, rw_g1, rw_g2, rw_k_k, rw_k_a, rw_r_k, rw_lnx_g, rw_lnx_b, nsa_cmp_pos, nsa_cmp_w1, nsa_cmp_w2, w_out_a, w_out_b, w_out, ln2_g, w_up, conv_w, conv_b, w_down, ln3_g, w_ple_gate, w_ple_proj, ln_f_g):
    B, S, D = x.shape
    T = B * S
    assert w_in.shape[0] == 1, "single-layer block"
    P = dict(rw_mu_rkv=rw_mu_rkv, rw_w0=rw_w0, rw_w2=rw_w2, rw_a0=rw_a0, rw_a2=rw_a2, rw_g2=rw_g2,
             rw_k_k=rw_k_k, rw_k_a=rw_k_a, rw_r_k=rw_r_k, rw_lnx_g=rw_lnx_g, rw_lnx_b=rw_lnx_b,
             nsa_cmp_pos=nsa_cmp_pos, nsa_cmp_w1=nsa_cmp_w1, nsa_cmp_w2=nsa_cmp_w2)
    h = x.reshape(T, D)
    w_rows, w_cols = _prep_proj_weights(w_in[0], rw_mu_wag[0], rw_w1[0], rw_a1[0], rw_g1[0])
    rw, kvc, ksa, kwa, gates, qt, vst, vwt, gt = _proj_call(h, ln1_g[0][None], w_rows, w_cols, S)
    ya = _rwkv_from_proj(rw.reshape(B, S, RW_COLS), P)
    ybt = _nsa_from_proj(kvc, ksa, kwa, qt, vst, vwt, gt, P)
    h1 = _merge_call(h, ya.reshape(T, RW_WIDTH), ybt, gates,
                     w_out_a[0].astype(BF16), w_out_b[0].astype(BF16), w_out[0].astype(BF16))
    lns = jnp.stack([ln2_g[0], ln3_g[0], ln_f_g], axis=0)
    out = _ffn_call(h1, p[0].reshape(T, -1), lns, w_up[0].astype(BF16), conv_w[0], conv_b[0][None],
                    w_down[0].astype(BF16), w_ple_gate[0].astype(BF16), w_ple_proj[0].astype(BF16), S)
    return out.reshape(B, S, D)
```

```python
import functools

import numpy as np
import jax
import jax.numpy as jnp
from jax import lax
from jax.experimental import pallas as pl
from jax.experimental.pallas import tpu as pltpu

F32 = jnp.float32
BF16 = jnp.bfloat16

HEAD_DIM = 64
NORM_EPS = 1e-6
NEG_INF = -1e30

RW_HEADS = 8
RW_WIDTH = RW_HEADS * HEAD_DIM
RW_DECAY_LORA = 64
RW_AAA_LORA = 64
RW_GATE_LORA = 160
RW_LNX_EPS = 64e-5
RW_CHUNK = 64
RW_GROUP = 4
RW_GROUP_W = RW_GROUP * HEAD_DIM
RW_GATE_PAD = 256

NSA_Q_HEADS = 8
NSA_KV_HEADS = 2
NSA_GROUP = NSA_Q_HEADS // NSA_KV_HEADS
NSA_WIDTH = NSA_Q_HEADS * HEAD_DIM
NSA_KV_WIDTH = NSA_KV_HEADS * HEAD_DIM
CMP_BLOCK = 32
CMP_STRIDE = 16
CMP_HIDDEN = 128
SEL_BLOCK = 64
SEL_TOP = 16
SEL_FORCE_SCORE = 1e4
WINDOW = 512

CONV_WIDTH = 3

RW_COLS = 3 * RW_WIDTH + 2 * 128 + 2 * RW_GATE_PAD
GATE_COLS = 2 * 1024

VMEM_LIMIT = 56 * 1024 * 1024


def _dot(a, b):
    return jnp.dot(a, b, preferred_element_type=F32)


def _dot_nt(a, b):
    return lax.dot_general(a, b, (((1,), (1,)), ((), ())), preferred_element_type=F32)


def _dot_tn(a, b):
    return lax.dot_general(a, b, (((0,), (0,)), ((), ())), preferred_element_type=F32)


def _split2(x):
    hi = x.astype(BF16)
    lo = (x - hi.astype(F32)).astype(BF16)
    return hi, lo


def _split3(x):
    hi = x.astype(BF16)
    r1 = x - hi.astype(F32)
    mid = r1.astype(BF16)
    lo = (r1 - mid.astype(F32)).astype(BF16)
    return hi, mid, lo


def _sigmoid(x):
    return 1.0 / (1.0 + jnp.exp(-x))


LOG2_E = 1.4426950408889634
DECAY_SCALE_LOG2 = 0.6065306597126334 * LOG2_E


PROJ_TM = 512
PROJ_CHUNK = 768
KVC_COLS = 2 * NSA_KV_WIDTH
KEY_COLS = NSA_KV_HEADS * 128
NSA_GATE_ROWS = 32
T_ROWS = NSA_WIDTH + 2 * NSA_KV_WIDTH + NSA_GATE_ROWS


def _proj_kernel(x_ref, g_ref, w_ref, wt_ref, rw_ref, kvc_ref, ksa_ref, kwa_ref, gate_ref,
                 qt_ref, vst_ref, vwt_ref, gt_ref, *, tiles_per_seq):
    tm = x_ref.shape[0]
    x = x_ref[...]
    ms = jnp.mean(x * x, axis=-1, keepdims=True)
    u = (x * lax.rsqrt(ms + NORM_EPS) * g_ref[...]).astype(BF16)

    col = 0
    for o_ref in (rw_ref, kvc_ref):
        width = o_ref.shape[-1]
        for c in range(0, width, PROJ_CHUNK):
            hi = min(c + PROJ_CHUNK, width)
            o_ref[:, c:hi] = _dot(u, w_ref[:, col + c:col + hi]).astype(o_ref.dtype)
        col += width

    s0 = (pl.program_id(0) % tiles_per_seq) * tm
    blk = (s0 + lax.broadcasted_iota(jnp.int32, (tm, KEY_COLS), 0)) // SEL_BLOCK
    lane = lax.broadcasted_iota(jnp.int32, (tm, KEY_COLS), 1) % 128
    onehot = jnp.where(lane - HEAD_DIM == blk, 1.0, 0.0)
    ks = _dot(u, w_ref[:, col:col + KEY_COLS]) + onehot
    kw = _dot(u, w_ref[:, col + KEY_COLS:col + 2 * KEY_COLS])
    for hk in range(NSA_KV_HEADS):
        ksa_ref[0, hk] = ks[:, hk * 128:(hk + 1) * 128].astype(ksa_ref.dtype)
        kwa_ref[0, hk] = kw[:, hk * 128:(hk + 1) * 128].astype(kwa_ref.dtype)
    col += 2 * KEY_COLS

    width = gate_ref.shape[-1]
    for c in range(0, width, PROJ_CHUNK):
        hi = min(c + PROJ_CHUNK, width)
        gate_ref[:, c:hi] = _dot(u, w_ref[:, col + c:col + hi]).astype(gate_ref.dtype)

    t = _dot_nt(wt_ref[...], u)
    row = 0
    for o_ref in (qt_ref, vst_ref, vwt_ref, gt_ref):
        n = o_ref.shape[1]
        o_ref[0] = t[row:row + n].astype(o_ref.dtype)
        row += n


def _proj_call(x2, g, w_all, wt_all, seq):
    T, D = x2.shape
    tm = PROJ_TM
    B = T // seq
    tps = seq // tm
    rows = lambda w: pl.BlockSpec((tm, w), lambda i: (i, 0))
    full = lambda a: pl.BlockSpec(a.shape, lambda i: (0,) * a.ndim)
    keys = pl.BlockSpec((1, NSA_KV_HEADS, tm, 128), lambda i: (i // tps, 0, i % tps, 0))
    tcols = lambda n: pl.BlockSpec((1, n, tm), lambda i: (i // tps, 0, i % tps))
    sds = jax.ShapeDtypeStruct
    return pl.pallas_call(
        functools.partial(_proj_kernel, tiles_per_seq=tps),
        grid=(T // tm,),
        in_specs=[rows(D), full(g), full(w_all), full(wt_all)],
        out_specs=[rows(RW_COLS), rows(KVC_COLS), keys, keys, rows(GATE_COLS),
                   tcols(NSA_WIDTH), tcols(NSA_KV_WIDTH), tcols(NSA_KV_WIDTH), tcols(NSA_GATE_ROWS)],
        out_shape=[sds((T, RW_COLS), BF16), sds((T, KVC_COLS), BF16),
                   sds((B, NSA_KV_HEADS, seq, 128), BF16), sds((B, NSA_KV_HEADS, seq, 128), BF16),
                   sds((T, GATE_COLS), BF16),
                   sds((B, NSA_WIDTH, seq), BF16), sds((B, NSA_KV_WIDTH, seq), BF16),
                   sds((B, NSA_KV_WIDTH, seq), BF16), sds((B, NSA_GATE_ROWS, seq), BF16)],
        compiler_params=pltpu.CompilerParams(
            dimension_semantics=("arbitrary",), vmem_limit_bytes=VMEM_LIMIT),
        name="proj",
    )(x2, g, w_all, wt_all)


def _rwkv_kernel(x_ref, mu_ref, vec_ref, w2a2_ref, g2_ref, o_ref, state_ref, prev_ref):
    C = RW_CHUNK
    GW = RW_GROUP_W
    W = RW_WIDTH
    NB = x_ref.shape[0]
    R = NB * C
    t_idx = pl.program_id(1)
    ops = {}

    @pl.when(t_idx == 0)
    def _():
        state_ref[...] = jnp.zeros_like(state_ref)
        prev_ref[...] = jnp.zeros_like(prev_ref)

    mu = mu_ref[...]
    w0, a0, k_k, k_a, r_k, lnx_g, lnx_b = (vec_ref[i:i + 1, :] for i in range(7))

    gr = lax.broadcasted_iota(jnp.int32, (GW, GW), 0) // HEAD_DIM
    gc = lax.broadcasted_iota(jnp.int32, (GW, GW), 1) // HEAD_DIM
    blk = gr == gc
    ones_bd = jnp.where(blk, 1.0, 0.0).astype(BF16)

    def headsums(zs):
        parts = []
        for z in zs:
            zb = z.astype(BF16)
            parts += [zb[:, :GW], zb[:, GW:]]
        s = _dot(jnp.concatenate(parts, axis=0), ones_bd)
        return [jnp.concatenate([s[2 * R * i:2 * R * i + R], s[2 * R * i + R:2 * R * (i + 1)]], axis=1)
                for i in range(len(zs))]

    t_n = lax.broadcasted_iota(jnp.int32, (C, GW), 0)
    s_n = lax.broadcasted_iota(jnp.int32, (C, GW), 1) % HEAD_DIM
    strict = t_n > s_n
    incl = t_n >= s_n
    eye_n = jnp.where(t_n == s_n, 1.0, 0.0)

    def bd(z):
        z4 = jnp.concatenate([z.astype(F32)] * RW_GROUP, axis=0)
        return jnp.where(blk, z4, 0.0).astype(BF16)

    def prepare():
        x = x_ref[...].reshape(R, RW_COLS).astype(F32)
        rolled = pltpu.roll(x, 1, axis=0)
        row8 = lax.broadcasted_iota(jnp.int32, (8, 1), 0)
        pieces = []
        for bi in range(NB):
            pieces.append(jnp.where(row8 == 0, prev_ref[bi, 0:1, :], rolled[bi * C:bi * C + 8]))
            pieces.append(rolled[bi * C + 8:(bi + 1) * C])
            prev_ref[bi, 0:1, :] = x[(bi + 1) * C - 1:(bi + 1) * C, :]
        xs = jnp.concatenate(pieces, axis=0)

        def lerp(j):
            cur = x[:, j * W:(j + 1) * W]
            return cur + (xs[:, j * W:(j + 1) * W] - cur) * mu[j:j + 1, :]

        r, k, v = lerp(0), lerp(1), lerp(2)
        o = 3 * W
        pre_a = x[:, o:o + 128] + xs[:, o + 128:o + 256]
        lane = lax.broadcasted_iota(jnp.int32, (R, 128), 1)
        h_a = jnp.where(lane < RW_DECAY_LORA, jnp.tanh(pre_a), pre_a)
        lwa = _dot(h_a.astype(BF16), w2a2_ref[...])
        o += 256
        pre_g = x[:, o:o + RW_GATE_PAD] + xs[:, o + RW_GATE_PAD:o + 2 * RW_GATE_PAD]
        g = _dot(_sigmoid(pre_g).astype(BF16), g2_ref[...])
        ld = (-DECAY_SCALE_LOG2) * _sigmoid(w0 + lwa[:, :W])
        a = _sigmoid(a0 + lwa[:, W:])
        kkr = k * k_k
        k2 = k * (1.0 + (a - 1.0) * k_a)
        kk_ss, bonus = headsums([kkr * kkr, r * k2 * r_k])
        kk = kkr / jnp.maximum(jnp.sqrt(kk_ss), 1e-12)
        b = kk * a
        tr = lax.broadcasted_iota(jnp.int32, (R, R), 0)
        tc = lax.broadcasted_iota(jnp.int32, (R, R), 1)
        tri = jnp.where((tr >= tc) & (tr // C == tc // C), 1.0, 0.0).astype(BF16)
        l_inc = _dot(tri, jnp.concatenate(_split2(ld), axis=1))
        l_inc = l_inc[:, :W] + l_inc[:, W:]
        l_end = [l_inc[(bi + 1) * C - 1:(bi + 1) * C] for bi in range(NB)]
        e_neg = jnp.exp2(-l_inc)
        e_tail = jnp.concatenate([jnp.exp2(l_end[bi] - l_inc[bi * C:(bi + 1) * C]) for bi in range(NB)],
                                 axis=0)
        new_opb = (-kk * jnp.exp2(l_inc - ld), r * jnp.exp2(l_inc), b * e_neg, k2 * e_neg,
                   b * e_tail, k2 * e_tail)
        cast = (True, True, False, False, True, True)
        ops['matmul'] = tuple(z.astype(BF16) if c else z for z, c in zip(new_opb, cast))
        ops['output'] = (v, bonus, g)
        ops['decay'] = jnp.concatenate([jnp.broadcast_to(jnp.exp2(le), (8, W)) for le in l_end], axis=0)

    def consume():
        n_grp = W // GW
        chains = [(bi, gi) for bi in range(NB) for gi in range(n_grp)]
        cut = lambda z, c: z[c[0] * C:(c[0] + 1) * C, c[1] * GW:(c[1] + 1) * GW]
        each = lambda f, *lists: [f(*args) for args in zip(*lists)]

        p_a_hat, p_r_hat, p_b_hat, p_k_hat, p_b_tail, p_k_tail = ops['matmul']
        p_v, p_bonus, p_g = ops['output']
        p_elc = ops['decay']
        a_h = [cut(p_a_hat, c) for c in chains]
        r_h = [cut(p_r_hat, c) for c in chains]
        ar = each(lambda x1, x2: jnp.concatenate([x1, x2], axis=0), a_h, r_h)
        m1 = each(_dot_nt, ar, [bd(cut(p_b_hat, c)) for c in chains])
        m2 = each(_dot_nt, ar, [bd(cut(p_k_hat, c)) for c in chains])
        m_ab = [jnp.where(strict, m[:C], 0.0) for m in m1]
        m_rb = [jnp.where(incl, m[C:], 0.0) for m in m1]
        m_ak = [jnp.where(strict, m[:C], 0.0) for m in m2]
        m_rk = [jnp.where(incl, m[C:], 0.0) for m in m2]

        tinv = [eye_n + m for m in m_ab]
        p = each(lambda m: _dot(m.astype(BF16), bd(m)), m_ab)
        power = 2
        while 2 * power < C:
            tp = each(lambda t, q: _dot(jnp.concatenate([t, q], axis=0).astype(BF16), bd(q)), tinv, p)
            tinv = each(lambda t, x1: t + x1[:C], tinv, tp)
            p = [x1[C:] for x1 in tp]
            power *= 2
        tinv = each(lambda t, q: t + _dot(t.astype(BF16), bd(q)), tinv, p)

        s_old = [state_ref[i * GW:(i + 1) * GW, :] for i in range(len(chains))]
        s_bf = [s.astype(BF16) for s in s_old]
        vg = [cut(p_v, c) for c in chains]
        bd_v = [bd(x1) for x1 in vg]
        xz = each(lambda x1, s, m, bv: _dot_nt(x1, s) + _dot(m.astype(BF16), bv), a_h, s_bf, m_ak, bd_v)
        u = each(lambda t, x1: _dot(t.astype(BF16), bd(x1)), tinv, xz)
        y = each(lambda x1, s, mb, mk, uu, bv:
                 _dot_nt(x1, s) + _dot(jnp.concatenate([mb, mk], axis=1).astype(BF16),
                                       jnp.concatenate([bd(uu), bv], axis=0)),
                 r_h, s_bf, m_rb, m_rk, u, bd_v)
        new_states = []
        for c, uu, vv, s in zip(chains, u, vg, s_old):
            upd = _dot_tn(jnp.concatenate([uu, vv], axis=0).astype(BF16),
                          jnp.concatenate([cut(p_b_tail, c), cut(p_k_tail, c)], axis=0))
            decay = p_elc[c[0] * 8:c[0] * 8 + 1, c[1] * GW:(c[1] + 1) * GW]
            new_states.append(s * decay + jnp.where(blk, upd, 0.0))
        state_ref[...] = jnp.concatenate(new_states, axis=0)
        y_rows = [jnp.concatenate(y[bi * n_grp:(bi + 1) * n_grp], axis=1) for bi in range(NB)]
        y = jnp.concatenate(y_rows, axis=0)
        mean = headsums([y])[0] * (1.0 / HEAD_DIM)
        yc = y - mean
        var = headsums([yc * yc])[0] * (1.0 / HEAD_DIM)
        yn = yc * lax.rsqrt(var + RW_LNX_EPS) * lnx_g + lnx_b
        yn = yn + p_bonus * p_v
        o_ref[...] = (yn * p_g).reshape(NB, C, W).astype(o_ref.dtype)

    prepare()
    consume()


RW_SEQS_PER_STEP = 8


def _rwkv_call(rw3, mu, vecs, w2a2, g2p):
    B, S, _ = rw3.shape
    C = RW_CHUNK
    nb = RW_SEQS_PER_STEP if B % RW_SEQS_PER_STEP == 0 else 1
    n_groups = RW_WIDTH // RW_GROUP_W
    return pl.pallas_call(
        _rwkv_kernel,
        grid=(B // nb, S // C),
        in_specs=[
            pl.BlockSpec((nb, C, RW_COLS), lambda b, t: (b, t, 0)),
            pl.BlockSpec(mu.shape, lambda b, t: (0, 0)),
            pl.BlockSpec(vecs.shape, lambda b, t: (0, 0)),
            pl.BlockSpec(w2a2.shape, lambda b, t: (0, 0)),
            pl.BlockSpec(g2p.shape, lambda b, t: (0, 0)),
        ],
        out_specs=pl.BlockSpec((nb, C, RW_WIDTH), lambda b, t: (b, t, 0)),
        out_shape=jax.ShapeDtypeStruct((B, S, RW_WIDTH), BF16),
        scratch_shapes=[
            pltpu.VMEM((nb * n_groups * RW_GROUP_W, RW_GROUP_W), F32),
            pltpu.VMEM((nb, 8, RW_COLS), F32),
        ],
        compiler_params=pltpu.CompilerParams(
            dimension_semantics=("arbitrary", "arbitrary"), vmem_limit_bytes=VMEM_LIMIT),
        name="rwkv",
    )(rw3, mu, vecs, w2a2, g2p)


def _compress_kernel(kv_ref, wab_ref, pos_ref, w1_ref, w2_ref, w2t_ref, kcb_ref, vcbt_ref):
    n_half = kv_ref.shape[1]
    for j in range(2):
        src = jnp.concatenate(
            [kv_ref[0, :, l * KVC_COLS + j * NSA_KV_WIDTH:l * KVC_COLS + (j + 1) * NSA_KV_WIDTH]
             for l in range(CMP_STRIDE)], axis=1)
        pab = _dot(src, wab_ref[j])
        half = NSA_KV_HEADS * CMP_HIDDEN
        pa, pb = pab[:, :half], pab[:, half:]
        pb = pltpu.roll(pb, n_half - 1, axis=0)
        pos_term = _dot(pos_ref[j], w1_ref[j])[0:1]
        hid = pa + pb + jnp.concatenate([pos_term] * NSA_KV_HEADS, axis=1)
        act = (hid * _sigmoid(hid)).astype(BF16)
        for hk in range(NSA_KV_HEADS):
            a_h = act[:, hk * CMP_HIDDEN:(hk + 1) * CMP_HIDDEN]
            if j == 0:
                kcb_ref[0, hk] = _dot(a_h, w2_ref[j]).astype(kcb_ref.dtype)
            else:
                vcbt_ref[0, hk] = _dot_nt(w2t_ref[j], a_h).astype(vcbt_ref.dtype)


def _compress_call(kv, wab, pos8, w1, w2, w2t):
    B, n_half, width = kv.shape
    full = lambda a: pl.BlockSpec(a.shape, lambda b: (0,) * a.ndim)
    return pl.pallas_call(
        _compress_kernel,
        grid=(B,),
        in_specs=[pl.BlockSpec((1, n_half, width), lambda b: (b, 0, 0)),
                  full(wab), full(pos8), full(w1), full(w2), full(w2t)],
        out_specs=[pl.BlockSpec((1, NSA_KV_HEADS, n_half, HEAD_DIM), lambda b: (b, 0, 0, 0)),
                   pl.BlockSpec((1, NSA_KV_HEADS, HEAD_DIM, n_half), lambda b: (b, 0, 0, 0))],
        out_shape=[jax.ShapeDtypeStruct((B, NSA_KV_HEADS, n_half, HEAD_DIM), BF16),
                   jax.ShapeDtypeStruct((B, NSA_KV_HEADS, HEAD_DIM, n_half), BF16)],
        compiler_params=pltpu.CompilerParams(
            dimension_semantics=("arbitrary",), vmem_limit_bytes=VMEM_LIMIT),
        name="nsa_compress",
    )(kv, wab, pos8, w1, w2, w2t)


NSA2_TQ = 256
NSA_AUG = 128
NSA_BIAS_ROWS = 32


def _nsa2_kernel(qt_ref, gt_ref, ks_ref, vst_ref, kw_ref, vwt_ref, kcb_ref, vcbt_ref, ovt_ref,
                 tri_ref, wbias_ref, o_ref, *, seq):
    tq = NSA2_TQ
    G = NSA_GROUP
    R = G * tq
    n_half = kcb_ref.shape[2]
    n_cmp = n_half - 1
    n_sel = seq // SEL_BLOCK
    n_top = min(SEL_TOP, n_sel)
    n_wchunks = WINDOW // tq + 1
    step = pl.program_id(1)
    q0 = step * tq

    t_lane = q0 + lax.broadcasted_iota(jnp.int32, (1, R), 1) % tq
    gates = _sigmoid(gt_ref[0].astype(F32))
    pad_rows = jnp.zeros((NSA_AUG - HEAD_DIM - NSA_BIAS_ROWS, tq), BF16)

    hrow = lambda hk, g: slice((hk * G + g) * HEAD_DIM, (hk * G + g + 1) * HEAD_DIM)
    o_cmp, qaug = {}, {}
    for hk in range(NSA_KV_HEADS):
        q64 = jnp.concatenate([qt_ref[0, hrow(hk, g), :] for g in range(G)], axis=1)
        q64 = q64 * jnp.asarray(HEAD_DIM ** -0.5, BF16)

        cidx = lax.broadcasted_iota(jnp.int32, (n_half, R), 0)
        cvalid = (cidx * CMP_STRIDE + (CMP_BLOCK - 1) <= t_lane) & (cidx < n_cmp)
        s = jnp.where(cvalid, _dot(kcb_ref[0, hk], q64), NEG_INF)
        m = jnp.max(s, axis=0, keepdims=True)
        e = jnp.where(cvalid, jnp.exp(s - m), 0.0)
        l = jnp.sum(e, axis=0, keepdims=True)
        p_c = e / jnp.where(l > 0.0, l, 1.0)
        o_c = _dot(vcbt_ref[0, hk], p_c.astype(BF16))

        psum = p_c[:, 0:tq]
        for g in range(1, G):
            psum = psum + p_c[:, g * tq:(g + 1) * tq]
        hi, lo = _split2(psum)
        imp2 = _dot(ovt_ref[...], jnp.concatenate([hi, lo], axis=1))
        imp = imp2[:, :tq] + imp2[:, tq:]
        jblk = lax.broadcasted_iota(jnp.int32, (n_sel, tq), 0)
        cur = (q0 + lax.broadcasted_iota(jnp.int32, (n_sel, tq), 1)) // SEL_BLOCK
        forced = (jblk == 0) | (jblk == cur) | (jblk == cur - 1)
        score = jnp.where(forced, SEL_FORCE_SCORE, jnp.where(jblk <= cur, imp, -1.0))
        n_grp = n_sel // 8
        rows = [score[8 * q:8 * q + 8] for q in range(n_grp)]
        ranks = [jnp.zeros((8, tq), F32)] * n_grp
        sub8 = lax.broadcasted_iota(jnp.int32, (8, tq), 0)
        for j in range(n_sel):
            sj = score[j:j + 1, :]
            for q in range(n_grp):
                wins = jnp.where(sj > rows[q], 1.0, 0.0)
                wins_ties = jnp.where(sj >= rows[q], 1.0, 0.0)
                if 8 * q + 7 < j:
                    ahead = wins
                elif 8 * q > j:
                    ahead = wins_ties
                else:
                    ahead = jnp.where(sub8 > j - 8 * q, wins_ties, wins)
                ranks[q] = ranks[q] + ahead
        rank = jnp.concatenate(ranks, axis=0)
        sel_bias = jnp.where(rank < n_top, 0.0, NEG_INF).astype(BF16)
        q_l2 = (q64.astype(F32) * LOG2_E).astype(BF16)
        for g in range(G):
            o_cmp[hk, g] = o_c[:, g * tq:(g + 1) * tq]
            qaug[hk, g] = jnp.concatenate([q_l2[:, g * tq:(g + 1) * tq], sel_bias, pad_rows], axis=0)

    chains = [(hk, g) for hk in range(NSA_KV_HEADS) for g in range(G)]

    def softmax_pv(s_list, vt_of, carry=None):
        m_blk = [jnp.max(s, axis=0, keepdims=True) for s in s_list]
        if carry is None:
            m_new = m_blk
        else:
            m_new = [jnp.maximum(c[0], mb) for c, mb in zip(carry, m_blk)]
        p = [jnp.exp2(s - mn) for s, mn in zip(s_list, m_new)]
        l_blk = [jnp.sum(x, axis=0, keepdims=True) for x in p]
        pv = [_dot(vt_of(c), x.astype(BF16)) for c, x in zip(chains, p)]
        if carry is None:
            return [(mn, lb, a) for mn, lb, a in zip(m_new, l_blk, pv)]
        alpha = [jnp.exp2(c[0] - mn) for c, mn in zip(carry, m_new)]
        return [(mn, c[1] * al + lb, c[2] * al + a)
                for c, mn, al, lb, a in zip(carry, m_new, alpha, l_blk, pv)]

    def vcols(ref, hk, j):
        return ref[0, hk * HEAD_DIM:(hk + 1) * HEAD_DIM, pl.ds(pl.multiple_of(j * tq, tq), tq)]

    wchunk = [jnp.maximum(step - (n_wchunks - 1) + w, 0) for w in range(n_wchunks)]
    kw_rows = [jnp.concatenate([kw_ref[0, hk, j] for j in wchunk], axis=0) for hk in range(NSA_KV_HEADS)]
    vw_cols = [jnp.concatenate([vcols(vwt_ref, hk, j) for j in wchunk], axis=1)
               for hk in range(NSA_KV_HEADS)]
    wbias = wbias_ref[jnp.minimum(step, n_wchunks - 1)]
    s_win = [_dot(kw_rows[hk], qaug[hk, g]) + wbias for hk, g in chains]
    win = softmax_pv(s_win, lambda c: vw_cols[c[0]])

    tri = tri_ref[...]
    s_diag = [_dot(ks_ref[0, hk, step], qaug[hk, g]) + tri for hk, g in chains]
    carry = softmax_pv(s_diag, lambda c: vcols(vst_ref, c[0], step))

    def body(j, flat):
        carry = [tuple(flat[3 * i:3 * i + 3]) for i in range(len(chains))]
        s_j = [_dot(ks_ref[0, hk, j], qaug[hk, g]) for hk, g in chains]
        new = softmax_pv(s_j, lambda c: vcols(vst_ref, c[0], j), carry)
        return tuple(x for c in new for x in c)

    flat = lax.fori_loop(0, step, body, tuple(x for c in carry for x in c))
    sel = [tuple(flat[3 * i:3 * i + 3]) for i in range(len(chains))]

    for i, (hk, g) in enumerate(chains):
        gate = lambda j: gates[(hk * G + g) * 3 + j:(hk * G + g) * 3 + j + 1, :]
        out = (gate(0) * o_cmp[hk, g] + gate(1) * (sel[i][2] / sel[i][1])
               + gate(2) * (win[i][2] / win[i][1]))
        o_ref[0, hrow(hk, g), :] = out.astype(o_ref.dtype)


def _nsa2_call(qt, gt, ks5, vst, kw5, vwt, kcb, vcbt, ovt, tri, wbias, seq):
    B = qt.shape[0]
    tq = NSA2_TQ
    per_b = lambda a: pl.BlockSpec((1,) + a.shape[1:], lambda b, i: (b,) + (0,) * (a.ndim - 1))
    full = lambda a: pl.BlockSpec(a.shape, lambda b, i: (0,) * a.ndim)
    return pl.pallas_call(
        functools.partial(_nsa2_kernel, seq=seq),
        grid=(B, seq // tq),
        in_specs=[pl.BlockSpec((1, NSA_WIDTH, tq), lambda b, i: (b, 0, i)),
                  pl.BlockSpec((1, gt.shape[1], tq), lambda b, i: (b, 0, i)),
                  per_b(ks5), per_b(vst), per_b(kw5), per_b(vwt), per_b(kcb), per_b(vcbt),
                  full(ovt), full(tri), full(wbias)],
        out_specs=pl.BlockSpec((1, NSA_WIDTH, tq), lambda b, i: (b, 0, i)),
        out_shape=jax.ShapeDtypeStruct((B, NSA_WIDTH, seq), BF16),
        compiler_params=pltpu.CompilerParams(
            dimension_semantics=("arbitrary", "arbitrary"), vmem_limit_bytes=VMEM_LIMIT),
        name="nsa_attention",
    )(qt, gt, ks5, vst, kw5, vwt, kcb, vcbt, ovt, tri, wbias)


def _nsa_from_proj(kvc, ksa, kwa, qt, vst, vwt, gt, P):
    B, Hk, S, _ = ksa.shape
    tq = NSA2_TQ
    Dh = HEAD_DIM
    n_sel = S // SEL_BLOCK
    assert S % tq == 0 and WINDOW % tq == 0 and n_sel == NSA_BIAS_ROWS
    n_half = S // CMP_STRIDE
    kv = kvc.reshape(B, n_half, CMP_STRIDE * KVC_COLS)
    w1 = P['nsa_cmp_w1'][0]
    w1h = w1.astype(BF16).reshape(2, 2, CMP_STRIDE, 1, HEAD_DIM, CMP_HIDDEN)
    zero = jnp.zeros_like(w1h[0, 0])

    def block(j, a, h):
        parts = [w1h[j, a] if hh == h else zero for hh in range(Hk)]
        return jnp.concatenate(parts, axis=1).reshape(CMP_STRIDE * NSA_KV_WIDTH, CMP_HIDDEN)

    wab = jnp.stack([jnp.concatenate([block(j, a, h) for a in range(2) for h in range(Hk)], axis=1)
                     for j in range(2)])
    pos8 = jnp.broadcast_to(P['nsa_cmp_pos'][0].reshape(2, 1, CMP_BLOCK * HEAD_DIM),
                            (2, 8, CMP_BLOCK * HEAD_DIM)).astype(BF16)
    w2 = P['nsa_cmp_w2'][0].astype(BF16)
    kcb, vcbt = _compress_call(kv, wab, pos8, w1.astype(BF16), w2, jnp.swapaxes(w2, 1, 2))

    ks5 = ksa.reshape(B, Hk, S // tq, tq, NSA_AUG)
    kw5 = kwa.reshape(B, Hk, S // tq, tq, NSA_AUG)

    n_cmp = (S - CMP_BLOCK) // CMP_STRIDE + 1
    cmp_start = np.arange(n_half) * CMP_STRIDE
    sel_start = np.arange(n_sel) * SEL_BLOCK
    overlap = ((cmp_start[:, None] <= sel_start[None, :] + SEL_BLOCK - 1)
               & (cmp_start[:, None] + CMP_BLOCK - 1 >= sel_start[None, :])
               & (np.arange(n_half)[:, None] < n_cmp)).astype(np.float32)
    tri = np.where(np.arange(tq)[:, None] <= np.arange(tq)[None, :], 0.0, NEG_INF).astype(np.float32)
    n_w = WINDOW // tq
    masked = np.full((tq, tq), NEG_INF, np.float32)
    clear = np.zeros((tq, tq), np.float32)
    wbias = np.stack([np.concatenate([masked] * (n_w - v) + [clear] * v + [tri], axis=0) for v in range(n_w)]
                     + [np.concatenate([NEG_INF - tri] + [clear] * (n_w - 1) + [tri], axis=0)])
    return _nsa2_call(qt, gt, ks5, vst, kw5, vwt, kcb, vcbt, jnp.asarray(overlap.T, BF16),
                      jnp.asarray(tri), jnp.asarray(wbias), S)


def _merge_kernel(x_ref, ya_ref, ybt_ref, gate_ref, wa_ref, wb_ref, wo_ref, o_ref):
    D = x_ref.shape[-1]
    ta = _dot(ya_ref[...], wa_ref[...])
    tb = _dot_tn(ybt_ref[0], wb_ref[...])
    ga = _sigmoid(gate_ref[:, :D].astype(F32))
    gb = _sigmoid(gate_ref[:, D:].astype(F32))
    mix = (ga * ta + gb * tb).astype(BF16)
    o_ref[...] = x_ref[...] + _dot(mix, wo_ref[...])


def _merge_call(x2, ya2, ybt, gates, wa, wb, wo, tm=1024):
    T, D = x2.shape
    tiles_per_seq = ybt.shape[2] // tm
    row = lambda w: pl.BlockSpec((tm, w), lambda i: (i, 0))
    full = lambda a: pl.BlockSpec(a.shape, lambda i: (0,) * a.ndim)
    ybt_spec = pl.BlockSpec((1, ybt.shape[1], tm), lambda i: (i // tiles_per_seq, 0, i % tiles_per_seq))
    return pl.pallas_call(
        _merge_kernel,
        grid=(T // tm,),
        in_specs=[row(D), row(ya2.shape[1]), ybt_spec, row(gates.shape[1]),
                  full(wa), full(wb), full(wo)],
        out_specs=row(D),
        out_shape=jax.ShapeDtypeStruct((T, D), F32),
        compiler_params=pltpu.CompilerParams(
            dimension_semantics=("arbitrary",), vmem_limit_bytes=VMEM_LIMIT),
        name="merge",
    )(x2, ya2, ybt, gates, wa, wb, wo)


FFN_HALO = 8


def _rms(x, g):
    return x * lax.rsqrt(jnp.mean(x * x, axis=-1, keepdims=True) + NORM_EPS) * g


def _ffn_kernel(h_ref, halo_ref, p_ref, ln_ref, wup_ref, cw_ref, cb_ref, wdn_ref, wpg_ref, wpp_ref,
                o_ref, up0a_ref, up0b_ref, up1a_ref, up1b_ref, act_ref, *, tiles_per_seq, fc):
    up_refs = ((up0a_ref, up0b_ref), (up1a_ref, up1b_ref))
    tm = act_ref.shape[0]
    D = halo_ref.shape[1]
    V = tm // 8
    d_ff = wdn_ref.shape[0]
    slabs = lambda ref, w: jnp.swapaxes(ref[0], 0, 1).reshape(tm, w)
    h = slabs(h_ref, D)
    first = (pl.program_id(0) % tiles_per_seq) == 0
    halo = jnp.where(first, 0.0, halo_ref[...])
    ln2, ln3, lnf = ln_ref[0:1, :], ln_ref[1:2, :], ln_ref[2:3, :]
    u = jnp.concatenate([_rms(halo, ln2), _rms(h, ln2)], axis=0).astype(BF16)

    n_chunks = d_ff // fc
    sub = lax.broadcasted_iota(jnp.int32, (8, 1), 0)

    def project(c):
        for half in range(2):
            col = half * d_ff + c * fc
            up_refs[c % 2][half][...] = _dot(u, wup_ref[:, col:col + fc])

    def conv(c, half):
        ref = up_refs[c % 2][half]
        col = half * d_ff + c * fc
        halo_up = ref[0:FFN_HALO, :]
        last = lambda k: ref[FFN_HALO + tm - 8 * k:FFN_HALO + tm - 8 * (k - 1), :]
        wrap1 = pltpu.roll(jnp.where(sub == 7, halo_up, last(1)), 1, axis=0)
        wrap2 = pltpu.roll(jnp.where(sub == 7, pltpu.roll(halo_up, 1, axis=0), last(2)), 1, axis=0)
        x0 = ref[FFN_HALO:FFN_HALO + tm, :]
        x1 = jnp.concatenate([wrap1, ref[FFN_HALO:FFN_HALO + tm - 8, :]], axis=0)
        x2 = jnp.concatenate([wrap2, wrap1, ref[FFN_HALO:FFN_HALO + tm - 16, :]], axis=0)
        tap = lambda j: cw_ref[j:j + 1, col:col + fc]
        return cb_ref[:, col:col + fc] + tap(0) * x2 + tap(1) * x1 + tap(2) * x0

    project(0)
    for c in range(n_chunks):
        if c + 1 < n_chunks:
            project(c + 1)
        a = conv(c, 0)
        b = conv(c, 1)
        act_ref[:, c * fc:(c + 1) * fc] = (a * _sigmoid(a) * b).astype(BF16)
    h2 = h + _dot(act_ref[...], wdn_ref[...])
    gate = _sigmoid(_dot(_rms(h2, ln3).astype(BF16), wpg_ref[...]))
    h3 = h2 + gate * _dot(slabs(p_ref, p_ref.shape[3]).astype(BF16), wpp_ref[...])
    o_ref[0] = jnp.swapaxes(_rms(h3, lnf).reshape(V, 8, D), 0, 1)


FFN_TM = 512
FFN_FC = 256


def _ffn_call(h2d, p2d, lns, wup, cw, cb, wdn, wpg, wpp, seq):
    T, D = h2d.shape
    tm, fc = FFN_TM, FFN_FC
    assert CONV_WIDTH == 3 and seq % tm == 0 and wdn.shape[0] % fc == 0
    tiles_per_seq = seq // tm
    runs = lambda x: x.reshape(T // tm, 8, tm // 8, x.shape[1])
    run_spec = lambda w: pl.BlockSpec((1, 8, tm // 8, w), lambda i: (i, 0, 0, 0))
    full = lambda a: pl.BlockSpec(a.shape, lambda i: (0,) * a.ndim, pipeline_mode=pl.Buffered(1))
    halo = pl.BlockSpec((FFN_HALO, D), lambda i: (jnp.maximum(i * (tm // FFN_HALO) - 1, 0), 0))
    out = pl.pallas_call(
        functools.partial(_ffn_kernel, tiles_per_seq=tiles_per_seq, fc=fc),
        grid=(T // tm,),
        in_specs=[run_spec(D), halo, run_spec(p2d.shape[1]), full(lns), full(wup), full(cw), full(cb),
                  full(wdn), full(wpg), full(wpp)],
        out_specs=run_spec(D),
        out_shape=jax.ShapeDtypeStruct((T // tm, 8, tm // 8, D), F32),
        scratch_shapes=[pltpu.VMEM((FFN_HALO + tm, fc), F32)] * 4 + [pltpu.VMEM((tm, wdn.shape[0]), BF16)],
        compiler_params=pltpu.CompilerParams(
            dimension_semantics=("arbitrary",), vmem_limit_bytes=VMEM_LIMIT),
        name="ffn",
    )(runs(h2d), h2d, runs(p2d), lns, wup, cw, cb, wdn, wpg, wpp)
    return out.reshape(T, D)


def _prep_proj_weights(w_in, mu_wag, w1, a1, g1):
    D = w_in.shape[0]
    sizes = (RW_WIDTH, RW_WIDTH, RW_WIDTH, NSA_WIDTH) + (NSA_KV_WIDTH,) * 6 + (3 * NSA_Q_HEADS, D, D)
    offs = np.concatenate([[0], np.cumsum(sizes)])
    part = lambda i, j: w_in[:, offs[i]:offs[j]]
    mw, ma, mg = mu_wag[0][:, None], mu_wag[1][:, None], mu_wag[2][:, None]
    zg = jnp.zeros((D, RW_GATE_PAD - RW_GATE_LORA), F32)
    rw = jnp.concatenate([
        part(0, 3),
        (1.0 - mw) * w1, (1.0 - ma) * a1,
        mw * w1, ma * a1,
        (1.0 - mg) * g1, zg,
        mg * g1, zg], axis=1)
    def widen(w):
        w = w.reshape(D, NSA_KV_HEADS, HEAD_DIM)
        return jnp.concatenate([w, jnp.zeros_like(w)], axis=2).reshape(D, KEY_COLS)

    w_rows = jnp.concatenate([rw, part(4, 6), widen(part(6, 7)), widen(part(8, 9)), part(11, 13)], axis=1)
    w_cols = jnp.concatenate([part(3, 4), part(7, 8), part(9, 10), part(10, 11),
                              jnp.zeros((D, NSA_GATE_ROWS - 3 * NSA_Q_HEADS), F32)], axis=1)
    return w_rows.astype(BF16), w_cols.T.astype(BF16)


def _prep_rwkv_weights(w2, a2, g2):
    z = jnp.zeros_like(w2)
    w2a2 = jnp.concatenate([jnp.concatenate([w2, z], axis=1),
                            jnp.concatenate([z, a2], axis=1)], axis=0).astype(BF16)
    g2p = jnp.concatenate([g2, jnp.zeros((RW_GATE_PAD - RW_GATE_LORA, RW_WIDTH), F32)],
                          axis=0).astype(BF16)
    return w2a2, g2p


def _rwkv_from_proj(rw3, P):
    w2a2, g2p = _prep_rwkv_weights(P['rw_w2'][0], P['rw_a2'][0], P['rw_g2'][0])
    vecs = jnp.stack([P['rw_w0'][0], P['rw_a0'][0], P['rw_k_k'][0], P['rw_k_a'][0],
                      P['rw_r_k'][0].reshape(-1), P['rw_lnx_g'][0], P['rw_lnx_b'][0],
                      jnp.zeros((RW_WIDTH,), F32)], axis=0)
    return _rwkv_call(rw3, P['rw_mu_rkv'][0], vecs, w2a2, g2p)


def kernel(x, p, ln1_g, w_in, rw_mu_rkv, rw_mu_wag, rw_w0, rw_w1, rw_w2, rw_a0, rw_a1, rw_a2, rw_g1, rw_g2, rw_k_k, rw_k_a, rw_r_k, rw_lnx_g, rw_lnx_b, nsa_cmp_pos, nsa_cmp_w1, nsa_cmp_w2, w_out_a, w_out_b, w_out, ln2_g, w_up, conv_w, conv_b, w_down, ln3_g, w_ple_gate, w_ple_proj, ln_f_g):
    B, S, D = x.shape
    T = B * S
    assert w_in.shape[0] == 1, "single-layer block"
    P = dict(rw_mu_rkv=rw_mu_rkv, rw_w0=rw_w0, rw_w2=rw_w2, rw_a0=rw_a0, rw_a2=rw_a2, rw_g2=rw_g2,
             rw_k_k=rw_k_k, rw_k_a=rw_k_a, rw_r_k=rw_r_k, rw_lnx_g=rw_lnx_g, rw_lnx_b=rw_lnx_b,
             nsa_cmp_pos=nsa_cmp_pos, nsa_cmp_w1=nsa_cmp_w1, nsa_cmp_w2=nsa_cmp_w2)
    h = x.reshape(T, D)
    w_rows, w_cols = _prep_proj_weights(w_in[0], rw_mu_wag[0], rw_w1[0], rw_a1[0], rw_g1[0])
    rw, kvc, ksa, kwa, gates, qt, vst, vwt, gt = _proj_call(h, ln1_g[0][None], w_rows, w_cols, S)
    ya = _rwkv_from_proj(rw.reshape(B, S, RW_COLS), P)
    ybt = _nsa_from_proj(kvc, ksa, kwa, qt, vst, vwt, gt, P)
    h1 = _merge_call(h, ya.reshape(T, RW_WIDTH), ybt, gates,
                     w_out_a[0].astype(BF16), w_out_b[0].astype(BF16), w_out[0].astype(BF16))
    lns = jnp.stack([ln2_g[0], ln3_g[0], ln_f_g], axis=0)
    out = _ffn_call(h1, p[0].reshape(T, -1), lns, w_up[0].astype(BF16), conv_w[0], conv_b[0][None],
                    w_down[0].astype(BF16), w_ple_gate[0].astype(BF16), w_ple_proj[0].astype(BF16), S)
    return out.reshape(B, S, D)
```

```python
import functools

import numpy as np
import jax
import jax.numpy as jnp
from jax import lax
from jax.experimental import pallas as pl
from jax.experimental.pallas import tpu as pltpu

F32 = jnp.float32
BF16 = jnp.bfloat16

HEAD_DIM = 64
NORM_EPS = 1e-6
NEG_INF = -1e30

RW_HEADS = 8
RW_WIDTH = RW_HEADS * HEAD_DIM
RW_DECAY_LORA = 64
RW_AAA_LORA = 64
RW_GATE_LORA = 160
RW_LNX_EPS = 64e-5
RW_CHUNK = 64
RW_GROUP = 4
RW_GROUP_W = RW_GROUP * HEAD_DIM
RW_GATE_PAD = 256

NSA_Q_HEADS = 8
NSA_KV_HEADS = 2
NSA_GROUP = NSA_Q_HEADS // NSA_KV_HEADS
NSA_WIDTH = NSA_Q_HEADS * HEAD_DIM
NSA_KV_WIDTH = NSA_KV_HEADS * HEAD_DIM
CMP_BLOCK = 32
CMP_STRIDE = 16
CMP_HIDDEN = 128
SEL_BLOCK = 64
SEL_TOP = 16
SEL_FORCE_SCORE = 1e4
WINDOW = 512

CONV_WIDTH = 3

RW_COLS = 3 * RW_WIDTH + 2 * 128 + 2 * RW_GATE_PAD

VMEM_LIMIT = 56 * 1024 * 1024


def _dot(a, b):
    return jnp.dot(a, b, preferred_element_type=F32)


def _dot_nt(a, b):
    return lax.dot_general(a, b, (((1,), (1,)), ((), ())), preferred_element_type=F32)


def _dot_tn(a, b):
    return lax.dot_general(a, b, (((0,), (0,)), ((), ())), preferred_element_type=F32)


def _split2(x):
    hi = x.astype(BF16)
    lo = (x - hi.astype(F32)).astype(BF16)
    return hi, lo


def _split3(x):
    hi = x.astype(BF16)
    r1 = x - hi.astype(F32)
    mid = r1.astype(BF16)
    lo = (r1 - mid.astype(F32)).astype(BF16)
    return hi, mid, lo


def _sigmoid(x):
    return 1.0 / (1.0 + jnp.exp(-x))


LOG2_E = 1.4426950408889634
DECAY_SCALE_LOG2 = 0.6065306597126334 * LOG2_E


PROJ_TM = 512
PROJ_CHUNK = 768
KVC_COLS = 2 * NSA_KV_WIDTH
KEY_COLS = NSA_KV_HEADS * 128
NSA_GATE_ROWS = 32
T_ROWS = NSA_WIDTH + 2 * NSA_KV_WIDTH + NSA_GATE_ROWS


def _proj_kernel(x_ref, g_ref, w_ref, wt_ref, rw_ref, kvc_ref, ksa_ref, kwa_ref,
                 qt_ref, vst_ref, vwt_ref, gt_ref, *, tiles_per_seq):
    tm = x_ref.shape[0]
    x = x_ref[...]
    ms = jnp.mean(x * x, axis=-1, keepdims=True)
    u = (x * lax.rsqrt(ms + NORM_EPS) * g_ref[...]).astype(BF16)

    col = 0
    for o_ref in (rw_ref, kvc_ref):
        width = o_ref.shape[-1]
        for c in range(0, width, PROJ_CHUNK):
            hi = min(c + PROJ_CHUNK, width)
            o_ref[:, c:hi] = _dot(u, w_ref[:, col + c:col + hi]).astype(o_ref.dtype)
        col += width

    s0 = (pl.program_id(0) % tiles_per_seq) * tm
    blk = (s0 + lax.broadcasted_iota(jnp.int32, (tm, KEY_COLS), 0)) // SEL_BLOCK
    lane = lax.broadcasted_iota(jnp.int32, (tm, KEY_COLS), 1) % 128
    onehot = jnp.where(lane - HEAD_DIM == blk, 1.0, 0.0)
    ks = _dot(u, w_ref[:, col:col + KEY_COLS]) + onehot
    kw = _dot(u, w_ref[:, col + KEY_COLS:col + 2 * KEY_COLS])
    for hk in range(NSA_KV_HEADS):
        ksa_ref[0, hk] = ks[:, hk * 128:(hk + 1) * 128].astype(ksa_ref.dtype)
        kwa_ref[0, hk] = kw[:, hk * 128:(hk + 1) * 128].astype(kwa_ref.dtype)

    t = _dot_nt(wt_ref[...], u)
    row = 0
    for o_ref in (qt_ref, vst_ref, vwt_ref, gt_ref):
        n = o_ref.shape[1]
        o_ref[0] = t[row:row + n].astype(o_ref.dtype)
        row += n


def _proj_call(x2, g, w_all, wt_all, seq):
    T, D = x2.shape
    tm = PROJ_TM
    B = T // seq
    tps = seq // tm
    rows = lambda w: pl.BlockSpec((tm, w), lambda i: (i, 0))
    full = lambda a: pl.BlockSpec(a.shape, lambda i: (0,) * a.ndim)
    keys = pl.BlockSpec((1, NSA_KV_HEADS, tm, 128), lambda i: (i // tps, 0, i % tps, 0))
    tcols = lambda n: pl.BlockSpec((1, n, tm), lambda i: (i // tps, 0, i % tps))
    sds = jax.ShapeDtypeStruct
    return pl.pallas_call(
        functools.partial(_proj_kernel, tiles_per_seq=tps),
        grid=(T // tm,),
        in_specs=[rows(D), full(g), full(w_all), full(wt_all)],
        out_specs=[rows(RW_COLS), rows(KVC_COLS), keys, keys,
                   tcols(NSA_WIDTH), tcols(NSA_KV_WIDTH), tcols(NSA_KV_WIDTH), tcols(NSA_GATE_ROWS)],
        out_shape=[sds((T, RW_COLS), BF16), sds((T, KVC_COLS), BF16),
                   sds((B, NSA_KV_HEADS, seq, 128), BF16), sds((B, NSA_KV_HEADS, seq, 128), BF16),
                   sds((B, NSA_WIDTH, seq), BF16), sds((B, NSA_KV_WIDTH, seq), BF16),
                   sds((B, NSA_KV_WIDTH, seq), BF16), sds((B, NSA_GATE_ROWS, seq), BF16)],
        compiler_params=pltpu.CompilerParams(
            dimension_semantics=("arbitrary",), vmem_limit_bytes=VMEM_LIMIT),
        name="proj",
    )(x2, g, w_all, wt_all)


def _rwkv_kernel(x_ref, mu_ref, vec_ref, w2a2_ref, g2_ref, o_ref, state_ref, prev_ref):
    C = RW_CHUNK
    GW = RW_GROUP_W
    W = RW_WIDTH
    NB = x_ref.shape[0]
    R = NB * C
    t_idx = pl.program_id(1)
    ops = {}

    @pl.when(t_idx == 0)
    def _():
        state_ref[...] = jnp.zeros_like(state_ref)
        prev_ref[...] = jnp.zeros_like(prev_ref)

    mu = mu_ref[...]
    w0, a0, k_k, k_a, r_k, lnx_g, lnx_b = (vec_ref[i:i + 1, :] for i in range(7))

    gr = lax.broadcasted_iota(jnp.int32, (GW, GW), 0) // HEAD_DIM
    gc = lax.broadcasted_iota(jnp.int32, (GW, GW), 1) // HEAD_DIM
    blk = gr == gc
    ones_bd = jnp.where(blk, 1.0, 0.0).astype(BF16)

    def headsums(zs):
        parts = []
        for z in zs:
            zb = z.astype(BF16)
            parts += [zb[:, :GW], zb[:, GW:]]
        s = _dot(jnp.concatenate(parts, axis=0), ones_bd)
        return [jnp.concatenate([s[2 * R * i:2 * R * i + R], s[2 * R * i + R:2 * R * (i + 1)]], axis=1)
                for i in range(len(zs))]

    t_n = lax.broadcasted_iota(jnp.int32, (C, GW), 0)
    s_n = lax.broadcasted_iota(jnp.int32, (C, GW), 1) % HEAD_DIM
    strict = t_n > s_n
    incl = t_n >= s_n
    eye_n = jnp.where(t_n == s_n, 1.0, 0.0)

    def bd(z):
        z4 = jnp.concatenate([z.astype(F32)] * RW_GROUP, axis=0)
        return jnp.where(blk, z4, 0.0).astype(BF16)

    def prepare():
        x = x_ref[...].reshape(R, RW_COLS).astype(F32)
        rolled = pltpu.roll(x, 1, axis=0)
        row8 = lax.broadcasted_iota(jnp.int32, (8, 1), 0)
        pieces = []
        for bi in range(NB):
            pieces.append(jnp.where(row8 == 0, prev_ref[bi, 0:1, :], rolled[bi * C:bi * C + 8]))
            pieces.append(rolled[bi * C + 8:(bi + 1) * C])
            prev_ref[bi, 0:1, :] = x[(bi + 1) * C - 1:(bi + 1) * C, :]
        xs = jnp.concatenate(pieces, axis=0)

        def lerp(j):
            cur = x[:, j * W:(j + 1) * W]
            return cur + (xs[:, j * W:(j + 1) * W] - cur) * mu[j:j + 1, :]

        r, k, v = lerp(0), lerp(1), lerp(2)
        o = 3 * W
        pre_a = x[:, o:o + 128] + xs[:, o + 128:o + 256]
        lane = lax.broadcasted_iota(jnp.int32, (R, 128), 1)
        h_a = jnp.where(lane < RW_DECAY_LORA, jnp.tanh(pre_a), pre_a)
        lwa = _dot(h_a.astype(BF16), w2a2_ref[...])
        o += 256
        pre_g = x[:, o:o + RW_GATE_PAD] + xs[:, o + RW_GATE_PAD:o + 2 * RW_GATE_PAD]
        g = _dot(_sigmoid(pre_g).astype(BF16), g2_ref[...])
        ld = (-DECAY_SCALE_LOG2) * _sigmoid(w0 + lwa[:, :W])
        a = _sigmoid(a0 + lwa[:, W:])
        kkr = k * k_k
        k2 = k * (1.0 + (a - 1.0) * k_a)
        kk_ss, bonus = headsums([kkr * kkr, r * k2 * r_k])
        kk = kkr / jnp.maximum(jnp.sqrt(kk_ss), 1e-12)
        b = kk * a
        tr = lax.broadcasted_iota(jnp.int32, (R, R), 0)
        tc = lax.broadcasted_iota(jnp.int32, (R, R), 1)
        tri = jnp.where((tr >= tc) & (tr // C == tc // C), 1.0, 0.0).astype(BF16)
        l_inc = _dot(tri, jnp.concatenate(_split2(ld), axis=1))
        l_inc = l_inc[:, :W] + l_inc[:, W:]
        l_end = [l_inc[(bi + 1) * C - 1:(bi + 1) * C] for bi in range(NB)]
        e_neg = jnp.exp2(-l_inc)
        e_tail = jnp.concatenate([jnp.exp2(l_end[bi] - l_inc[bi * C:(bi + 1) * C]) for bi in range(NB)],
                                 axis=0)
        new_opb = (-kk * jnp.exp2(l_inc - ld), r * jnp.exp2(l_inc), b * e_neg, k2 * e_neg,
                   b * e_tail, k2 * e_tail)
        cast = (True, True, False, False, True, True)
        ops['matmul'] = tuple(z.astype(BF16) if c else z for z, c in zip(new_opb, cast))
        ops['output'] = (v, bonus, g)
        ops['decay'] = jnp.concatenate([jnp.broadcast_to(jnp.exp2(le), (8, W)) for le in l_end], axis=0)

    def consume():
        n_grp = W // GW
        chains = [(bi, gi) for bi in range(NB) for gi in range(n_grp)]
        cut = lambda z, c: z[c[0] * C:(c[0] + 1) * C, c[1] * GW:(c[1] + 1) * GW]
        each = lambda f, *lists: [f(*args) for args in zip(*lists)]

        p_a_hat, p_r_hat, p_b_hat, p_k_hat, p_b_tail, p_k_tail = ops['matmul']
        p_v, p_bonus, p_g = ops['output']
        p_elc = ops['decay']
        a_h = [cut(p_a_hat, c) for c in chains]
        r_h = [cut(p_r_hat, c) for c in chains]
        ar = each(lambda x1, x2: jnp.concatenate([x1, x2], axis=0), a_h, r_h)
        m1 = each(_dot_nt, ar, [bd(cut(p_b_hat, c)) for c in chains])
        m2 = each(_dot_nt, ar, [bd(cut(p_k_hat, c)) for c in chains])
        m_ab = [jnp.where(strict, m[:C], 0.0) for m in m1]
        m_rb = [jnp.where(incl, m[C:], 0.0) for m in m1]
        m_ak = [jnp.where(strict, m[:C], 0.0) for m in m2]
        m_rk = [jnp.where(incl, m[C:], 0.0) for m in m2]

        tinv = [eye_n + m for m in m_ab]
        p = each(lambda m: _dot(m.astype(BF16), bd(m)), m_ab)
        power = 2
        while 2 * power < C:
            tp = each(lambda t, q: _dot(jnp.concatenate([t, q], axis=0).astype(BF16), bd(q)), tinv, p)
            tinv = each(lambda t, x1: t + x1[:C], tinv, tp)
            p = [x1[C:] for x1 in tp]
            power *= 2
        tinv = each(lambda t, q: t + _dot(t.astype(BF16), bd(q)), tinv, p)

        s_old = [state_ref[i * GW:(i + 1) * GW, :] for i in range(len(chains))]
        s_bf = [s.astype(BF16) for s in s_old]
        vg = [cut(p_v, c) for c in chains]
        bd_v = [bd(x1) for x1 in vg]
        xz = each(lambda x1, s, m, bv: _dot_nt(x1, s) + _dot(m.astype(BF16), bv), a_h, s_bf, m_ak, bd_v)
        u = each(lambda t, x1: _dot(t.astype(BF16), bd(x1)), tinv, xz)
        y = each(lambda x1, s, mb, mk, uu, bv:
                 _dot_nt(x1, s) + _dot(jnp.concatenate([mb, mk], axis=1).astype(BF16),
                                       jnp.concatenate([bd(uu), bv], axis=0)),
                 r_h, s_bf, m_rb, m_rk, u, bd_v)
        new_states = []
        for c, uu, vv, s in zip(chains, u, vg, s_old):
            upd = _dot_tn(jnp.concatenate([uu, vv], axis=0).astype(BF16),
                          jnp.concatenate([cut(p_b_tail, c), cut(p_k_tail, c)], axis=0))
            decay = p_elc[c[0] * 8:c[0] * 8 + 1, c[1] * GW:(c[1] + 1) * GW]
            new_states.append(s * decay + jnp.where(blk, upd, 0.0))
        state_ref[...] = jnp.concatenate(new_states, axis=0)
        y_rows = [jnp.concatenate(y[bi * n_grp:(bi + 1) * n_grp], axis=1) for bi in range(NB)]
        y = jnp.concatenate(y_rows, axis=0)
        mean = headsums([y])[0] * (1.0 / HEAD_DIM)
        yc = y - mean
        var = headsums([yc * yc])[0] * (1.0 / HEAD_DIM)
        yn = yc * lax.rsqrt(var + RW_LNX_EPS) * lnx_g + lnx_b
        yn = yn + p_bonus * p_v
        o_ref[...] = (yn * p_g).reshape(NB, C, W).astype(o_ref.dtype)

    prepare()
    consume()


RW_SEQS_PER_STEP = 8


def _rwkv_call(rw3, mu, vecs, w2a2, g2p):
    B, S, _ = rw3.shape
    C = RW_CHUNK
    nb = RW_SEQS_PER_STEP if B % RW_SEQS_PER_STEP == 0 else 1
    n_groups = RW_WIDTH // RW_GROUP_W
    return pl.pallas_call(
        _rwkv_kernel,
        grid=(B // nb, S // C),
        in_specs=[
            pl.BlockSpec((nb, C, RW_COLS), lambda b, t: (b, t, 0)),
            pl.BlockSpec(mu.shape, lambda b, t: (0, 0)),
            pl.BlockSpec(vecs.shape, lambda b, t: (0, 0)),
            pl.BlockSpec(w2a2.shape, lambda b, t: (0, 0)),
            pl.BlockSpec(g2p.shape, lambda b, t: (0, 0)),
        ],
        out_specs=pl.BlockSpec((nb, C, RW_WIDTH), lambda b, t: (b, t, 0)),
        out_shape=jax.ShapeDtypeStruct((B, S, RW_WIDTH), BF16),
        scratch_shapes=[
            pltpu.VMEM((nb * n_groups * RW_GROUP_W, RW_GROUP_W), F32),
            pltpu.VMEM((nb, 8, RW_COLS), F32),
        ],
        compiler_params=pltpu.CompilerParams(
            dimension_semantics=("arbitrary", "arbitrary"), vmem_limit_bytes=VMEM_LIMIT),
        name="rwkv",
    )(rw3, mu, vecs, w2a2, g2p)


def _compress_kernel(kv_ref, wab_ref, pos_ref, w1_ref, w2_ref, w2t_ref, kcb_ref, vcbt_ref):
    n_half = kv_ref.shape[1]
    for j in range(2):
        src = jnp.concatenate(
            [kv_ref[0, :, l * KVC_COLS + j * NSA_KV_WIDTH:l * KVC_COLS + (j + 1) * NSA_KV_WIDTH]
             for l in range(CMP_STRIDE)], axis=1)
        pab = _dot(src, wab_ref[j])
        half = NSA_KV_HEADS * CMP_HIDDEN
        pa, pb = pab[:, :half], pab[:, half:]
        pb = pltpu.roll(pb, n_half - 1, axis=0)
        pos_term = _dot(pos_ref[j], w1_ref[j])[0:1]
        hid = pa + pb + jnp.concatenate([pos_term] * NSA_KV_HEADS, axis=1)
        act = (hid * _sigmoid(hid)).astype(BF16)
        for hk in range(NSA_KV_HEADS):
            a_h = act[:, hk * CMP_HIDDEN:(hk + 1) * CMP_HIDDEN]
            if j == 0:
                kcb_ref[0, hk] = _dot(a_h, w2_ref[j]).astype(kcb_ref.dtype)
            else:
                vcbt_ref[0, hk] = _dot_nt(w2t_ref[j], a_h).astype(vcbt_ref.dtype)


def _compress_call(kv, wab, pos8, w1, w2, w2t):
    B, n_half, width = kv.shape
    full = lambda a: pl.BlockSpec(a.shape, lambda b: (0,) * a.ndim)
    return pl.pallas_call(
        _compress_kernel,
        grid=(B,),
        in_specs=[pl.BlockSpec((1, n_half, width), lambda b: (b, 0, 0)),
                  full(wab), full(pos8), full(w1), full(w2), full(w2t)],
        out_specs=[pl.BlockSpec((1, NSA_KV_HEADS, n_half, HEAD_DIM), lambda b: (b, 0, 0, 0)),
                   pl.BlockSpec((1, NSA_KV_HEADS, HEAD_DIM, n_half), lambda b: (b, 0, 0, 0))],
        out_shape=[jax.ShapeDtypeStruct((B, NSA_KV_HEADS, n_half, HEAD_DIM), BF16),
                   jax.ShapeDtypeStruct((B, NSA_KV_HEADS, HEAD_DIM, n_half), BF16)],
        compiler_params=pltpu.CompilerParams(
            dimension_semantics=("arbitrary",), vmem_limit_bytes=VMEM_LIMIT),
        name="nsa_compress",
    )(kv, wab, pos8, w1, w2, w2t)


NSA2_TQ = 256
NSA_AUG = 128
NSA_BIAS_ROWS = 32


def _nsa2_kernel(qt_ref, gt_ref, ks_ref, vst_ref, kw_ref, vwt_ref, kcb_ref, vcbt_ref, ovt_ref,
                 tri_ref, wbias_ref, o_ref, *, seq):
    tq = NSA2_TQ
    G = NSA_GROUP
    R = G * tq
    n_half = kcb_ref.shape[2]
    n_cmp = n_half - 1
    n_sel = seq // SEL_BLOCK
    n_top = min(SEL_TOP, n_sel)
    n_wchunks = WINDOW // tq + 1
    step = pl.program_id(1)
    q0 = step * tq

    t_lane = q0 + lax.broadcasted_iota(jnp.int32, (1, R), 1) % tq
    gates = _sigmoid(gt_ref[0].astype(F32))
    pad_rows = jnp.zeros((NSA_AUG - HEAD_DIM - NSA_BIAS_ROWS, tq), BF16)

    hrow = lambda hk, g: slice((hk * G + g) * HEAD_DIM, (hk * G + g + 1) * HEAD_DIM)
    o_cmp, qaug = {}, {}
    for hk in range(NSA_KV_HEADS):
        q64 = jnp.concatenate([qt_ref[0, hrow(hk, g), :] for g in range(G)], axis=1)
        q64 = q64 * jnp.asarray(HEAD_DIM ** -0.5, BF16)

        cidx = lax.broadcasted_iota(jnp.int32, (n_half, R), 0)
        cvalid = (cidx * CMP_STRIDE + (CMP_BLOCK - 1) <= t_lane) & (cidx < n_cmp)
        s = jnp.where(cvalid, _dot(kcb_ref[0, hk], q64), NEG_INF)
        m = jnp.max(s, axis=0, keepdims=True)
        e = jnp.where(cvalid, jnp.exp(s - m), 0.0)
        l = jnp.sum(e, axis=0, keepdims=True)
        p_c = e / jnp.where(l > 0.0, l, 1.0)
        o_c = _dot(vcbt_ref[0, hk], p_c.astype(BF16))

        psum = p_c[:, 0:tq]
        for g in range(1, G):
            psum = psum + p_c[:, g * tq:(g + 1) * tq]
        hi, lo = _split2(psum)
        imp2 = _dot(ovt_ref[...], jnp.concatenate([hi, lo], axis=1))
        imp = imp2[:, :tq] + imp2[:, tq:]
        jblk = lax.broadcasted_iota(jnp.int32, (n_sel, tq), 0)
        cur = (q0 + lax.broadcasted_iota(jnp.int32, (n_sel, tq), 1)) // SEL_BLOCK
        forced = (jblk == 0) | (jblk == cur) | (jblk == cur - 1)
        score = jnp.where(forced, SEL_FORCE_SCORE, jnp.where(jblk <= cur, imp, -1.0))
        n_grp = n_sel // 8
        rows = [score[8 * q:8 * q + 8] for q in range(n_grp)]
        ranks = [jnp.zeros((8, tq), F32)] * n_grp
        sub8 = lax.broadcasted_iota(jnp.int32, (8, tq), 0)
        for j in range(n_sel):
            sj = score[j:j + 1, :]
            for q in range(n_grp):
                wins = jnp.where(sj > rows[q], 1.0, 0.0)
                wins_ties = jnp.where(sj >= rows[q], 1.0, 0.0)
                if 8 * q + 7 < j:
                    ahead = wins
                elif 8 * q > j:
                    ahead = wins_ties
                else:
                    ahead = jnp.where(sub8 > j - 8 * q, wins_ties, wins)
                ranks[q] = ranks[q] + ahead
        rank = jnp.concatenate(ranks, axis=0)
        sel_bias = jnp.where(rank < n_top, 0.0, NEG_INF).astype(BF16)
        q_l2 = (q64.astype(F32) * LOG2_E).astype(BF16)
        for g in range(G):
            o_cmp[hk, g] = o_c[:, g * tq:(g + 1) * tq]
            qaug[hk, g] = jnp.concatenate([q_l2[:, g * tq:(g + 1) * tq], sel_bias, pad_rows], axis=0)

    chains = [(hk, g) for hk in range(NSA_KV_HEADS) for g in range(G)]

    def softmax_pv(s_list, vt_of, carry=None):
        m_blk = [jnp.max(s, axis=0, keepdims=True) for s in s_list]
        if carry is None:
            m_new = m_blk
        else:
            m_new = [jnp.maximum(c[0], mb) for c, mb in zip(carry, m_blk)]
        p = [jnp.exp2(s - mn) for s, mn in zip(s_list, m_new)]
        l_blk = [jnp.sum(x, axis=0, keepdims=True) for x in p]
        pv = [_dot(vt_of(c), x.astype(BF16)) for c, x in zip(chains, p)]
        if carry is None:
            return [(mn, lb, a) for mn, lb, a in zip(m_new, l_blk, pv)]
        alpha = [jnp.exp2(c[0] - mn) for c, mn in zip(carry, m_new)]
        return [(mn, c[1] * al + lb, c[2] * al + a)
                for c, mn, al, lb, a in zip(carry, m_new, alpha, l_blk, pv)]

    def vcols(ref, hk, j):
        return ref[0, hk * HEAD_DIM:(hk + 1) * HEAD_DIM, pl.ds(pl.multiple_of(j * tq, tq), tq)]

    wchunk = [jnp.maximum(step - (n_wchunks - 1) + w, 0) for w in range(n_wchunks)]
    kw_rows = [jnp.concatenate([kw_ref[0, hk, j] for j in wchunk], axis=0) for hk in range(NSA_KV_HEADS)]
    vw_cols = [jnp.concatenate([vcols(vwt_ref, hk, j) for j in wchunk], axis=1)
               for hk in range(NSA_KV_HEADS)]
    wbias = wbias_ref[jnp.minimum(step, n_wchunks - 1)]
    s_win = [_dot(kw_rows[hk], qaug[hk, g]) + wbias for hk, g in chains]
    win = softmax_pv(s_win, lambda c: vw_cols[c[0]])

    tri = tri_ref[...]
    s_diag = [_dot(ks_ref[0, hk, step], qaug[hk, g]) + tri for hk, g in chains]
    carry = softmax_pv(s_diag, lambda c: vcols(vst_ref, c[0], step))

    def body(j, flat):
        carry = [tuple(flat[3 * i:3 * i + 3]) for i in range(len(chains))]
        s_j = [_dot(ks_ref[0, hk, j], qaug[hk, g]) for hk, g in chains]
        new = softmax_pv(s_j, lambda c: vcols(vst_ref, c[0], j), carry)
        return tuple(x for c in new for x in c)

    flat = lax.fori_loop(0, step, body, tuple(x for c in carry for x in c))
    sel = [tuple(flat[3 * i:3 * i + 3]) for i in range(len(chains))]

    for i, (hk, g) in enumerate(chains):
        gate = lambda j: gates[(hk * G + g) * 3 + j:(hk * G + g) * 3 + j + 1, :]
        out = (gate(0) * o_cmp[hk, g] + gate(1) * (sel[i][2] / sel[i][1])
               + gate(2) * (win[i][2] / win[i][1]))
        o_ref[0, hrow(hk, g), :] = out.astype(o_ref.dtype)


def _nsa2_call(qt, gt, ks5, vst, kw5, vwt, kcb, vcbt, ovt, tri, wbias, seq):
    B = qt.shape[0]
    tq = NSA2_TQ
    per_b = lambda a: pl.BlockSpec((1,) + a.shape[1:], lambda b, i: (b,) + (0,) * (a.ndim - 1))
    full = lambda a: pl.BlockSpec(a.shape, lambda b, i: (0,) * a.ndim)
    return pl.pallas_call(
        functools.partial(_nsa2_kernel, seq=seq),
        grid=(B, seq // tq),
        in_specs=[pl.BlockSpec((1, NSA_WIDTH, tq), lambda b, i: (b, 0, i)),
                  pl.BlockSpec((1, gt.shape[1], tq), lambda b, i: (b, 0, i)),
                  per_b(ks5), per_b(vst), per_b(kw5), per_b(vwt), per_b(kcb), per_b(vcbt),
                  full(ovt), full(tri), full(wbias)],
        out_specs=pl.BlockSpec((1, NSA_WIDTH, tq), lambda b, i: (b, 0, i)),
        out_shape=jax.ShapeDtypeStruct((B, NSA_WIDTH, seq), BF16),
        compiler_params=pltpu.CompilerParams(
            dimension_semantics=("arbitrary", "arbitrary"), vmem_limit_bytes=VMEM_LIMIT),
        name="nsa_attention",
    )(qt, gt, ks5, vst, kw5, vwt, kcb, vcbt, ovt, tri, wbias)


def _nsa_from_proj(kvc, ksa, kwa, qt, vst, vwt, gt, P):
    B, Hk, S, _ = ksa.shape
    tq = NSA2_TQ
    Dh = HEAD_DIM
    n_sel = S // SEL_BLOCK
    assert S % tq == 0 and WINDOW % tq == 0 and n_sel == NSA_BIAS_ROWS
    n_half = S // CMP_STRIDE
    kv = kvc.reshape(B, n_half, CMP_STRIDE * KVC_COLS)
    w1 = P['nsa_cmp_w1'][0]
    w1h = w1.astype(BF16).reshape(2, 2, CMP_STRIDE, 1, HEAD_DIM, CMP_HIDDEN)
    zero = jnp.zeros_like(w1h[0, 0])

    def block(j, a, h):
        parts = [w1h[j, a] if hh == h else zero for hh in range(Hk)]
        return jnp.concatenate(parts, axis=1).reshape(CMP_STRIDE * NSA_KV_WIDTH, CMP_HIDDEN)

    wab = jnp.stack([jnp.concatenate([block(j, a, h) for a in range(2) for h in range(Hk)], axis=1)
                     for j in range(2)])
    pos8 = jnp.broadcast_to(P['nsa_cmp_pos'][0].reshape(2, 1, CMP_BLOCK * HEAD_DIM),
                            (2, 8, CMP_BLOCK * HEAD_DIM)).astype(BF16)
    w2 = P['nsa_cmp_w2'][0].astype(BF16)
    kcb, vcbt = _compress_call(kv, wab, pos8, w1.astype(BF16), w2, jnp.swapaxes(w2, 1, 2))

    ks5 = ksa.reshape(B, Hk, S // tq, tq, NSA_AUG)
    kw5 = kwa.reshape(B, Hk, S // tq, tq, NSA_AUG)

    n_cmp = (S - CMP_BLOCK) // CMP_STRIDE + 1
    cmp_start = np.arange(n_half) * CMP_STRIDE
    sel_start = np.arange(n_sel) * SEL_BLOCK
    overlap = ((cmp_start[:, None] <= sel_start[None, :] + SEL_BLOCK - 1)
               & (cmp_start[:, None] + CMP_BLOCK - 1 >= sel_start[None, :])
               & (np.arange(n_half)[:, None] < n_cmp)).astype(np.float32)
    tri = np.where(np.arange(tq)[:, None] <= np.arange(tq)[None, :], 0.0, NEG_INF).astype(np.float32)
    n_w = WINDOW // tq
    masked = np.full((tq, tq), NEG_INF, np.float32)
    clear = np.zeros((tq, tq), np.float32)
    wbias = np.stack([np.concatenate([masked] * (n_w - v) + [clear] * v + [tri], axis=0) for v in range(n_w)]
                     + [np.concatenate([NEG_INF - tri] + [clear] * (n_w - 1) + [tri], axis=0)])
    return _nsa2_call(qt, gt, ks5, vst, kw5, vwt, kcb, vcbt, jnp.asarray(overlap.T, BF16),
                      jnp.asarray(tri), jnp.asarray(wbias), S)


def _merge_kernel(x_ref, ya_ref, ybt_ref, ln_ref, wg_ref, wa_ref, wb_ref, wo_ref, o_ref):
    D = x_ref.shape[-1]
    x = x_ref[...]
    u = _rms(x, ln_ref[...]).astype(BF16)
    ga = _sigmoid(_dot(u, wg_ref[:, :D]))
    ta = _dot(ya_ref[...], wa_ref[...])
    gb = _sigmoid(_dot(u, wg_ref[:, D:]))
    tb = _dot_tn(ybt_ref[0], wb_ref[...])
    mix = (ga * ta + gb * tb).astype(BF16)
    o_ref[...] = x + _dot(mix, wo_ref[...])


def _merge_call(x2, ya2, ybt, ln_g, wg, wa, wb, wo, tm=1024):
    T, D = x2.shape
    tiles_per_seq = ybt.shape[2] // tm
    row = lambda w: pl.BlockSpec((tm, w), lambda i: (i, 0))
    full = lambda a: pl.BlockSpec(a.shape, lambda i: (0,) * a.ndim)
    ybt_spec = pl.BlockSpec((1, ybt.shape[1], tm), lambda i: (i // tiles_per_seq, 0, i % tiles_per_seq))
    return pl.pallas_call(
        _merge_kernel,
        grid=(T // tm,),
        in_specs=[row(D), row(ya2.shape[1]), ybt_spec, full(ln_g), full(wg),
                  full(wa), full(wb), full(wo)],
        out_specs=row(D),
        out_shape=jax.ShapeDtypeStruct((T, D), F32),
        compiler_params=pltpu.CompilerParams(
            dimension_semantics=("arbitrary",), vmem_limit_bytes=VMEM_LIMIT),
        name="merge",
    )(x2, ya2, ybt, ln_g, wg, wa, wb, wo)


FFN_HALO = 8


def _rms(x, g):
    return x * lax.rsqrt(jnp.mean(x * x, axis=-1, keepdims=True) + NORM_EPS) * g


def _ffn_kernel(h_ref, halo_ref, p_ref, ln_ref, wup_ref, cw_ref, cb_ref, wdn_ref, wpg_ref, wpp_ref,
                o_ref, up0a_ref, up0b_ref, up1a_ref, up1b_ref, act_ref, *, tiles_per_seq, fc):
    up_refs = ((up0a_ref, up0b_ref), (up1a_ref, up1b_ref))
    tm = act_ref.shape[0]
    D = halo_ref.shape[1]
    V = tm // 8
    d_ff = wdn_ref.shape[0]
    slabs = lambda ref, w: jnp.swapaxes(ref[0], 0, 1).reshape(tm, w)
    h = slabs(h_ref, D)
    first = (pl.program_id(0) % tiles_per_seq) == 0
    halo = jnp.where(first, 0.0, halo_ref[...])
    ln2, ln3, lnf = ln_ref[0:1, :], ln_ref[1:2, :], ln_ref[2:3, :]
    u = jnp.concatenate([_rms(halo, ln2), _rms(h, ln2)], axis=0).astype(BF16)

    n_chunks = d_ff // fc
    sub = lax.broadcasted_iota(jnp.int32, (8, 1), 0)

    def project(c):
        for half in range(2):
            col = half * d_ff + c * fc
            up_refs[c % 2][half][...] = _dot(u, wup_ref[:, col:col + fc])

    def conv(c, half):
        ref = up_refs[c % 2][half]
        col = half * d_ff + c * fc
        halo_up = ref[0:FFN_HALO, :]
        last = lambda k: ref[FFN_HALO + tm - 8 * k:FFN_HALO + tm - 8 * (k - 1), :]
        wrap1 = pltpu.roll(jnp.where(sub == 7, halo_up, last(1)), 1, axis=0)
        wrap2 = pltpu.roll(jnp.where(sub == 7, pltpu.roll(halo_up, 1, axis=0), last(2)), 1, axis=0)
        x0 = ref[FFN_HALO:FFN_HALO + tm, :]
        x1 = jnp.concatenate([wrap1, ref[FFN_HALO:FFN_HALO + tm - 8, :]], axis=0)
        x2 = jnp.concatenate([wrap2, wrap1, ref[FFN_HALO:FFN_HALO + tm - 16, :]], axis=0)
        tap = lambda j: cw_ref[j:j + 1, col:col + fc]
        return cb_ref[:, col:col + fc] + tap(0) * x2 + tap(1) * x1 + tap(2) * x0

    project(0)
    for c in range(n_chunks):
        if c + 1 < n_chunks:
            project(c + 1)
        a = conv(c, 0)
        b = conv(c, 1)
        act_ref[:, c * fc:(c + 1) * fc] = (a * _sigmoid(a) * b).astype(BF16)
    h2 = h + _dot(act_ref[...], wdn_ref[...])
    gate = _sigmoid(_dot(_rms(h2, ln3).astype(BF16), wpg_ref[...]))
    h3 = h2 + gate * _dot(slabs(p_ref, p_ref.shape[3]).astype(BF16), wpp_ref[...])
    o_ref[0] = jnp.swapaxes(_rms(h3, lnf).reshape(V, 8, D), 0, 1)


FFN_TM = 512
FFN_FC = 256


def _ffn_call(h2d, p2d, lns, wup, cw, cb, wdn, wpg, wpp, seq):
    T, D = h2d.shape
    tm, fc = FFN_TM, FFN_FC
    assert CONV_WIDTH == 3 and seq % tm == 0 and wdn.shape[0] % fc == 0
    tiles_per_seq = seq // tm
    runs = lambda x: x.reshape(T // tm, 8, tm // 8, x.shape[1])
    run_spec = lambda w: pl.BlockSpec((1, 8, tm // 8, w), lambda i: (i, 0, 0, 0))
    full = lambda a: pl.BlockSpec(a.shape, lambda i: (0,) * a.ndim, pipeline_mode=pl.Buffered(1))
    halo = pl.BlockSpec((FFN_HALO, D), lambda i: (jnp.maximum(i * (tm // FFN_HALO) - 1, 0), 0))
    out = pl.pallas_call(
        functools.partial(_ffn_kernel, tiles_per_seq=tiles_per_seq, fc=fc),
        grid=(T // tm,),
        in_specs=[run_spec(D), halo, run_spec(p2d.shape[1]), full(lns), full(wup), full(cw), full(cb),
                  full(wdn), full(wpg), full(wpp)],
        out_specs=run_spec(D),
        out_shape=jax.ShapeDtypeStruct((T // tm, 8, tm // 8, D), F32),
        scratch_shapes=[pltpu.VMEM((FFN_HALO + tm, fc), F32)] * 4 + [pltpu.VMEM((tm, wdn.shape[0]), BF16)],
        compiler_params=pltpu.CompilerParams(
            dimension_semantics=("arbitrary",), vmem_limit_bytes=VMEM_LIMIT),
        name="ffn",
    )(runs(h2d), h2d, runs(p2d), lns, wup, cw, cb, wdn, wpg, wpp)
    return out.reshape(T, D)


def _prep_proj_weights(w_in, mu_wag, w1, a1, g1):
    D = w_in.shape[0]
    sizes = (RW_WIDTH, RW_WIDTH, RW_WIDTH, NSA_WIDTH) + (NSA_KV_WIDTH,) * 6 + (3 * NSA_Q_HEADS, D, D)
    offs = np.concatenate([[0], np.cumsum(sizes)])
    part = lambda i, j: w_in[:, offs[i]:offs[j]]
    mw, ma, mg = mu_wag[0][:, None], mu_wag[1][:, None], mu_wag[2][:, None]
    zg = jnp.zeros((D, RW_GATE_PAD - RW_GATE_LORA), F32)
    rw = jnp.concatenate([
        part(0, 3),
        (1.0 - mw) * w1, (1.0 - ma) * a1,
        mw * w1, ma * a1,
        (1.0 - mg) * g1, zg,
        mg * g1, zg], axis=1)
    def widen(w):
        w = w.reshape(D, NSA_KV_HEADS, HEAD_DIM)
        return jnp.concatenate([w, jnp.zeros_like(w)], axis=2).reshape(D, KEY_COLS)

    w_rows = jnp.concatenate([rw, part(4, 6), widen(part(6, 7)), widen(part(8, 9))], axis=1)
    w_cols = jnp.concatenate([part(3, 4), part(7, 8), part(9, 10), part(10, 11),
                              jnp.zeros((D, NSA_GATE_ROWS - 3 * NSA_Q_HEADS), F32)], axis=1)
    return w_rows.astype(BF16), w_cols.T.astype(BF16), part(11, 13).astype(BF16)


def _prep_rwkv_weights(w2, a2, g2):
    z = jnp.zeros_like(w2)
    w2a2 = jnp.concatenate([jnp.concatenate([w2, z], axis=1),
                            jnp.concatenate([z, a2], axis=1)], axis=0).astype(BF16)
    g2p = jnp.concatenate([g2, jnp.zeros((RW_GATE_PAD - RW_GATE_LORA, RW_WIDTH), F32)],
                          axis=0).astype(BF16)
    return w2a2, g2p


def _rwkv_from_proj(rw3, P):
    w2a2, g2p = _prep_rwkv_weights(P['rw_w2'][0], P['rw_a2'][0], P['rw_g2'][0])
    vecs = jnp.stack([P['rw_w0'][0], P['rw_a0'][0], P['rw_k_k'][0], P['rw_k_a'][0],
                      P['rw_r_k'][0].reshape(-1), P['rw_lnx_g'][0], P['rw_lnx_b'][0],
                      jnp.zeros((RW_WIDTH,), F32)], axis=0)
    return _rwkv_call(rw3, P['rw_mu_rkv'][0], vecs, w2a2, g2p)


def kernel(x, p, ln1_g, w_in, rw_mu_rkv, rw_mu_wag, rw_w0, rw_w1, rw_w2, rw_a0, rw_a1, rw_a2, rw_g1, rw_g2, rw_k_k, rw_k_a, rw_r_k, rw_lnx_g, rw_lnx_b, nsa_cmp_pos, nsa_cmp_w1, nsa_cmp_w2, w_out_a, w_out_b, w_out, ln2_g, w_up, conv_w, conv_b, w_down, ln3_g, w_ple_gate, w_ple_proj, ln_f_g):
    B, S, D = x.shape
    T = B * S
    assert w_in.shape[0] == 1, "single-layer block"
    P = dict(rw_mu_rkv=rw_mu_rkv, rw_w0=rw_w0, rw_w2=rw_w2, rw_a0=rw_a0, rw_a2=rw_a2, rw_g2=rw_g2,
             rw_k_k=rw_k_k, rw_k_a=rw_k_a, rw_r_k=rw_r_k, rw_lnx_g=rw_lnx_g, rw_lnx_b=rw_lnx_b,
             nsa_cmp_pos=nsa_cmp_pos, nsa_cmp_w1=nsa_cmp_w1, nsa_cmp_w2=nsa_cmp_w2)
    h = x.reshape(T, D)
    w_rows, w_cols, w_gates = _prep_proj_weights(w_in[0], rw_mu_wag[0], rw_w1[0], rw_a1[0], rw_g1[0])
    rw, kvc, ksa, kwa, qt, vst, vwt, gt = _proj_call(h, ln1_g[0][None], w_rows, w_cols, S)
    ya = _rwkv_from_proj(rw.reshape(B, S, RW_COLS), P)
    ybt = _nsa_from_proj(kvc, ksa, kwa, qt, vst, vwt, gt, P)
    h1 = _merge_call(h, ya.reshape(T, RW_WIDTH), ybt, ln1_g[0][None], w_gates,
                     w_out_a[0].astype(BF16), w_out_b[0].astype(BF16), w_out[0].astype(BF16))
    lns = jnp.stack([ln2_g[0], ln3_g[0], ln_f_g], axis=0)
    out = _ffn_call(h1, p[0].reshape(T, -1), lns, w_up[0].astype(BF16), conv_w[0], conv_b[0][None],
                    w_down[0].astype(BF16), w_ple_gate[0].astype(BF16), w_ple_proj[0].astype(BF16), S)
    return out.reshape(B, S, D)
```

```python
import functools

import numpy as np
import jax
import jax.numpy as jnp
from jax import lax
from jax.experimental import pallas as pl
from jax.experimental.pallas import tpu as pltpu

F32 = jnp.float32
BF16 = jnp.bfloat16

HEAD_DIM = 64
NORM_EPS = 1e-6
NEG_INF = -1e30

RW_HEADS = 8
RW_WIDTH = RW_HEADS * HEAD_DIM
RW_DECAY_LORA = 64
RW_AAA_LORA = 64
RW_GATE_LORA = 160
RW_LNX_EPS = 64e-5
RW_CHUNK = 64
RW_GROUP = 4
RW_GROUP_W = RW_GROUP * HEAD_DIM
RW_GATE_PAD = 256

NSA_Q_HEADS = 8
NSA_KV_HEADS = 2
NSA_GROUP = NSA_Q_HEADS // NSA_KV_HEADS
NSA_WIDTH = NSA_Q_HEADS * HEAD_DIM
NSA_KV_WIDTH = NSA_KV_HEADS * HEAD_DIM
CMP_BLOCK = 32
CMP_STRIDE = 16
CMP_HIDDEN = 128
SEL_BLOCK = 64
SEL_TOP = 16
SEL_FORCE_SCORE = 1e4
WINDOW = 512

CONV_WIDTH = 3

RW_COLS = 3 * RW_WIDTH + 2 * 128 + 2 * RW_GATE_PAD
GATE_COLS = 2 * 1024

V7X_VMEM_BYTES = 64 * 1024 * 1024
VMEM_LIMIT = V7X_VMEM_BYTES - 8 * 1024 * 1024


def _dot(a, b):
    return jnp.dot(a, b, preferred_element_type=F32)


def _dot_nt(a, b):
    return lax.dot_general(a, b, (((1,), (1,)), ((), ())), preferred_element_type=F32)


def _dot_tn(a, b):
    return lax.dot_general(a, b, (((0,), (0,)), ((), ())), preferred_element_type=F32)


def _split2(x):
    hi = x.astype(BF16)
    lo = (x - hi.astype(F32)).astype(BF16)
    return hi, lo


def _sigmoid(x):
    return 1.0 / (1.0 + jnp.exp(-x))


LOG2_E = 1.4426950408889634
DECAY_SCALE_LOG2 = 0.6065306597126334 * LOG2_E


PROJ_TM = 1024
PROJ_CHUNK = 768
KVC_COLS = 2 * NSA_KV_WIDTH
KEY_COLS = NSA_KV_HEADS * 128
NSA_GATE_ROWS = 32
T_ROWS = NSA_WIDTH + 2 * NSA_KV_WIDTH + NSA_GATE_ROWS


def _proj_kernel(x_ref, g_ref, w_ref, wt_ref, rw_ref, kvc_ref, ksa_ref, kwa_ref, gate_ref,
                 qt_ref, vst_ref, vwt_ref, gt_ref, *, tiles_per_seq):
    tm = x_ref.shape[0]
    x = x_ref[...]
    ms = jnp.mean(x * x, axis=-1, keepdims=True)
    u = (x * lax.rsqrt(ms + NORM_EPS) * g_ref[...]).astype(BF16)

    col = 0
    for o_ref in (rw_ref, kvc_ref):
        width = o_ref.shape[-1]
        for c in range(0, width, PROJ_CHUNK):
            hi = min(c + PROJ_CHUNK, width)
            o_ref[:, c:hi] = _dot(u, w_ref[:, col + c:col + hi]).astype(o_ref.dtype)
        col += width

    s0 = (pl.program_id(0) % tiles_per_seq) * tm
    blk = (s0 + lax.broadcasted_iota(jnp.int32, (tm, KEY_COLS), 0)) // SEL_BLOCK
    lane = lax.broadcasted_iota(jnp.int32, (tm, KEY_COLS), 1) % 128
    onehot = jnp.where(lane - HEAD_DIM == blk, 1.0, 0.0)
    ks = _dot(u, w_ref[:, col:col + KEY_COLS]) + onehot
    kw = _dot(u, w_ref[:, col + KEY_COLS:col + 2 * KEY_COLS])
    for hk in range(NSA_KV_HEADS):
        ksa_ref[0, hk] = ks[:, hk * 128:(hk + 1) * 128].astype(ksa_ref.dtype)
        kwa_ref[0, hk] = kw[:, hk * 128:(hk + 1) * 128].astype(kwa_ref.dtype)
    col += 2 * KEY_COLS

    width = gate_ref.shape[-1]
    for c in range(0, width, PROJ_CHUNK):
        hi = min(c + PROJ_CHUNK, width)
        gate_ref[:, c:hi] = _dot(u, w_ref[:, col + c:col + hi]).astype(gate_ref.dtype)

    t = _dot_nt(wt_ref[...], u)
    row = 0
    for o_ref in (qt_ref, vst_ref, vwt_ref, gt_ref):
        n = o_ref.shape[1]
        o_ref[0] = t[row:row + n].astype(o_ref.dtype)
        row += n


def _proj_call(x2, g, w_all, wt_all, seq):
    T, D = x2.shape
    tm = PROJ_TM
    B = T // seq
    tps = seq // tm
    rows = lambda w: pl.BlockSpec((tm, w), lambda i: (i, 0))
    full = lambda a: pl.BlockSpec(a.shape, lambda i: (0,) * a.ndim, pipeline_mode=pl.Buffered(1))
    keys = pl.BlockSpec((1, NSA_KV_HEADS, tm, 128), lambda i: (i // tps, 0, i % tps, 0))
    tcols = lambda n: pl.BlockSpec((1, n, tm), lambda i: (i // tps, 0, i % tps))
    sds = jax.ShapeDtypeStruct
    return pl.pallas_call(
        functools.partial(_proj_kernel, tiles_per_seq=tps),
        grid=(T // tm,),
        in_specs=[rows(D), full(g), full(w_all), full(wt_all)],
        out_specs=[rows(RW_COLS), rows(KVC_COLS), keys, keys, rows(GATE_COLS),
                   tcols(NSA_WIDTH), tcols(NSA_KV_WIDTH), tcols(NSA_KV_WIDTH), tcols(NSA_GATE_ROWS)],
        out_shape=[sds((T, RW_COLS), BF16), sds((T, KVC_COLS), BF16),
                   sds((B, NSA_KV_HEADS, seq, 128), BF16), sds((B, NSA_KV_HEADS, seq, 128), BF16),
                   sds((T, GATE_COLS), BF16),
                   sds((B, NSA_WIDTH, seq), BF16), sds((B, NSA_KV_WIDTH, seq), BF16),
                   sds((B, NSA_KV_WIDTH, seq), BF16), sds((B, NSA_GATE_ROWS, seq), BF16)],
        compiler_params=pltpu.CompilerParams(
            dimension_semantics=("arbitrary",), vmem_limit_bytes=VMEM_LIMIT),
        name="proj",
    )(x2, g, w_all, wt_all)


def _rwkv_kernel(x_ref, mu_ref, vec_ref, w2a2_ref, g2_ref, o_ref, state_ref, prev_ref):
    C = RW_CHUNK
    GW = RW_GROUP_W
    W = RW_WIDTH
    NB = x_ref.shape[0]
    R = NB * C
    t_idx = pl.program_id(1)
    ops = {}

    @pl.when(t_idx == 0)
    def _():
        state_ref[...] = jnp.zeros_like(state_ref)
        prev_ref[...] = jnp.zeros_like(prev_ref)

    mu = mu_ref[...]
    w0, a0, k_k, k_a, r_k, lnx_g, lnx_b = (vec_ref[i:i + 1, :] for i in range(7))

    gr = lax.broadcasted_iota(jnp.int32, (GW, GW), 0) // HEAD_DIM
    gc = lax.broadcasted_iota(jnp.int32, (GW, GW), 1) // HEAD_DIM
    blk = gr == gc
    ones_bd = jnp.where(blk, 1.0, 0.0).astype(BF16)

    def headsums(zs):
        parts = []
        for z in zs:
            zb = z.astype(BF16)
            parts += [zb[:, :GW], zb[:, GW:]]
        s = _dot(jnp.concatenate(parts, axis=0), ones_bd)
        return [jnp.concatenate([s[2 * R * i:2 * R * i + R], s[2 * R * i + R:2 * R * (i + 1)]], axis=1)
                for i in range(len(zs))]

    t_n = lax.broadcasted_iota(jnp.int32, (C, GW), 0)
    s_n = lax.broadcasted_iota(jnp.int32, (C, GW), 1) % HEAD_DIM
    strict = t_n > s_n
    incl = t_n >= s_n
    eye_n = jnp.where(t_n == s_n, 1.0, 0.0)

    def bd(z):
        z4 = jnp.concatenate([z.astype(F32)] * RW_GROUP, axis=0)
        return jnp.where(blk, z4, 0.0).astype(BF16)

    def prepare():
        x = x_ref[...].reshape(R, RW_COLS).astype(F32)
        rolled = pltpu.roll(x, 1, axis=0)
        row8 = lax.broadcasted_iota(jnp.int32, (8, 1), 0)
        pieces = []
        for bi in range(NB):
            pieces.append(jnp.where(row8 == 0, prev_ref[bi, 0:1, :], rolled[bi * C:bi * C + 8]))
            pieces.append(rolled[bi * C + 8:(bi + 1) * C])
            prev_ref[bi, 0:1, :] = x[(bi + 1) * C - 1:(bi + 1) * C, :]
        xs = jnp.concatenate(pieces, axis=0)

        def lerp(j):
            cur = x[:, j * W:(j + 1) * W]
            return cur + (xs[:, j * W:(j + 1) * W] - cur) * mu[j:j + 1, :]

        r, k, v = lerp(0), lerp(1), lerp(2)
        o = 3 * W
        pre_a = x[:, o:o + 128] + xs[:, o + 128:o + 256]
        lane = lax.broadcasted_iota(jnp.int32, (R, 128), 1)
        h_a = jnp.where(lane < RW_DECAY_LORA, jnp.tanh(pre_a), pre_a)
        lwa = _dot(h_a.astype(BF16), w2a2_ref[...])
        o += 256
        pre_g = x[:, o:o + RW_GATE_PAD] + xs[:, o + RW_GATE_PAD:o + 2 * RW_GATE_PAD]
        g = _dot(_sigmoid(pre_g).astype(BF16), g2_ref[...])
        ld = (-DECAY_SCALE_LOG2) * _sigmoid(w0 + lwa[:, :W])
        a = _sigmoid(a0 + lwa[:, W:])
        kkr = k * k_k
        k2 = k * (1.0 + (a - 1.0) * k_a)
        kk_ss, bonus = headsums([kkr * kkr, r * k2 * r_k])
        kk = kkr / jnp.maximum(jnp.sqrt(kk_ss), 1e-12)
        b = kk * a
        tr = lax.broadcasted_iota(jnp.int32, (R, R), 0)
        tc = lax.broadcasted_iota(jnp.int32, (R, R), 1)
        tri = jnp.where((tr >= tc) & (tr // C == tc // C), 1.0, 0.0).astype(BF16)
        l_inc = _dot(tri, jnp.concatenate(_split2(ld), axis=1))
        l_inc = l_inc[:, :W] + l_inc[:, W:]
        l_end = [l_inc[(bi + 1) * C - 1:(bi + 1) * C] for bi in range(NB)]
        e_neg = jnp.exp2(-l_inc)
        e_tail = jnp.concatenate([jnp.exp2(l_end[bi] - l_inc[bi * C:(bi + 1) * C]) for bi in range(NB)],
                                 axis=0)
        new_opb = (-kk * jnp.exp2(l_inc - ld), r * jnp.exp2(l_inc), b * e_neg, k2 * e_neg,
                   b * e_tail, k2 * e_tail)
        cast = (True, True, False, False, True, True)
        ops['matmul'] = tuple(z.astype(BF16) if c else z for z, c in zip(new_opb, cast))
        ops['output'] = (v, bonus, g)
        ops['decay'] = jnp.concatenate([jnp.broadcast_to(jnp.exp2(le), (8, W)) for le in l_end], axis=0)

    def consume():
        n_grp = W // GW
        chains = [(bi, gi) for bi in range(NB) for gi in range(n_grp)]
        cut = lambda z, c: z[c[0] * C:(c[0] + 1) * C, c[1] * GW:(c[1] + 1) * GW]
        each = lambda f, *lists: [f(*args) for args in zip(*lists)]

        p_a_hat, p_r_hat, p_b_hat, p_k_hat, p_b_tail, p_k_tail = ops['matmul']
        p_v, p_bonus, p_g = ops['output']
        p_elc = ops['decay']
        a_h = [cut(p_a_hat, c) for c in chains]
        r_h = [cut(p_r_hat, c) for c in chains]
        ar = each(lambda x1, x2: jnp.concatenate([x1, x2], axis=0), a_h, r_h)
        m1 = each(_dot_nt, ar, [bd(cut(p_b_hat, c)) for c in chains])
        m2 = each(_dot_nt, ar, [bd(cut(p_k_hat, c)) for c in chains])
        m_ab = [jnp.where(strict, m[:C], 0.0) for m in m1]
        m_rb = [jnp.where(incl, m[C:], 0.0) for m in m1]
        m_ak = [jnp.where(strict, m[:C], 0.0) for m in m2]
        m_rk = [jnp.where(incl, m[C:], 0.0) for m in m2]

        tinv = [eye_n + m for m in m_ab]
        p = each(lambda m: _dot(m.astype(BF16), bd(m)), m_ab)
        power = 2
        while 2 * power < C:
            tp = each(lambda t, q: _dot(jnp.concatenate([t, q], axis=0).astype(BF16), bd(q)), tinv, p)
            tinv = each(lambda t, x1: t + x1[:C], tinv, tp)
            p = [x1[C:] for x1 in tp]
            power *= 2
        tinv = each(lambda t, q: t + _dot(t.astype(BF16), bd(q)), tinv, p)

        s_old = [state_ref[i * GW:(i + 1) * GW, :] for i in range(len(chains))]
        s_bf = [s.astype(BF16) for s in s_old]
        vg = [cut(p_v, c) for c in chains]
        bd_v = [bd(x1) for x1 in vg]
        xz = each(lambda x1, s, m, bv: _dot_nt(x1, s) + _dot(m.astype(BF16), bv), a_h, s_bf, m_ak, bd_v)
        u = each(lambda t, x1: _dot(t.astype(BF16), bd(x1)), tinv, xz)
        y = each(lambda x1, s, mb, mk, uu, bv:
                 _dot_nt(x1, s) + _dot(jnp.concatenate([mb, mk], axis=1).astype(BF16),
                                       jnp.concatenate([bd(uu), bv], axis=0)),
                 r_h, s_bf, m_rb, m_rk, u, bd_v)
        new_states = []
        for c, uu, vv, s in zip(chains, u, vg, s_old):
            upd = _dot_tn(jnp.concatenate([uu, vv], axis=0).astype(BF16),
                          jnp.concatenate([cut(p_b_tail, c), cut(p_k_tail, c)], axis=0))
            decay = p_elc[c[0] * 8:c[0] * 8 + 1, c[1] * GW:(c[1] + 1) * GW]
            new_states.append(s * decay + jnp.where(blk, upd, 0.0))
        state_ref[...] = jnp.concatenate(new_states, axis=0)
        y_rows = [jnp.concatenate(y[bi * n_grp:(bi + 1) * n_grp], axis=1) for bi in range(NB)]
        y = jnp.concatenate(y_rows, axis=0)
        mean = headsums([y])[0] * (1.0 / HEAD_DIM)
        yc = y - mean
        var = headsums([yc * yc])[0] * (1.0 / HEAD_DIM)
        yn = yc * lax.rsqrt(var + RW_LNX_EPS) * lnx_g + lnx_b
        yn = yn + p_bonus * p_v
        o_ref[...] = (yn * p_g).reshape(NB, C, W).astype(o_ref.dtype)

    prepare()
    consume()


RW_SEQS_PER_STEP = 8


def _rwkv_call(rw3, mu, vecs, w2a2, g2p):
    B, S, _ = rw3.shape
    C = RW_CHUNK
    nb = RW_SEQS_PER_STEP if B % RW_SEQS_PER_STEP == 0 else 1
    n_groups = RW_WIDTH // RW_GROUP_W
    return pl.pallas_call(
        _rwkv_kernel,
        grid=(B // nb, S // C),
        in_specs=[
            pl.BlockSpec((nb, C, RW_COLS), lambda b, t: (b, t, 0)),
            pl.BlockSpec(mu.shape, lambda b, t: (0, 0)),
            pl.BlockSpec(vecs.shape, lambda b, t: (0, 0)),
            pl.BlockSpec(w2a2.shape, lambda b, t: (0, 0)),
            pl.BlockSpec(g2p.shape, lambda b, t: (0, 0)),
        ],
        out_specs=pl.BlockSpec((nb, C, RW_WIDTH), lambda b, t: (b, t, 0)),
        out_shape=jax.ShapeDtypeStruct((B, S, RW_WIDTH), BF16),
        scratch_shapes=[
            pltpu.VMEM((nb * n_groups * RW_GROUP_W, RW_GROUP_W), F32),
            pltpu.VMEM((nb, 8, RW_COLS), F32),
        ],
        compiler_params=pltpu.CompilerParams(
            dimension_semantics=("arbitrary", "arbitrary"), vmem_limit_bytes=VMEM_LIMIT),
        name="rwkv",
    )(rw3, mu, vecs, w2a2, g2p)


def _compress_kernel(kv_ref, wab_ref, pos_ref, w1_ref, w2_ref, w2t_ref, kcb_ref, vcbt_ref):
    n_half = kv_ref.shape[1]
    for j in range(2):
        src = jnp.concatenate(
            [kv_ref[0, :, l * KVC_COLS + j * NSA_KV_WIDTH:l * KVC_COLS + (j + 1) * NSA_KV_WIDTH]
             for l in range(CMP_STRIDE)], axis=1)
        pab = _dot(src, wab_ref[j])
        half = NSA_KV_HEADS * CMP_HIDDEN
        pa, pb = pab[:, :half], pab[:, half:]
        pb = pltpu.roll(pb, n_half - 1, axis=0)
        pos_term = _dot(pos_ref[j], w1_ref[j])[0:1]
        hid = pa + pb + jnp.concatenate([pos_term] * NSA_KV_HEADS, axis=1)
        act = (hid * _sigmoid(hid)).astype(BF16)
        for hk in range(NSA_KV_HEADS):
            a_h = act[:, hk * CMP_HIDDEN:(hk + 1) * CMP_HIDDEN]
            if j == 0:
                kcb_ref[0, hk] = _dot(a_h, w2_ref[j]).astype(kcb_ref.dtype)
            else:
                vcbt_ref[0, hk] = _dot_nt(w2t_ref[j], a_h).astype(vcbt_ref.dtype)


def _compress_call(kv, wab, pos8, w1, w2, w2t):
    B, n_half, width = kv.shape
    full = lambda a: pl.BlockSpec(a.shape, lambda b: (0,) * a.ndim)
    return pl.pallas_call(
        _compress_kernel,
        grid=(B,),
        in_specs=[pl.BlockSpec((1, n_half, width), lambda b: (b, 0, 0)),
                  full(wab), full(pos8), full(w1), full(w2), full(w2t)],
        out_specs=[pl.BlockSpec((1, NSA_KV_HEADS, n_half, HEAD_DIM), lambda b: (b, 0, 0, 0)),
                   pl.BlockSpec((1, NSA_KV_HEADS, HEAD_DIM, n_half), lambda b: (b, 0, 0, 0))],
        out_shape=[jax.ShapeDtypeStruct((B, NSA_KV_HEADS, n_half, HEAD_DIM), BF16),
                   jax.ShapeDtypeStruct((B, NSA_KV_HEADS, HEAD_DIM, n_half), BF16)],
        compiler_params=pltpu.CompilerParams(
            dimension_semantics=("arbitrary",), vmem_limit_bytes=VMEM_LIMIT),
        name="nsa_compress",
    )(kv, wab, pos8, w1, w2, w2t)


NSA_TQ = 256
NSA_AUG = 128
NSA_BIAS_ROWS = 32


def _nsa_kernel(qt_ref, gt_ref, ks_ref, vst_ref, kw_ref, vwt_ref, kcb_ref, vcbt_ref, ovt_ref,
                 tri_ref, wbias_ref, o_ref, *, seq):
    tq = NSA_TQ
    G = NSA_GROUP
    R = G * tq
    n_half = kcb_ref.shape[2]
    n_cmp = n_half - 1
    n_sel = seq // SEL_BLOCK
    n_top = min(SEL_TOP, n_sel)
    n_wchunks = WINDOW // tq + 1
    step = pl.program_id(1)
    q0 = step * tq

    t_lane = q0 + lax.broadcasted_iota(jnp.int32, (1, R), 1) % tq
    gates = _sigmoid(gt_ref[0].astype(F32))
    pad_rows = jnp.zeros((NSA_AUG - HEAD_DIM - NSA_BIAS_ROWS, tq), BF16)

    hrow = lambda hk, g: slice((hk * G + g) * HEAD_DIM, (hk * G + g + 1) * HEAD_DIM)
    o_cmp, qaug = {}, {}
    for hk in range(NSA_KV_HEADS):
        q64 = jnp.concatenate([qt_ref[0, hrow(hk, g), :] for g in range(G)], axis=1)
        q64 = q64 * jnp.asarray(HEAD_DIM ** -0.5, BF16)

        cidx = lax.broadcasted_iota(jnp.int32, (n_half, R), 0)
        cvalid = (cidx * CMP_STRIDE + (CMP_BLOCK - 1) <= t_lane) & (cidx < n_cmp)
        s = jnp.where(cvalid, _dot(kcb_ref[0, hk], q64), NEG_INF)
        m = jnp.max(s, axis=0, keepdims=True)
        e = jnp.where(cvalid, jnp.exp(s - m), 0.0)
        l = jnp.sum(e, axis=0, keepdims=True)
        p_c = e / jnp.where(l > 0.0, l, 1.0)
        o_c = _dot(vcbt_ref[0, hk], p_c.astype(BF16))

        psum = p_c[:, 0:tq]
        for g in range(1, G):
            psum = psum + p_c[:, g * tq:(g + 1) * tq]
        hi, lo = _split2(psum)
        imp2 = _dot(ovt_ref[...], jnp.concatenate([hi, lo], axis=1))
        imp = imp2[:, :tq] + imp2[:, tq:]
        jblk = lax.broadcasted_iota(jnp.int32, (n_sel, tq), 0)
        cur = (q0 + lax.broadcasted_iota(jnp.int32, (n_sel, tq), 1)) // SEL_BLOCK
        forced = (jblk == 0) | (jblk == cur) | (jblk == cur - 1)
        score = jnp.where(forced, SEL_FORCE_SCORE, jnp.where(jblk <= cur, imp, -1.0))
        n_grp = n_sel // 8
        rows = [score[8 * q:8 * q + 8] for q in range(n_grp)]
        ranks = [jnp.zeros((8, tq), F32)] * n_grp
        sub8 = lax.broadcasted_iota(jnp.int32, (8, tq), 0)
        for j in range(n_sel):
            sj = score[j:j + 1, :]
            for q in range(n_grp):
                wins = jnp.where(sj > rows[q], 1.0, 0.0)
                wins_ties = jnp.where(sj >= rows[q], 1.0, 0.0)
                if 8 * q + 7 < j:
                    ahead = wins
                elif 8 * q > j:
                    ahead = wins_ties
                else:
                    ahead = jnp.where(sub8 > j - 8 * q, wins_ties, wins)
                ranks[q] = ranks[q] + ahead
        rank = jnp.concatenate(ranks, axis=0)
        sel_bias = jnp.where(rank < n_top, 0.0, NEG_INF).astype(BF16)
        q_l2 = (q64.astype(F32) * LOG2_E).astype(BF16)
        for g in range(G):
            o_cmp[hk, g] = o_c[:, g * tq:(g + 1) * tq]
            qaug[hk, g] = jnp.concatenate([q_l2[:, g * tq:(g + 1) * tq], sel_bias, pad_rows], axis=0)

    chains = [(hk, g) for hk in range(NSA_KV_HEADS) for g in range(G)]

    def softmax_pv(s_list, vt_of, carry=None):
        m_blk = [jnp.max(s, axis=0, keepdims=True) for s in s_list]
        if carry is None:
            m_new = m_blk
        else:
            m_new = [jnp.maximum(c[0], mb) for c, mb in zip(carry, m_blk)]
        p = [jnp.exp2(s - mn) for s, mn in zip(s_list, m_new)]
        l_blk = [jnp.sum(x, axis=0, keepdims=True) for x in p]
        pv = [_dot(vt_of(c), x.astype(BF16)) for c, x in zip(chains, p)]
        if carry is None:
            return [(mn, lb, a) for mn, lb, a in zip(m_new, l_blk, pv)]
        alpha = [jnp.exp2(c[0] - mn) for c, mn in zip(carry, m_new)]
        return [(mn, c[1] * al + lb, c[2] * al + a)
                for c, mn, al, lb, a in zip(carry, m_new, alpha, l_blk, pv)]

    def vcols(ref, hk, j):
        return ref[0, hk * HEAD_DIM:(hk + 1) * HEAD_DIM, pl.ds(pl.multiple_of(j * tq, tq), tq)]

    wchunk = [jnp.maximum(step - (n_wchunks - 1) + w, 0) for w in range(n_wchunks)]
    kw_rows = [jnp.concatenate([kw_ref[0, hk, j] for j in wchunk], axis=0) for hk in range(NSA_KV_HEADS)]
    vw_cols = [jnp.concatenate([vcols(vwt_ref, hk, j) for j in wchunk], axis=1)
               for hk in range(NSA_KV_HEADS)]
    wbias = wbias_ref[jnp.minimum(step, n_wchunks - 1)]
    s_win = [_dot(kw_rows[hk], qaug[hk, g]) + wbias for hk, g in chains]
    win = softmax_pv(s_win, lambda c: vw_cols[c[0]])

    tri = tri_ref[...]
    s_diag = [_dot(ks_ref[0, hk, step], qaug[hk, g]) + tri for hk, g in chains]
    carry = softmax_pv(s_diag, lambda c: vcols(vst_ref, c[0], step))

    def body(j, flat):
        carry = [tuple(flat[3 * i:3 * i + 3]) for i in range(len(chains))]
        s_j = [_dot(ks_ref[0, hk, j], qaug[hk, g]) for hk, g in chains]
        new = softmax_pv(s_j, lambda c: vcols(vst_ref, c[0], j), carry)
        return tuple(x for c in new for x in c)

    flat = lax.fori_loop(0, step, body, tuple(x for c in carry for x in c))
    sel = [tuple(flat[3 * i:3 * i + 3]) for i in range(len(chains))]

    for i, (hk, g) in enumerate(chains):
        gate = lambda j: gates[(hk * G + g) * 3 + j:(hk * G + g) * 3 + j + 1, :]
        out = (gate(0) * o_cmp[hk, g] + gate(1) * (sel[i][2] / sel[i][1])
               + gate(2) * (win[i][2] / win[i][1]))
        o_ref[0, hrow(hk, g), :] = out.astype(o_ref.dtype)


def _nsa_call(qt, gt, ks5, vst, kw5, vwt, kcb, vcbt, ovt, tri, wbias, seq):
    B = qt.shape[0]
    tq = NSA_TQ
    per_b = lambda a: pl.BlockSpec((1,) + a.shape[1:], lambda b, i: (b,) + (0,) * (a.ndim - 1))
    full = lambda a: pl.BlockSpec(a.shape, lambda b, i: (0,) * a.ndim)
    return pl.pallas_call(
        functools.partial(_nsa_kernel, seq=seq),
        grid=(B, seq // tq),
        in_specs=[pl.BlockSpec((1, NSA_WIDTH, tq), lambda b, i: (b, 0, i)),
                  pl.BlockSpec((1, gt.shape[1], tq), lambda b, i: (b, 0, i)),
                  per_b(ks5), per_b(vst), per_b(kw5), per_b(vwt), per_b(kcb), per_b(vcbt),
                  full(ovt), full(tri), full(wbias)],
        out_specs=pl.BlockSpec((1, NSA_WIDTH, tq), lambda b, i: (b, 0, i)),
        out_shape=jax.ShapeDtypeStruct((B, NSA_WIDTH, seq), BF16),
        compiler_params=pltpu.CompilerParams(
            dimension_semantics=("arbitrary", "arbitrary"), vmem_limit_bytes=VMEM_LIMIT),
        name="nsa_attention",
    )(qt, gt, ks5, vst, kw5, vwt, kcb, vcbt, ovt, tri, wbias)


def _nsa_from_proj(kvc, ksa, kwa, qt, vst, vwt, gt, P):
    B, Hk, S, _ = ksa.shape
    tq = NSA_TQ
    Dh = HEAD_DIM
    n_sel = S // SEL_BLOCK
    assert S % tq == 0 and WINDOW % tq == 0 and n_sel == NSA_BIAS_ROWS
    n_half = S // CMP_STRIDE
    kv = kvc.reshape(B, n_half, CMP_STRIDE * KVC_COLS)
    w1 = P['nsa_cmp_w1'][0]
    w1h = w1.astype(BF16).reshape(2, 2, CMP_STRIDE, 1, HEAD_DIM, CMP_HIDDEN)
    zero = jnp.zeros_like(w1h[0, 0])

    def block(j, a, h):
        parts = [w1h[j, a] if hh == h else zero for hh in range(Hk)]
        return jnp.concatenate(parts, axis=1).reshape(CMP_STRIDE * NSA_KV_WIDTH, CMP_HIDDEN)

    wab = jnp.stack([jnp.concatenate([block(j, a, h) for a in range(2) for h in range(Hk)], axis=1)
                     for j in range(2)])
    pos8 = jnp.broadcast_to(P['nsa_cmp_pos'][0].reshape(2, 1, CMP_BLOCK * HEAD_DIM),
                            (2, 8, CMP_BLOCK * HEAD_DIM)).astype(BF16)
    w2 = P['nsa_cmp_w2'][0].astype(BF16)
    kcb, vcbt = _compress_call(kv, wab, pos8, w1.astype(BF16), w2, jnp.swapaxes(w2, 1, 2))

    ks5 = ksa.reshape(B, Hk, S // tq, tq, NSA_AUG)
    kw5 = kwa.reshape(B, Hk, S // tq, tq, NSA_AUG)

    n_cmp = (S - CMP_BLOCK) // CMP_STRIDE + 1
    cmp_start = np.arange(n_half) * CMP_STRIDE
    sel_start = np.arange(n_sel) * SEL_BLOCK
    overlap = ((cmp_start[:, None] <= sel_start[None, :] + SEL_BLOCK - 1)
               & (cmp_start[:, None] + CMP_BLOCK - 1 >= sel_start[None, :])
               & (np.arange(n_half)[:, None] < n_cmp)).astype(np.float32)
    tri = np.where(np.arange(tq)[:, None] <= np.arange(tq)[None, :], 0.0, NEG_INF).astype(np.float32)
    n_w = WINDOW // tq
    masked = np.full((tq, tq), NEG_INF, np.float32)
    clear = np.zeros((tq, tq), np.float32)
    wbias = np.stack([np.concatenate([masked] * (n_w - v) + [clear] * v + [tri], axis=0) for v in range(n_w)]
                     + [np.concatenate([NEG_INF - tri] + [clear] * (n_w - 1) + [tri], axis=0)])
    return _nsa_call(qt, gt, ks5, vst, kw5, vwt, kcb, vcbt, jnp.asarray(overlap.T, BF16),
                      jnp.asarray(tri), jnp.asarray(wbias), S)


def _merge_kernel(x_ref, ya_ref, ybt_ref, gate_ref, wa_ref, wb_ref, wo_ref, o_ref):
    D = x_ref.shape[-1]
    ta = _dot(ya_ref[...], wa_ref[...])
    tb = _dot_tn(ybt_ref[0], wb_ref[...])
    ga = _sigmoid(gate_ref[:, :D].astype(F32))
    gb = _sigmoid(gate_ref[:, D:].astype(F32))
    mix = (ga * ta + gb * tb).astype(BF16)
    o_ref[...] = x_ref[...] + _dot(mix, wo_ref[...])


def _merge_call(x2, ya2, ybt, gates, wa, wb, wo, tm=1024):
    T, D = x2.shape
    tiles_per_seq = ybt.shape[2] // tm
    row = lambda w: pl.BlockSpec((tm, w), lambda i: (i, 0))
    full = lambda a: pl.BlockSpec(a.shape, lambda i: (0,) * a.ndim)
    ybt_spec = pl.BlockSpec((1, ybt.shape[1], tm), lambda i: (i // tiles_per_seq, 0, i % tiles_per_seq))
    return pl.pallas_call(
        _merge_kernel,
        grid=(T // tm,),
        in_specs=[row(D), row(ya2.shape[1]), ybt_spec, row(gates.shape[1]),
                  full(wa), full(wb), full(wo)],
        out_specs=row(D),
        out_shape=jax.ShapeDtypeStruct((T, D), F32),
        compiler_params=pltpu.CompilerParams(
            dimension_semantics=("arbitrary",), vmem_limit_bytes=VMEM_LIMIT),
        name="merge",
    )(x2, ya2, ybt, gates, wa, wb, wo)


FFN_HALO = 8


def _rms(x, g):
    return x * lax.rsqrt(jnp.mean(x * x, axis=-1, keepdims=True) + NORM_EPS) * g


def _ffn_kernel(h_ref, halo_ref, p_ref, ln_ref, wup_ref, cw_ref, cb_ref, wdn_ref, wpg_ref, wpp_ref,
                o_ref, up0a_ref, up0b_ref, up1a_ref, up1b_ref, act_ref, *, tiles_per_seq, fc):
    up_refs = ((up0a_ref, up0b_ref), (up1a_ref, up1b_ref))
    tm = act_ref.shape[0]
    D = halo_ref.shape[1]
    V = tm // 8
    d_ff = wdn_ref.shape[0]
    slabs = lambda ref, w: jnp.swapaxes(ref[0], 0, 1).reshape(tm, w)
    h = slabs(h_ref, D)
    first = (pl.program_id(0) % tiles_per_seq) == 0
    halo = jnp.where(first, 0.0, halo_ref[...])
    ln2, ln3, lnf = ln_ref[0:1, :], ln_ref[1:2, :], ln_ref[2:3, :]
    u = jnp.concatenate([_rms(halo, ln2), _rms(h, ln2)], axis=0).astype(BF16)

    n_chunks = d_ff // fc
    sub = lax.broadcasted_iota(jnp.int32, (8, 1), 0)

    def project(c):
        for half in range(2):
            col = half * d_ff + c * fc
            up_refs[c % 2][half][...] = _dot(u, wup_ref[:, col:col + fc])

    def conv(c, half):
        ref = up_refs[c % 2][half]
        col = half * d_ff + c * fc
        halo_up = ref[0:FFN_HALO, :]
        last = lambda k: ref[FFN_HALO + tm - 8 * k:FFN_HALO + tm - 8 * (k - 1), :]
        wrap1 = pltpu.roll(jnp.where(sub == 7, halo_up, last(1)), 1, axis=0)
        wrap2 = pltpu.roll(jnp.where(sub == 7, pltpu.roll(halo_up, 1, axis=0), last(2)), 1, axis=0)
        x0 = ref[FFN_HALO:FFN_HALO + tm, :]
        x1 = jnp.concatenate([wrap1, ref[FFN_HALO:FFN_HALO + tm - 8, :]], axis=0)
        x2 = jnp.concatenate([wrap2, wrap1, ref[FFN_HALO:FFN_HALO + tm - 16, :]], axis=0)
        tap = lambda j: cw_ref[j:j + 1, col:col + fc]
        return cb_ref[:, col:col + fc] + tap(0) * x2 + tap(1) * x1 + tap(2) * x0

    project(0)
    for c in range(n_chunks):
        if c + 1 < n_chunks:
            project(c + 1)
        a = conv(c, 0)
        b = conv(c, 1)
        act_ref[:, c * fc:(c + 1) * fc] = (a * _sigmoid(a) * b).astype(BF16)
    h2 = h + _dot(act_ref[...], wdn_ref[...])
    gate = _sigmoid(_dot(_rms(h2, ln3).astype(BF16), wpg_ref[...]))
    h3 = h2 + gate * _dot(slabs(p_ref, p_ref.shape[3]).astype(BF16), wpp_ref[...])
    o_ref[0] = jnp.swapaxes(_rms(h3, lnf).reshape(V, 8, D), 0, 1)


FFN_TM = 512
FFN_FC = 256


def _ffn_call(h2d, p2d, lns, wup, cw, cb, wdn, wpg, wpp, seq):
    T, D = h2d.shape
    tm, fc = FFN_TM, FFN_FC
    assert CONV_WIDTH == 3 and seq % tm == 0 and wdn.shape[0] % fc == 0
    tiles_per_seq = seq // tm
    runs = lambda x: x.reshape(T // tm, 8, tm // 8, x.shape[1])
    run_spec = lambda w: pl.BlockSpec((1, 8, tm // 8, w), lambda i: (i, 0, 0, 0))
    full = lambda a: pl.BlockSpec(a.shape, lambda i: (0,) * a.ndim, pipeline_mode=pl.Buffered(1))
    halo = pl.BlockSpec((FFN_HALO, D), lambda i: (jnp.maximum(i * (tm // FFN_HALO) - 1, 0), 0))
    out = pl.pallas_call(
        functools.partial(_ffn_kernel, tiles_per_seq=tiles_per_seq, fc=fc),
        grid=(T // tm,),
        in_specs=[run_spec(D), halo, run_spec(p2d.shape[1]), full(lns), full(wup), full(cw), full(cb),
                  full(wdn), full(wpg), full(wpp)],
        out_specs=run_spec(D),
        out_shape=jax.ShapeDtypeStruct((T // tm, 8, tm // 8, D), F32),
        scratch_shapes=[pltpu.VMEM((FFN_HALO + tm, fc), F32)] * 4 + [pltpu.VMEM((tm, wdn.shape[0]), BF16)],
        compiler_params=pltpu.CompilerParams(
            dimension_semantics=("arbitrary",), vmem_limit_bytes=VMEM_LIMIT),
        name="ffn",
    )(runs(h2d), h2d, runs(p2d), lns, wup, cw, cb, wdn, wpg, wpp)
    return out.reshape(T, D)


def _prep_proj_weights(w_in, mu_wag, w1, a1, g1):
    D = w_in.shape[0]
    sizes = (RW_WIDTH, RW_WIDTH, RW_WIDTH, NSA_WIDTH) + (NSA_KV_WIDTH,) * 6 + (3 * NSA_Q_HEADS, D, D)
    offs = np.concatenate([[0], np.cumsum(sizes)])
    part = lambda i, j: w_in[:, offs[i]:offs[j]]
    mw, ma, mg = mu_wag[0][:, None], mu_wag[1][:, None], mu_wag[2][:, None]
    zg = jnp.zeros((D, RW_GATE_PAD - RW_GATE_LORA), F32)
    rw = jnp.concatenate([
        part(0, 3),
        (1.0 - mw) * w1, (1.0 - ma) * a1,
        mw * w1, ma * a1,
        (1.0 - mg) * g1, zg,
        mg * g1, zg], axis=1)
    def widen(w):
        w = w.reshape(D, NSA_KV_HEADS, HEAD_DIM)
        return jnp.concatenate([w, jnp.zeros_like(w)], axis=2).reshape(D, KEY_COLS)

    w_rows = jnp.concatenate([rw, part(4, 6), widen(part(6, 7)), widen(part(8, 9)), part(11, 13)], axis=1)
    w_cols = jnp.concatenate([part(3, 4), part(7, 8), part(9, 10), part(10, 11),
                              jnp.zeros((D, NSA_GATE_ROWS - 3 * NSA_Q_HEADS), F32)], axis=1)
    return w_rows.astype(BF16), w_cols.T.astype(BF16)


def _prep_rwkv_weights(w2, a2, g2):
    z = jnp.zeros_like(w2)
    w2a2 = jnp.concatenate([jnp.concatenate([w2, z], axis=1),
                            jnp.concatenate([z, a2], axis=1)], axis=0).astype(BF16)
    g2p = jnp.concatenate([g2, jnp.zeros((RW_GATE_PAD - RW_GATE_LORA, RW_WIDTH), F32)],
                          axis=0).astype(BF16)
    return w2a2, g2p


def _rwkv_from_proj(rw3, P):
    w2a2, g2p = _prep_rwkv_weights(P['rw_w2'][0], P['rw_a2'][0], P['rw_g2'][0])
    vecs = jnp.stack([P['rw_w0'][0], P['rw_a0'][0], P['rw_k_k'][0], P['rw_k_a'][0],
                      P['rw_r_k'][0].reshape(-1), P['rw_lnx_g'][0], P['rw_lnx_b'][0],
                      jnp.zeros((RW_WIDTH,), F32)], axis=0)
    return _rwkv_call(rw3, P['rw_mu_rkv'][0], vecs, w2a2, g2p)


def kernel(x, p, ln1_g, w_in, rw_mu_rkv, rw_mu_wag, rw_w0, rw_w1, rw_w2, rw_a0, rw_a1, rw_a2, rw_g1, rw_g2, rw_k_k, rw_k_a, rw_r_k, rw_lnx_g, rw_lnx_b, nsa_cmp_pos, nsa_cmp_w1, nsa_cmp_w2, w_out_a, w_out_b, w_out, ln2_g, w_up, conv_w, conv_b, w_down, ln3_g, w_ple_gate, w_ple_proj, ln_f_g):
    B, S, D = x.shape
    T = B * S
    assert w_in.shape[0] == 1, "single-layer block"
    P = dict(rw_mu_rkv=rw_mu_rkv, rw_w0=rw_w0, rw_w2=rw_w2, rw_a0=rw_a0, rw_a2=rw_a2, rw_g2=rw_g2,
             rw_k_k=rw_k_k, rw_k_a=rw_k_a, rw_r_k=rw_r_k, rw_lnx_g=rw_lnx_g, rw_lnx_b=rw_lnx_b,
             nsa_cmp_pos=nsa_cmp_pos, nsa_cmp_w1=nsa_cmp_w1, nsa_cmp_w2=nsa_cmp_w2)
    h = x.reshape(T, D)
    w_rows, w_cols = _prep_proj_weights(w_in[0], rw_mu_wag[0], rw_w1[0], rw_a1[0], rw_g1[0])
    rw, kvc, ksa, kwa, gates, qt, vst, vwt, gt = _proj_call(h, ln1_g[0][None], w_rows, w_cols, S)
    ya = _rwkv_from_proj(rw.reshape(B, S, RW_COLS), P)
    ybt = _nsa_from_proj(kvc, ksa, kwa, qt, vst, vwt, gt, P)
    h1 = _merge_call(h, ya.reshape(T, RW_WIDTH), ybt, gates,
                     w_out_a[0].astype(BF16), w_out_b[0].astype(BF16), w_out[0].astype(BF16))
    lns = jnp.stack([ln2_g[0], ln3_g[0], ln_f_g], axis=0)
    out = _ffn_call(h1, p[0].reshape(T, -1), lns, w_up[0].astype(BF16), conv_w[0], conv_b[0][None],
                    w_down[0].astype(BF16), w_ple_gate[0].astype(BF16), w_ple_proj[0].astype(BF16), S)
    return out.reshape(B, S, D)
```

```python
import functools

import numpy as np
import jax
import jax.numpy as jnp
from jax import lax
from jax.experimental import pallas as pl
from jax.experimental.pallas import tpu as pltpu

F32 = jnp.float32
BF16 = jnp.bfloat16

HEAD_DIM = 64
NORM_EPS = 1e-6
NEG_INF = -1e30

RW_HEADS = 8
RW_WIDTH = RW_HEADS * HEAD_DIM
RW_DECAY_LORA = 64
RW_AAA_LORA = 64
RW_GATE_LORA = 160
RW_LNX_EPS = 64e-5
RW_CHUNK = 64
RW_GROUP = 4
RW_GROUP_W = RW_GROUP * HEAD_DIM
RW_GATE_PAD = 256

NSA_Q_HEADS = 8
NSA_KV_HEADS = 2
NSA_GROUP = NSA_Q_HEADS // NSA_KV_HEADS
NSA_WIDTH = NSA_Q_HEADS * HEAD_DIM
NSA_KV_WIDTH = NSA_KV_HEADS * HEAD_DIM
CMP_BLOCK = 32
CMP_STRIDE = 16
CMP_HIDDEN = 128
SEL_BLOCK = 64
SEL_TOP = 16
SEL_FORCE_SCORE = 1e4
WINDOW = 512

CONV_WIDTH = 3

RW_COLS = 3 * RW_WIDTH + 2 * 128 + 2 * RW_GATE_PAD
GATE_COLS = 2 * 1024

V7X_VMEM_BYTES = 64 * 1024 * 1024
VMEM_LIMIT = V7X_VMEM_BYTES - 8 * 1024 * 1024


def _dot(a, b):
    return jnp.dot(a, b, preferred_element_type=F32)


def _dot_nt(a, b):
    return lax.dot_general(a, b, (((1,), (1,)), ((), ())), preferred_element_type=F32)


def _dot_tn(a, b):
    return lax.dot_general(a, b, (((0,), (0,)), ((), ())), preferred_element_type=F32)


def _split2(x):
    hi = x.astype(BF16)
    lo = (x - hi.astype(F32)).astype(BF16)
    return hi, lo


def _sigmoid(x):
    return 1.0 / (1.0 + jnp.exp(-x))


LOG2_E = 1.4426950408889634
DECAY_SCALE_LOG2 = 0.6065306597126334 * LOG2_E


PROJ_TM = 1024
PROJ_CHUNK = 768
KVC_COLS = 2 * NSA_KV_WIDTH
KEY_COLS = NSA_KV_HEADS * 128
NSA_GATE_ROWS = 32
T_ROWS = NSA_WIDTH + 2 * NSA_KV_WIDTH + NSA_GATE_ROWS


def _proj_kernel(x_ref, g_ref, w_ref, wt_ref, rw_ref, kvc_ref, ksa_ref, kwa_ref, gate_ref,
                 qt_ref, vst_ref, vwt_ref, gt_ref, *, tiles_per_seq):
    tm = x_ref.shape[0]
    x = x_ref[...]
    ms = jnp.mean(x * x, axis=-1, keepdims=True)
    u = (x * lax.rsqrt(ms + NORM_EPS) * g_ref[...]).astype(BF16)

    col = 0
    for o_ref in (rw_ref, kvc_ref):
        width = o_ref.shape[-1]
        for c in range(0, width, PROJ_CHUNK):
            hi = min(c + PROJ_CHUNK, width)
            o_ref[:, c:hi] = _dot(u, w_ref[:, col + c:col + hi]).astype(o_ref.dtype)
        col += width

    s0 = (pl.program_id(0) % tiles_per_seq) * tm
    blk = (s0 + lax.broadcasted_iota(jnp.int32, (tm, KEY_COLS), 0)) // SEL_BLOCK
    lane = lax.broadcasted_iota(jnp.int32, (tm, KEY_COLS), 1) % 128
    onehot = jnp.where(lane - HEAD_DIM == blk, 1.0, 0.0)
    ks = _dot(u, w_ref[:, col:col + KEY_COLS]) + onehot
    kw = _dot(u, w_ref[:, col + KEY_COLS:col + 2 * KEY_COLS])
    for hk in range(NSA_KV_HEADS):
        ksa_ref[0, hk] = ks[:, hk * 128:(hk + 1) * 128].astype(ksa_ref.dtype)
        kwa_ref[0, hk] = kw[:, hk * 128:(hk + 1) * 128].astype(kwa_ref.dtype)
    col += 2 * KEY_COLS

    width = gate_ref.shape[-1]
    for c in range(0, width, PROJ_CHUNK):
        hi = min(c + PROJ_CHUNK, width)
        gate_ref[:, c:hi] = _dot(u, w_ref[:, col + c:col + hi]).astype(gate_ref.dtype)

    t = _dot_nt(wt_ref[...], u)
    row = 0
    for o_ref in (qt_ref, vst_ref, vwt_ref, gt_ref):
        n = o_ref.shape[1]
        o_ref[0] = t[row:row + n].astype(o_ref.dtype)
        row += n


def _proj_call(x2, g, w_all, wt_all, seq):
    T, D = x2.shape
    tm = PROJ_TM
    B = T // seq
    tps = seq // tm
    rows = lambda w: pl.BlockSpec((tm, w), lambda i: (i, 0))
    full = lambda a: pl.BlockSpec(a.shape, lambda i: (0,) * a.ndim, pipeline_mode=pl.Buffered(1))
    keys = pl.BlockSpec((1, NSA_KV_HEADS, tm, 128), lambda i: (i // tps, 0, i % tps, 0))
    tcols = lambda n: pl.BlockSpec((1, n, tm), lambda i: (i // tps, 0, i % tps))
    sds = jax.ShapeDtypeStruct
    return pl.pallas_call(
        functools.partial(_proj_kernel, tiles_per_seq=tps),
        grid=(T // tm,),
        in_specs=[rows(D), full(g), full(w_all), full(wt_all)],
        out_specs=[rows(RW_COLS), rows(KVC_COLS), keys, keys, rows(GATE_COLS),
                   tcols(NSA_WIDTH), tcols(NSA_KV_WIDTH), tcols(NSA_KV_WIDTH), tcols(NSA_GATE_ROWS)],
        out_shape=[sds((T, RW_COLS), BF16), sds((T, KVC_COLS), BF16),
                   sds((B, NSA_KV_HEADS, seq, 128), BF16), sds((B, NSA_KV_HEADS, seq, 128), BF16),
                   sds((T, GATE_COLS), BF16),
                   sds((B, NSA_WIDTH, seq), BF16), sds((B, NSA_KV_WIDTH, seq), BF16),
                   sds((B, NSA_KV_WIDTH, seq), BF16), sds((B, NSA_GATE_ROWS, seq), BF16)],
        compiler_params=pltpu.CompilerParams(
            dimension_semantics=("arbitrary",), vmem_limit_bytes=VMEM_LIMIT),
        name="proj",
    )(x2, g, w_all, wt_all)


def _rwkv_kernel(x_ref, mu_ref, vec_ref, w2a2_ref, g2_ref, o_ref, state_ref, prev_ref):
    C = RW_CHUNK
    GW = RW_GROUP_W
    W = RW_WIDTH
    NB = x_ref.shape[0]
    R = NB * C
    t_idx = pl.program_id(1)
    ops = {}

    @pl.when(t_idx == 0)
    def _():
        state_ref[...] = jnp.zeros_like(state_ref)
        prev_ref[...] = jnp.zeros_like(prev_ref)

    mu = mu_ref[...]
    w0, a0, k_k, k_a, r_k, lnx_g, lnx_b = (vec_ref[i:i + 1, :] for i in range(7))

    gr = lax.broadcasted_iota(jnp.int32, (GW, GW), 0) // HEAD_DIM
    gc = lax.broadcasted_iota(jnp.int32, (GW, GW), 1) // HEAD_DIM
    blk = gr == gc
    ones_bd = jnp.where(blk, 1.0, 0.0).astype(BF16)

    def headsums(zs):
        parts = []
        for z in zs:
            zb = z.astype(BF16)
            parts += [zb[:, :GW], zb[:, GW:]]
        s = _dot(jnp.concatenate(parts, axis=0), ones_bd)
        return [jnp.concatenate([s[2 * R * i:2 * R * i + R], s[2 * R * i + R:2 * R * (i + 1)]], axis=1)
                for i in range(len(zs))]

    t_n = lax.broadcasted_iota(jnp.int32, (C, GW), 0)
    s_n = lax.broadcasted_iota(jnp.int32, (C, GW), 1) % HEAD_DIM
    strict = t_n > s_n
    incl = t_n >= s_n
    eye_n = jnp.where(t_n == s_n, 1.0, 0.0)

    def bd(z):
        z4 = jnp.concatenate([z.astype(F32)] * RW_GROUP, axis=0)
        return jnp.where(blk, z4, 0.0).astype(BF16)

    def prepare():
        x = x_ref[...].reshape(R, RW_COLS).astype(F32)
        rolled = pltpu.roll(x, 1, axis=0)
        row8 = lax.broadcasted_iota(jnp.int32, (8, 1), 0)
        pieces = []
        for bi in range(NB):
            pieces.append(jnp.where(row8 == 0, prev_ref[bi, 0:1, :], rolled[bi * C:bi * C + 8]))
            pieces.append(rolled[bi * C + 8:(bi + 1) * C])
            prev_ref[bi, 0:1, :] = x[(bi + 1) * C - 1:(bi + 1) * C, :]
        xs = jnp.concatenate(pieces, axis=0)

        def lerp(j):
            cur = x[:, j * W:(j + 1) * W]
            return cur + (xs[:, j * W:(j + 1) * W] - cur) * mu[j:j + 1, :]

        r, k, v = lerp(0), lerp(1), lerp(2)
        o = 3 * W
        pre_a = x[:, o:o + 128] + xs[:, o + 128:o + 256]
        lane = lax.broadcasted_iota(jnp.int32, (R, 128), 1)
        h_a = jnp.where(lane < RW_DECAY_LORA, jnp.tanh(pre_a), pre_a)
        lwa = _dot(h_a.astype(BF16), w2a2_ref[...])
        o += 256
        pre_g = x[:, o:o + RW_GATE_PAD] + xs[:, o + RW_GATE_PAD:o + 2 * RW_GATE_PAD]
        g = _dot(_sigmoid(pre_g).astype(BF16), g2_ref[...])
        ld = (-DECAY_SCALE_LOG2) * _sigmoid(w0 + lwa[:, :W])
        a = _sigmoid(a0 + lwa[:, W:])
        kkr = k * k_k
        k2 = k * (1.0 + (a - 1.0) * k_a)
        kk_ss, bonus = headsums([kkr * kkr, r * k2 * r_k])
        kk = kkr / jnp.maximum(jnp.sqrt(kk_ss), 1e-12)
        b = kk * a
        tr = lax.broadcasted_iota(jnp.int32, (R, R), 0)
        tc = lax.broadcasted_iota(jnp.int32, (R, R), 1)
        tri = jnp.where((tr >= tc) & (tr // C == tc // C), 1.0, 0.0).astype(BF16)
        l_inc = _dot(tri, jnp.concatenate(_split2(ld), axis=1))
        l_inc = l_inc[:, :W] + l_inc[:, W:]
        l_end = [l_inc[(bi + 1) * C - 1:(bi + 1) * C] for bi in range(NB)]
        e_neg = jnp.exp2(-l_inc)
        e_tail = jnp.concatenate([jnp.exp2(l_end[bi] - l_inc[bi * C:(bi + 1) * C]) for bi in range(NB)],
                                 axis=0)
        new_opb = (-kk * jnp.exp2(l_inc - ld), r * jnp.exp2(l_inc), b * e_neg, k2 * e_neg,
                   b * e_tail, k2 * e_tail)
        cast = (True, True, False, False, True, True)
        ops['matmul'] = tuple(z.astype(BF16) if c else z for z, c in zip(new_opb, cast))
        ops['output'] = (v, bonus, g)
        ops['decay'] = jnp.concatenate([jnp.broadcast_to(jnp.exp2(le), (8, W)) for le in l_end], axis=0)

    def consume():
        n_grp = W // GW
        chains = [(bi, gi) for bi in range(NB) for gi in range(n_grp)]
        cut = lambda z, c: z[c[0] * C:(c[0] + 1) * C, c[1] * GW:(c[1] + 1) * GW]
        each = lambda f, *lists: [f(*args) for args in zip(*lists)]

        p_a_hat, p_r_hat, p_b_hat, p_k_hat, p_b_tail, p_k_tail = ops['matmul']
        p_v, p_bonus, p_g = ops['output']
        p_elc = ops['decay']
        a_h = [cut(p_a_hat, c) for c in chains]
        r_h = [cut(p_r_hat, c) for c in chains]
        ar = each(lambda x1, x2: jnp.concatenate([x1, x2], axis=0), a_h, r_h)
        m1 = each(_dot_nt, ar, [bd(cut(p_b_hat, c)) for c in chains])
        m2 = each(_dot_nt, ar, [bd(cut(p_k_hat, c)) for c in chains])
        m_ab = [jnp.where(strict, m[:C], 0.0) for m in m1]
        m_rb = [jnp.where(incl, m[C:], 0.0) for m in m1]
        m_ak = [jnp.where(strict, m[:C], 0.0) for m in m2]
        m_rk = [jnp.where(incl, m[C:], 0.0) for m in m2]

        tinv = [eye_n + m for m in m_ab]
        p = each(lambda m: _dot(m.astype(BF16), bd(m)), m_ab)
        power = 2
        while 2 * power < C:
            tp = each(lambda t, q: _dot(jnp.concatenate([t, q], axis=0).astype(BF16), bd(q)), tinv, p)
            tinv = each(lambda t, x1: t + x1[:C], tinv, tp)
            p = [x1[C:] for x1 in tp]
            power *= 2
        tinv = each(lambda t, q: t + _dot(t.astype(BF16), bd(q)), tinv, p)

        s_old = [state_ref[i * GW:(i + 1) * GW, :] for i in range(len(chains))]
        s_bf = [s.astype(BF16) for s in s_old]
        vg = [cut(p_v, c) for c in chains]
        bd_v = [bd(x1) for x1 in vg]
        xz = each(lambda x1, s, m, bv: _dot_nt(x1, s) + _dot(m.astype(BF16), bv), a_h, s_bf, m_ak, bd_v)
        u = each(lambda t, x1: _dot(t.astype(BF16), bd(x1)), tinv, xz)
        y = each(lambda x1, s, mb, mk, uu, bv:
                 _dot_nt(x1, s) + _dot(jnp.concatenate([mb, mk], axis=1).astype(BF16),
                                       jnp.concatenate([bd(uu), bv], axis=0)),
                 r_h, s_bf, m_rb, m_rk, u, bd_v)
        new_states = []
        for c, uu, vv, s in zip(chains, u, vg, s_old):
            upd = _dot_tn(jnp.concatenate([uu, vv], axis=0).astype(BF16),
                          jnp.concatenate([cut(p_b_tail, c), cut(p_k_tail, c)], axis=0))
            decay = p_elc[c[0] * 8:c[0] * 8 + 1, c[1] * GW:(c[1] + 1) * GW]
            new_states.append(s * decay + jnp.where(blk, upd, 0.0))
        state_ref[...] = jnp.concatenate(new_states, axis=0)
        y_rows = [jnp.concatenate(y[bi * n_grp:(bi + 1) * n_grp], axis=1) for bi in range(NB)]
        y = jnp.concatenate(y_rows, axis=0)
        mean = headsums([y])[0] * (1.0 / HEAD_DIM)
        yc = y - mean
        var = headsums([yc * yc])[0] * (1.0 / HEAD_DIM)
        yn = yc * lax.rsqrt(var + RW_LNX_EPS) * lnx_g + lnx_b
        yn = yn + p_bonus * p_v
        o_ref[...] = (yn * p_g).reshape(NB, C, W).astype(o_ref.dtype)

    prepare()
    consume()


RW_SEQS_PER_STEP = 8


def _rwkv_call(rw3, mu, vecs, w2a2, g2p):
    B, S, _ = rw3.shape
    C = RW_CHUNK
    nb = RW_SEQS_PER_STEP if B % RW_SEQS_PER_STEP == 0 else 1
    n_groups = RW_WIDTH // RW_GROUP_W
    return pl.pallas_call(
        _rwkv_kernel,
        grid=(B // nb, S // C),
        in_specs=[
            pl.BlockSpec((nb, C, RW_COLS), lambda b, t: (b, t, 0)),
            pl.BlockSpec(mu.shape, lambda b, t: (0, 0)),
            pl.BlockSpec(vecs.shape, lambda b, t: (0, 0)),
            pl.BlockSpec(w2a2.shape, lambda b, t: (0, 0)),
            pl.BlockSpec(g2p.shape, lambda b, t: (0, 0)),
        ],
        out_specs=pl.BlockSpec((nb, C, RW_WIDTH), lambda b, t: (b, t, 0)),
        out_shape=jax.ShapeDtypeStruct((B, S, RW_WIDTH), BF16),
        scratch_shapes=[
            pltpu.VMEM((nb * n_groups * RW_GROUP_W, RW_GROUP_W), F32),
            pltpu.VMEM((nb, 8, RW_COLS), F32),
        ],
        compiler_params=pltpu.CompilerParams(
            dimension_semantics=("arbitrary", "arbitrary"), vmem_limit_bytes=VMEM_LIMIT),
        name="rwkv",
    )(rw3, mu, vecs, w2a2, g2p)


def _compress_kernel(kv_ref, wab_ref, pos_ref, w1_ref, w2_ref, w2t_ref, kcb_ref, vcbt_ref):
    n_half = kv_ref.shape[1]
    for j in range(2):
        src = jnp.concatenate(
            [kv_ref[0, :, l * KVC_COLS + j * NSA_KV_WIDTH:l * KVC_COLS + (j + 1) * NSA_KV_WIDTH]
             for l in range(CMP_STRIDE)], axis=1)
        pab = _dot(src, wab_ref[j])
        half = NSA_KV_HEADS * CMP_HIDDEN
        pa, pb = pab[:, :half], pab[:, half:]
        pb = pltpu.roll(pb, n_half - 1, axis=0)
        pos_term = _dot(pos_ref[j], w1_ref[j])[0:1]
        hid = pa + pb + jnp.concatenate([pos_term] * NSA_KV_HEADS, axis=1)
        act = (hid * _sigmoid(hid)).astype(BF16)
        for hk in range(NSA_KV_HEADS):
            a_h = act[:, hk * CMP_HIDDEN:(hk + 1) * CMP_HIDDEN]
            if j == 0:
                kcb_ref[0, hk] = _dot(a_h, w2_ref[j]).astype(kcb_ref.dtype)
            else:
                vcbt_ref[0, hk] = _dot_nt(w2t_ref[j], a_h).astype(vcbt_ref.dtype)


def _compress_call(kv, wab, pos8, w1, w2, w2t):
    B, n_half, width = kv.shape
    full = lambda a: pl.BlockSpec(a.shape, lambda b: (0,) * a.ndim)
    return pl.pallas_call(
        _compress_kernel,
        grid=(B,),
        in_specs=[pl.BlockSpec((1, n_half, width), lambda b: (b, 0, 0)),
                  full(wab), full(pos8), full(w1), full(w2), full(w2t)],
        out_specs=[pl.BlockSpec((1, NSA_KV_HEADS, n_half, HEAD_DIM), lambda b: (b, 0, 0, 0)),
                   pl.BlockSpec((1, NSA_KV_HEADS, HEAD_DIM, n_half), lambda b: (b, 0, 0, 0))],
        out_shape=[jax.ShapeDtypeStruct((B, NSA_KV_HEADS, n_half, HEAD_DIM), BF16),
                   jax.ShapeDtypeStruct((B, NSA_KV_HEADS, HEAD_DIM, n_half), BF16)],
        compiler_params=pltpu.CompilerParams(
            dimension_semantics=("arbitrary",), vmem_limit_bytes=VMEM_LIMIT),
        name="nsa_compress",
    )(kv, wab, pos8, w1, w2, w2t)


NSA_TQ = 256
NSA_AUG = 128
NSA_BIAS_ROWS = 32


def _nsa_kernel(qt_ref, gt_ref, ks_ref, vst_ref, kw_ref, vwt_ref, kcb_ref, vcbt_ref, ovt_ref,
                 tri_ref, wbias_ref, o_ref, *, seq):
    tq = NSA_TQ
    G = NSA_GROUP
    R = G * tq
    n_half = kcb_ref.shape[2]
    n_cmp = n_half - 1
    n_sel = seq // SEL_BLOCK
    n_top = min(SEL_TOP, n_sel)
    n_wchunks = WINDOW // tq + 1
    step = pl.program_id(1)
    q0 = step * tq

    t_lane = q0 + lax.broadcasted_iota(jnp.int32, (1, R), 1) % tq
    gates = _sigmoid(gt_ref[0].astype(F32))
    pad_rows = jnp.zeros((NSA_AUG - HEAD_DIM - NSA_BIAS_ROWS, tq), BF16)

    hrow = lambda hk, g: slice((hk * G + g) * HEAD_DIM, (hk * G + g + 1) * HEAD_DIM)
    o_cmp, qaug = {}, {}
    for hk in range(NSA_KV_HEADS):
        q64 = jnp.concatenate([qt_ref[0, hrow(hk, g), :] for g in range(G)], axis=1)
        q64 = q64 * jnp.asarray(HEAD_DIM ** -0.5, BF16)

        cidx = lax.broadcasted_iota(jnp.int32, (n_half, R), 0)
        cvalid = (cidx * CMP_STRIDE + (CMP_BLOCK - 1) <= t_lane) & (cidx < n_cmp)
        s = jnp.where(cvalid, _dot(kcb_ref[0, hk], q64), NEG_INF)
        m = jnp.max(s, axis=0, keepdims=True)
        e = jnp.where(cvalid, jnp.exp(s - m), 0.0)
        l = jnp.sum(e, axis=0, keepdims=True)
        p_c = e / jnp.where(l > 0.0, l, 1.0)
        o_c = _dot(vcbt_ref[0, hk], p_c.astype(BF16))

        psum = p_c[:, 0:tq]
        for g in range(1, G):
            psum = psum + p_c[:, g * tq:(g + 1) * tq]
        hi, lo = _split2(psum)
        imp2 = _dot(ovt_ref[...], jnp.concatenate([hi, lo], axis=1))
        imp = imp2[:, :tq] + imp2[:, tq:]
        jblk = lax.broadcasted_iota(jnp.int32, (n_sel, tq), 0)
        cur = (q0 + lax.broadcasted_iota(jnp.int32, (n_sel, tq), 1)) // SEL_BLOCK
        forced = (jblk == 0) | (jblk == cur) | (jblk == cur - 1)
        score = jnp.where(forced, SEL_FORCE_SCORE, jnp.where(jblk <= cur, imp, -1.0))
        n_grp = n_sel // 8
        rows = [score[8 * q:8 * q + 8] for q in range(n_grp)]
        ranks = [jnp.zeros((8, tq), F32)] * n_grp
        sub8 = lax.broadcasted_iota(jnp.int32, (8, tq), 0)
        for j in range(n_sel):
            sj = score[j:j + 1, :]
            for q in range(n_grp):
                wins = jnp.where(sj > rows[q], 1.0, 0.0)
                wins_ties = jnp.where(sj >= rows[q], 1.0, 0.0)
                if 8 * q + 7 < j:
                    ahead = wins
                elif 8 * q > j:
                    ahead = wins_ties
                else:
                    ahead = jnp.where(sub8 > j - 8 * q, wins_ties, wins)
                ranks[q] = ranks[q] + ahead
        rank = jnp.concatenate(ranks, axis=0)
        sel_bias = jnp.where(rank < n_top, 0.0, NEG_INF).astype(BF16)
        q_l2 = (q64.astype(F32) * LOG2_E).astype(BF16)
        for g in range(G):
            o_cmp[hk, g] = o_c[:, g * tq:(g + 1) * tq]
            qaug[hk, g] = jnp.concatenate([q_l2[:, g * tq:(g + 1) * tq], sel_bias, pad_rows], axis=0)

    chains = [(hk, g) for hk in range(NSA_KV_HEADS) for g in range(G)]

    def softmax_pv(s_list, vt_of, carry=None):
        m_blk = [jnp.max(s, axis=0, keepdims=True) for s in s_list]
        if carry is None:
            m_new = m_blk
        else:
            m_new = [jnp.maximum(c[0], mb) for c, mb in zip(carry, m_blk)]
        p = [jnp.exp2(s - mn) for s, mn in zip(s_list, m_new)]
        l_blk = [jnp.sum(x, axis=0, keepdims=True) for x in p]
        pv = [_dot(vt_of(c), x.astype(BF16)) for c, x in zip(chains, p)]
        if carry is None:
            return [(mn, lb, a) for mn, lb, a in zip(m_new, l_blk, pv)]
        alpha = [jnp.exp2(c[0] - mn) for c, mn in zip(carry, m_new)]
        return [(mn, c[1] * al + lb, c[2] * al + a)
                for c, mn, al, lb, a in zip(carry, m_new, alpha, l_blk, pv)]

    def vcols(ref, hk, j):
        return ref[0, hk * HEAD_DIM:(hk + 1) * HEAD_DIM, pl.ds(pl.multiple_of(j * tq, tq), tq)]

    wchunk = [jnp.maximum(step - (n_wchunks - 1) + w, 0) for w in range(n_wchunks)]
    kw_rows = [jnp.concatenate([kw_ref[0, hk, j] for j in wchunk], axis=0) for hk in range(NSA_KV_HEADS)]
    vw_cols = [jnp.concatenate([vcols(vwt_ref, hk, j) for j in wchunk], axis=1)
               for hk in range(NSA_KV_HEADS)]
    wbias = wbias_ref[jnp.minimum(step, n_wchunks - 1)]
    s_win = [_dot(kw_rows[hk], qaug[hk, g]) + wbias for hk, g in chains]
    win = softmax_pv(s_win, lambda c: vw_cols[c[0]])

    tri = tri_ref[...]
    flatten = lambda carry: tuple(x for c in carry for x in c)
    pair_rows = lambda hk, j: ks_ref[0, hk, pl.ds(j, 2)].reshape(2 * tq, NSA_AUG)
    pair_cols = lambda hk, j: jnp.concatenate([vcols(vst_ref, hk, j), vcols(vst_ref, hk, j + 1)], axis=1)

    def diag_single():
        s = [_dot(ks_ref[0, hk, step], qaug[hk, g]) + tri for hk, g in chains]
        return flatten(softmax_pv(s, lambda c: vcols(vst_ref, c[0], step)))

    def diag_pair():
        bias = jnp.concatenate([jnp.zeros_like(tri), tri], axis=0)
        s = [_dot(pair_rows(hk, step - 1), qaug[hk, g]) + bias for hk, g in chains]
        return flatten(softmax_pv(s, lambda c: pair_cols(c[0], step - 1)))

    def body(jj, flat):
        carry = [tuple(flat[3 * i:3 * i + 3]) for i in range(len(chains))]
        s_j = [_dot(pair_rows(hk, 2 * jj), qaug[hk, g]) for hk, g in chains]
        return flatten(softmax_pv(s_j, lambda c: pair_cols(c[0], 2 * jj), carry))

    flat = lax.fori_loop(0, step // 2, body, lax.cond(step % 2 == 1, diag_pair, diag_single))
    sel = [tuple(flat[3 * i:3 * i + 3]) for i in range(len(chains))]

    for i, (hk, g) in enumerate(chains):
        gate = lambda j: gates[(hk * G + g) * 3 + j:(hk * G + g) * 3 + j + 1, :]
        out = (gate(0) * o_cmp[hk, g] + gate(1) * (sel[i][2] / sel[i][1])
               + gate(2) * (win[i][2] / win[i][1]))
        o_ref[0, hrow(hk, g), :] = out.astype(o_ref.dtype)


def _nsa_call(qt, gt, ks5, vst, kw5, vwt, kcb, vcbt, ovt, tri, wbias, seq):
    B = qt.shape[0]
    tq = NSA_TQ
    per_b = lambda a: pl.BlockSpec((1,) + a.shape[1:], lambda b, i: (b,) + (0,) * (a.ndim - 1))
    full = lambda a: pl.BlockSpec(a.shape, lambda b, i: (0,) * a.ndim)
    return pl.pallas_call(
        functools.partial(_nsa_kernel, seq=seq),
        grid=(B, seq // tq),
        in_specs=[pl.BlockSpec((1, NSA_WIDTH, tq), lambda b, i: (b, 0, i)),
                  pl.BlockSpec((1, gt.shape[1], tq), lambda b, i: (b, 0, i)),
                  per_b(ks5), per_b(vst), per_b(kw5), per_b(vwt), per_b(kcb), per_b(vcbt),
                  full(ovt), full(tri), full(wbias)],
        out_specs=pl.BlockSpec((1, NSA_WIDTH, tq), lambda b, i: (b, 0, i)),
        out_shape=jax.ShapeDtypeStruct((B, NSA_WIDTH, seq), BF16),
        compiler_params=pltpu.CompilerParams(
            dimension_semantics=("arbitrary", "arbitrary"), vmem_limit_bytes=VMEM_LIMIT),
        name="nsa_attention",
    )(qt, gt, ks5, vst, kw5, vwt, kcb, vcbt, ovt, tri, wbias)


def _nsa_from_proj(kvc, ksa, kwa, qt, vst, vwt, gt, P):
    B, Hk, S, _ = ksa.shape
    tq = NSA_TQ
    Dh = HEAD_DIM
    n_sel = S // SEL_BLOCK
    assert S % tq == 0 and WINDOW % tq == 0 and n_sel == NSA_BIAS_ROWS
    n_half = S // CMP_STRIDE
    kv = kvc.reshape(B, n_half, CMP_STRIDE * KVC_COLS)
    w1 = P['nsa_cmp_w1'][0]
    w1h = w1.astype(BF16).reshape(2, 2, CMP_STRIDE, 1, HEAD_DIM, CMP_HIDDEN)
    zero = jnp.zeros_like(w1h[0, 0])

    def block(j, a, h):
        parts = [w1h[j, a] if hh == h else zero for hh in range(Hk)]
        return jnp.concatenate(parts, axis=1).reshape(CMP_STRIDE * NSA_KV_WIDTH, CMP_HIDDEN)

    wab = jnp.stack([jnp.concatenate([block(j, a, h) for a in range(2) for h in range(Hk)], axis=1)
                     for j in range(2)])
    pos8 = jnp.broadcast_to(P['nsa_cmp_pos'][0].reshape(2, 1, CMP_BLOCK * HEAD_DIM),
                            (2, 8, CMP_BLOCK * HEAD_DIM)).astype(BF16)
    w2 = P['nsa_cmp_w2'][0].astype(BF16)
    kcb, vcbt = _compress_call(kv, wab, pos8, w1.astype(BF16), w2, jnp.swapaxes(w2, 1, 2))

    ks5 = ksa.reshape(B, Hk, S // tq, tq, NSA_AUG)
    kw5 = kwa.reshape(B, Hk, S // tq, tq, NSA_AUG)

    n_cmp = (S - CMP_BLOCK) // CMP_STRIDE + 1
    cmp_start = np.arange(n_half) * CMP_STRIDE
    sel_start = np.arange(n_sel) * SEL_BLOCK
    overlap = ((cmp_start[:, None] <= sel_start[None, :] + SEL_BLOCK - 1)
               & (cmp_start[:, None] + CMP_BLOCK - 1 >= sel_start[None, :])
               & (np.arange(n_half)[:, None] < n_cmp)).astype(np.float32)
    tri = np.where(np.arange(tq)[:, None] <= np.arange(tq)[None, :], 0.0, NEG_INF).astype(np.float32)
    n_w = WINDOW // tq
    masked = np.full((tq, tq), NEG_INF, np.float32)
    clear = np.zeros((tq, tq), np.float32)
    wbias = np.stack([np.concatenate([masked] * (n_w - v) + [clear] * v + [tri], axis=0) for v in range(n_w)]
                     + [np.concatenate([NEG_INF - tri] + [clear] * (n_w - 1) + [tri], axis=0)])
    return _nsa_call(qt, gt, ks5, vst, kw5, vwt, kcb, vcbt, jnp.asarray(overlap.T, BF16),
                      jnp.asarray(tri), jnp.asarray(wbias), S)


def _merge_kernel(x_ref, ya_ref, ybt_ref, gate_ref, wa_ref, wb_ref, wo_ref, o_ref):
    D = x_ref.shape[-1]
    ta = _dot(ya_ref[...], wa_ref[...])
    tb = _dot_tn(ybt_ref[0], wb_ref[...])
    ga = _sigmoid(gate_ref[:, :D].astype(F32))
    gb = _sigmoid(gate_ref[:, D:].astype(F32))
    mix = (ga * ta + gb * tb).astype(BF16)
    o_ref[...] = x_ref[...] + _dot(mix, wo_ref[...])


def _merge_call(x2, ya2, ybt, gates, wa, wb, wo, tm=1024):
    T, D = x2.shape
    tiles_per_seq = ybt.shape[2] // tm
    row = lambda w: pl.BlockSpec((tm, w), lambda i: (i, 0))
    full = lambda a: pl.BlockSpec(a.shape, lambda i: (0,) * a.ndim)
    ybt_spec = pl.BlockSpec((1, ybt.shape[1], tm), lambda i: (i // tiles_per_seq, 0, i % tiles_per_seq))
    return pl.pallas_call(
        _merge_kernel,
        grid=(T // tm,),
        in_specs=[row(D), row(ya2.shape[1]), ybt_spec, row(gates.shape[1]),
                  full(wa), full(wb), full(wo)],
        out_specs=row(D),
        out_shape=jax.ShapeDtypeStruct((T, D), F32),
        compiler_params=pltpu.CompilerParams(
            dimension_semantics=("arbitrary",), vmem_limit_bytes=VMEM_LIMIT),
        name="merge",
    )(x2, ya2, ybt, gates, wa, wb, wo)


FFN_HALO = 8


def _rms(x, g):
    return x * lax.rsqrt(jnp.mean(x * x, axis=-1, keepdims=True) + NORM_EPS) * g


def _ffn_kernel(h_ref, halo_ref, p_ref, ln_ref, wup_ref, cw_ref, cb_ref, wdn_ref, wpg_ref, wpp_ref,
                o_ref, up0a_ref, up0b_ref, up1a_ref, up1b_ref, act_ref, *, tiles_per_seq, fc):
    up_refs = ((up0a_ref, up0b_ref), (up1a_ref, up1b_ref))
    tm = act_ref.shape[0]
    D = halo_ref.shape[1]
    V = tm // 8
    d_ff = wdn_ref.shape[0]
    slabs = lambda ref, w: jnp.swapaxes(ref[0], 0, 1).reshape(tm, w)
    h = slabs(h_ref, D)
    first = (pl.program_id(0) % tiles_per_seq) == 0
    halo = jnp.where(first, 0.0, halo_ref[...])
    ln2, ln3, lnf = ln_ref[0:1, :], ln_ref[1:2, :], ln_ref[2:3, :]
    u = jnp.concatenate([_rms(halo, ln2), _rms(h, ln2)], axis=0).astype(BF16)

    n_chunks = d_ff // fc
    sub = lax.broadcasted_iota(jnp.int32, (8, 1), 0)

    def project(c):
        for half in range(2):
            col = half * d_ff + c * fc
            up_refs[c % 2][half][...] = _dot(u, wup_ref[:, col:col + fc])

    def conv(c, half):
        ref = up_refs[c % 2][half]
        col = half * d_ff + c * fc
        halo_up = ref[0:FFN_HALO, :]
        last = lambda k: ref[FFN_HALO + tm - 8 * k:FFN_HALO + tm - 8 * (k - 1), :]
        wrap1 = pltpu.roll(jnp.where(sub == 7, halo_up, last(1)), 1, axis=0)
        wrap2 = pltpu.roll(jnp.where(sub == 7, pltpu.roll(halo_up, 1, axis=0), last(2)), 1, axis=0)
        x0 = ref[FFN_HALO:FFN_HALO + tm, :]
        x1 = jnp.concatenate([wrap1, ref[FFN_HALO:FFN_HALO + tm - 8, :]], axis=0)
        x2 = jnp.concatenate([wrap2, wrap1, ref[FFN_HALO:FFN_HALO + tm - 16, :]], axis=0)
        tap = lambda j: cw_ref[j:j + 1, col:col + fc]
        return cb_ref[:, col:col + fc] + tap(0) * x2 + tap(1) * x1 + tap(2) * x0

    project(0)
    for c in range(n_chunks):
        if c + 1 < n_chunks:
            project(c + 1)
        a = conv(c, 0)
        b = conv(c, 1)
        act_ref[:, c * fc:(c + 1) * fc] = (a * _sigmoid(a) * b).astype(BF16)
    h2 = h + _dot(act_ref[...], wdn_ref[...])
    gate = _sigmoid(_dot(_rms(h2, ln3).astype(BF16), wpg_ref[...]))
    h3 = h2 + gate * _dot(slabs(p_ref, p_ref.shape[3]).astype(BF16), wpp_ref[...])
    o_ref[0] = jnp.swapaxes(_rms(h3, lnf).reshape(V, 8, D), 0, 1)


FFN_TM = 512
FFN_FC = 256


def _ffn_call(h2d, p2d, lns, wup, cw, cb, wdn, wpg, wpp, seq):
    T, D = h2d.shape
    tm, fc = FFN_TM, FFN_FC
    assert CONV_WIDTH == 3 and seq % tm == 0 and wdn.shape[0] % fc == 0
    tiles_per_seq = seq // tm
    runs = lambda x: x.reshape(T // tm, 8, tm // 8, x.shape[1])
    run_spec = lambda w: pl.BlockSpec((1, 8, tm // 8, w), lambda i: (i, 0, 0, 0))
    full = lambda a: pl.BlockSpec(a.shape, lambda i: (0,) * a.ndim, pipeline_mode=pl.Buffered(1))
    halo = pl.BlockSpec((FFN_HALO, D), lambda i: (jnp.maximum(i * (tm // FFN_HALO) - 1, 0), 0))
    out = pl.pallas_call(
        functools.partial(_ffn_kernel, tiles_per_seq=tiles_per_seq, fc=fc),
        grid=(T // tm,),
        in_specs=[run_spec(D), halo, run_spec(p2d.shape[1]), full(lns), full(wup), full(cw), full(cb),
                  full(wdn), full(wpg), full(wpp)],
        out_specs=run_spec(D),
        out_shape=jax.ShapeDtypeStruct((T // tm, 8, tm // 8, D), F32),
        scratch_shapes=[pltpu.VMEM((FFN_HALO + tm, fc), F32)] * 4 + [pltpu.VMEM((tm, wdn.shape[0]), BF16)],
        compiler_params=pltpu.CompilerParams(
            dimension_semantics=("arbitrary",), vmem_limit_bytes=VMEM_LIMIT),
        name="ffn",
    )(runs(h2d), h2d, runs(p2d), lns, wup, cw, cb, wdn, wpg, wpp)
    return out.reshape(T, D)


def _prep_proj_weights(w_in, mu_wag, w1, a1, g1):
    D = w_in.shape[0]
    sizes = (RW_WIDTH, RW_WIDTH, RW_WIDTH, NSA_WIDTH) + (NSA_KV_WIDTH,) * 6 + (3 * NSA_Q_HEADS, D, D)
    offs = np.concatenate([[0], np.cumsum(sizes)])
    part = lambda i, j: w_in[:, offs[i]:offs[j]]
    mw, ma, mg = mu_wag[0][:, None], mu_wag[1][:, None], mu_wag[2][:, None]
    zg = jnp.zeros((D, RW_GATE_PAD - RW_GATE_LORA), F32)
    rw = jnp.concatenate([
        part(0, 3),
        (1.0 - mw) * w1, (1.0 - ma) * a1,
        mw * w1, ma * a1,
        (1.0 - mg) * g1, zg,
        mg * g1, zg], axis=1)
    def widen(w):
        w = w.reshape(D, NSA_KV_HEADS, HEAD_DIM)
        return jnp.concatenate([w, jnp.zeros_like(w)], axis=2).reshape(D, KEY_COLS)

    w_rows = jnp.concatenate([rw, part(4, 6), widen(part(6, 7)), widen(part(8, 9)), part(11, 13)], axis=1)
    w_cols = jnp.concatenate([part(3, 4), part(7, 8), part(9, 10), part(10, 11),
                              jnp.zeros((D, NSA_GATE_ROWS - 3 * NSA_Q_HEADS), F32)], axis=1)
    return w_rows.astype(BF16), w_cols.T.astype(BF16)


def _prep_rwkv_weights(w2, a2, g2):
    z = jnp.zeros_like(w2)
    w2a2 = jnp.concatenate([jnp.concatenate([w2, z], axis=1),
                            jnp.concatenate([z, a2], axis=1)], axis=0).astype(BF16)
    g2p = jnp.concatenate([g2, jnp.zeros((RW_GATE_PAD - RW_GATE_LORA, RW_WIDTH), F32)],
                          axis=0).astype(BF16)
    return w2a2, g2p


def _rwkv_from_proj(rw3, P):
    w2a2, g2p = _prep_rwkv_weights(P['rw_w2'][0], P['rw_a2'][0], P['rw_g2'][0])
    vecs = jnp.stack([P['rw_w0'][0], P['rw_a0'][0], P['rw_k_k'][0], P['rw_k_a'][0],
                      P['rw_r_k'][0].reshape(-1), P['rw_lnx_g'][0], P['rw_lnx_b'][0],
                      jnp.zeros((RW_WIDTH,), F32)], axis=0)
    return _rwkv_call(rw3, P['rw_mu_rkv'][0], vecs, w2a2, g2p)


def kernel(x, p, ln1_g, w_in, rw_mu_rkv, rw_mu_wag, rw_w0, rw_w1, rw_w2, rw_a0, rw_a1, rw_a2, rw_g1, rw_g2, rw_k_k, rw_k_a, rw_r_k, rw_lnx_g, rw_lnx_b, nsa_cmp_pos, nsa_cmp_w1, nsa_cmp_w2, w_out_a, w_out_b, w_out, ln2_g, w_up, conv_w, conv_b, w_down, ln3_g, w_ple_gate, w_ple_proj, ln_f_g):
    B, S, D = x.shape
    T = B * S
    assert w_in.shape[0] == 1, "single-layer block"
    P = dict(rw_mu_rkv=rw_mu_rkv, rw_w0=rw_w0, rw_w2=rw_w2, rw_a0=rw_a0, rw_a2=rw_a2, rw_g2=rw_g2,
             rw_k_k=rw_k_k, rw_k_a=rw_k_a, rw_r_k=rw_r_k, rw_lnx_g=rw_lnx_g, rw_lnx_b=rw_lnx_b,
             nsa_cmp_pos=nsa_cmp_pos, nsa_cmp_w1=nsa_cmp_w1, nsa_cmp_w2=nsa_cmp_w2)
    h = x.reshape(T, D)
    w_rows, w_cols = _prep_proj_weights(w_in[0], rw_mu_wag[0], rw_w1[0], rw_a1[0], rw_g1[0])
    rw, kvc, ksa, kwa, gates, qt, vst, vwt, gt = _proj_call(h, ln1_g[0][None], w_rows, w_cols, S)
    ya = _rwkv_from_proj(rw.reshape(B, S, RW_COLS), P)
    ybt = _nsa_from_proj(kvc, ksa, kwa, qt, vst, vwt, gt, P)
    h1 = _merge_call(h, ya.reshape(T, RW_WIDTH), ybt, gates,
                     w_out_a[0].astype(BF16), w_out_b[0].astype(BF16), w_out[0].astype(BF16))
    lns = jnp.stack([ln2_g[0], ln3_g[0], ln_f_g], axis=0)
    out = _ffn_call(h1, p[0].reshape(T, -1), lns, w_up[0].astype(BF16), conv_w[0], conv_b[0][None],
                    w_down[0].astype(BF16), w_ple_gate[0].astype(BF16), w_ple_proj[0].astype(BF16), S)
    return out.reshape(B, S, D)
```

```python
import functools

import numpy as np
import jax
import jax.numpy as jnp
from jax import lax
from jax.experimental import pallas as pl
from jax.experimental.pallas import tpu as pltpu

F32 = jnp.float32
BF16 = jnp.bfloat16

HEAD_DIM = 64
NORM_EPS = 1e-6
NEG_INF = -1e30

RW_HEADS = 8
RW_WIDTH = RW_HEADS * HEAD_DIM
RW_DECAY_LORA = 64
RW_AAA_LORA = 64
RW_GATE_LORA = 160
RW_LNX_EPS = 64e-5
RW_CHUNK = 64
RW_GROUP = 4
RW_GROUP_W = RW_GROUP * HEAD_DIM
RW_GATE_PAD = 256

NSA_Q_HEADS = 8
NSA_KV_HEADS = 2
NSA_GROUP = NSA_Q_HEADS // NSA_KV_HEADS
NSA_WIDTH = NSA_Q_HEADS * HEAD_DIM
NSA_KV_WIDTH = NSA_KV_HEADS * HEAD_DIM
CMP_BLOCK = 32
CMP_STRIDE = 16
CMP_HIDDEN = 128
SEL_BLOCK = 64
SEL_TOP = 16
SEL_FORCE_SCORE = 1e4
WINDOW = 512

CONV_WIDTH = 3

RW_COLS = 3 * RW_WIDTH + 2 * 128 + 2 * RW_GATE_PAD
GATE_COLS = 2 * 1024

V7X_VMEM_BYTES = 64 * 1024 * 1024
VMEM_LIMIT = V7X_VMEM_BYTES - 8 * 1024 * 1024


def _dot(a, b):
    return jnp.dot(a, b, preferred_element_type=F32)


def _dot_nt(a, b):
    return lax.dot_general(a, b, (((1,), (1,)), ((), ())), preferred_element_type=F32)


def _dot_tn(a, b):
    return lax.dot_general(a, b, (((0,), (0,)), ((), ())), preferred_element_type=F32)


def _split2(x):
    hi = x.astype(BF16)
    lo = (x - hi.astype(F32)).astype(BF16)
    return hi, lo


def _sigmoid(x):
    return 1.0 / (1.0 + jnp.exp(-x))


LOG2_E = 1.4426950408889634
DECAY_SCALE_LOG2 = 0.6065306597126334 * LOG2_E


PROJ_TM = 1024
PROJ_CHUNK = 768
KVC_COLS = 2 * NSA_KV_WIDTH
KEY_COLS = NSA_KV_HEADS * 128
NSA_GATE_ROWS = 32
T_ROWS = NSA_WIDTH + 2 * NSA_KV_WIDTH + NSA_GATE_ROWS


def _proj_kernel(x_ref, g_ref, w_ref, wt_ref, rw_ref, kvc_ref, ksa_ref, kwa_ref, gate_ref,
                 qt_ref, vst_ref, vwt_ref, gt_ref, *, tiles_per_seq):
    tm = x_ref.shape[0]
    x = x_ref[...]
    ms = jnp.mean(x * x, axis=-1, keepdims=True)
    u = (x * lax.rsqrt(ms + NORM_EPS) * g_ref[...]).astype(BF16)

    col = 0
    for o_ref in (rw_ref, kvc_ref):
        width = o_ref.shape[-1]
        for c in range(0, width, PROJ_CHUNK):
            hi = min(c + PROJ_CHUNK, width)
            o_ref[:, c:hi] = _dot(u, w_ref[:, col + c:col + hi]).astype(o_ref.dtype)
        col += width

    s0 = (pl.program_id(0) % tiles_per_seq) * tm
    blk = (s0 + lax.broadcasted_iota(jnp.int32, (tm, KEY_COLS), 0)) // SEL_BLOCK
    lane = lax.broadcasted_iota(jnp.int32, (tm, KEY_COLS), 1) % 128
    onehot = jnp.where(lane - HEAD_DIM == blk, 1.0, 0.0)
    ks = _dot(u, w_ref[:, col:col + KEY_COLS]) + onehot
    kw = _dot(u, w_ref[:, col + KEY_COLS:col + 2 * KEY_COLS])
    for hk in range(NSA_KV_HEADS):
        ksa_ref[0, hk] = ks[:, hk * 128:(hk + 1) * 128].astype(ksa_ref.dtype)
        kwa_ref[0, hk] = kw[:, hk * 128:(hk + 1) * 128].astype(kwa_ref.dtype)
    col += 2 * KEY_COLS

    width = gate_ref.shape[-1]
    for c in range(0, width, PROJ_CHUNK):
        hi = min(c + PROJ_CHUNK, width)
        gate_ref[:, c:hi] = _dot(u, w_ref[:, col + c:col + hi]).astype(gate_ref.dtype)

    t = _dot_nt(wt_ref[...], u)
    row = 0
    for o_ref in (qt_ref, vst_ref, vwt_ref, gt_ref):
        n = o_ref.shape[1]
        o_ref[0] = t[row:row + n].astype(o_ref.dtype)
        row += n


def _proj_call(x2, g, w_all, wt_all, seq):
    T, D = x2.shape
    tm = PROJ_TM
    B = T // seq
    tps = seq // tm
    rows = lambda w: pl.BlockSpec((tm, w), lambda i: (i, 0))
    full = lambda a: pl.BlockSpec(a.shape, lambda i: (0,) * a.ndim, pipeline_mode=pl.Buffered(1))
    keys = pl.BlockSpec((1, NSA_KV_HEADS, tm, 128), lambda i: (i // tps, 0, i % tps, 0))
    tcols = lambda n: pl.BlockSpec((1, n, tm), lambda i: (i // tps, 0, i % tps))
    sds = jax.ShapeDtypeStruct
    return pl.pallas_call(
        functools.partial(_proj_kernel, tiles_per_seq=tps),
        grid=(T // tm,),
        in_specs=[rows(D), full(g), full(w_all), full(wt_all)],
        out_specs=[rows(RW_COLS), rows(KVC_COLS), keys, keys, rows(GATE_COLS),
                   tcols(NSA_WIDTH), tcols(NSA_KV_WIDTH), tcols(NSA_KV_WIDTH), tcols(NSA_GATE_ROWS)],
        out_shape=[sds((T, RW_COLS), BF16), sds((T, KVC_COLS), BF16),
                   sds((B, NSA_KV_HEADS, seq, 128), BF16), sds((B, NSA_KV_HEADS, seq, 128), BF16),
                   sds((T, GATE_COLS), BF16),
                   sds((B, NSA_WIDTH, seq), BF16), sds((B, NSA_KV_WIDTH, seq), BF16),
                   sds((B, NSA_KV_WIDTH, seq), BF16), sds((B, NSA_GATE_ROWS, seq), BF16)],
        compiler_params=pltpu.CompilerParams(
            dimension_semantics=("arbitrary",), vmem_limit_bytes=VMEM_LIMIT),
        name="proj",
    )(x2, g, w_all, wt_all)


def _rwkv_kernel(x_ref, mu_ref, vec_ref, w2a2_ref, g2_ref, o_ref, state_ref, prev_ref):
    C = RW_CHUNK
    GW = RW_GROUP_W
    W = RW_WIDTH
    NB = x_ref.shape[0]
    R = NB * C
    t_idx = pl.program_id(1)
    ops = {}

    @pl.when(t_idx == 0)
    def _():
        state_ref[...] = jnp.zeros_like(state_ref)
        prev_ref[...] = jnp.zeros_like(prev_ref)

    mu = mu_ref[...]
    w0, a0, k_k, k_a, r_k, lnx_g, lnx_b = (vec_ref[i:i + 1, :] for i in range(7))

    gr = lax.broadcasted_iota(jnp.int32, (GW, GW), 0) // HEAD_DIM
    gc = lax.broadcasted_iota(jnp.int32, (GW, GW), 1) // HEAD_DIM
    blk = gr == gc
    ones_bd = jnp.where(blk, 1.0, 0.0).astype(BF16)

    def headsums(zs):
        parts = []
        for z in zs:
            zb = z.astype(BF16)
            parts += [zb[:, :GW], zb[:, GW:]]
        s = _dot(jnp.concatenate(parts, axis=0), ones_bd)
        return [jnp.concatenate([s[2 * R * i:2 * R * i + R], s[2 * R * i + R:2 * R * (i + 1)]], axis=1)
                for i in range(len(zs))]

    t_n = lax.broadcasted_iota(jnp.int32, (C, GW), 0)
    s_n = lax.broadcasted_iota(jnp.int32, (C, GW), 1) % HEAD_DIM
    strict = t_n > s_n
    incl = t_n >= s_n
    eye_n = jnp.where(t_n == s_n, 1.0, 0.0)

    def bd(z):
        z4 = jnp.concatenate([z.astype(F32)] * RW_GROUP, axis=0)
        return jnp.where(blk, z4, 0.0).astype(BF16)

    def prepare():
        x = x_ref[...].reshape(R, RW_COLS).astype(F32)
        rolled = pltpu.roll(x, 1, axis=0)
        row8 = lax.broadcasted_iota(jnp.int32, (8, 1), 0)
        pieces = []
        for bi in range(NB):
            pieces.append(jnp.where(row8 == 0, prev_ref[bi, 0:1, :], rolled[bi * C:bi * C + 8]))
            pieces.append(rolled[bi * C + 8:(bi + 1) * C])
            prev_ref[bi, 0:1, :] = x[(bi + 1) * C - 1:(bi + 1) * C, :]
        xs = jnp.concatenate(pieces, axis=0)

        def lerp(j):
            cur = x[:, j * W:(j + 1) * W]
            return cur + (xs[:, j * W:(j + 1) * W] - cur) * mu[j:j + 1, :]

        r, k, v = lerp(0), lerp(1), lerp(2)
        o = 3 * W
        pre_a = x[:, o:o + 128] + xs[:, o + 128:o + 256]
        lane = lax.broadcasted_iota(jnp.int32, (R, 128), 1)
        h_a = jnp.where(lane < RW_DECAY_LORA, jnp.tanh(pre_a), pre_a)
        lwa = _dot(h_a.astype(BF16), w2a2_ref[...])
        o += 256
        pre_g = x[:, o:o + RW_GATE_PAD] + xs[:, o + RW_GATE_PAD:o + 2 * RW_GATE_PAD]
        g = _dot(_sigmoid(pre_g).astype(BF16), g2_ref[...])
        ld = (-DECAY_SCALE_LOG2) * _sigmoid(w0 + lwa[:, :W])
        a = _sigmoid(a0 + lwa[:, W:])
        kkr = k * k_k
        k2 = k * (1.0 + (a - 1.0) * k_a)
        kk_ss, bonus = headsums([kkr * kkr, r * k2 * r_k])
        kk = kkr / jnp.maximum(jnp.sqrt(kk_ss), 1e-12)
        b = kk * a
        tr = lax.broadcasted_iota(jnp.int32, (R, R), 0)
        tc = lax.broadcasted_iota(jnp.int32, (R, R), 1)
        tri = jnp.where((tr >= tc) & (tr // C == tc // C), 1.0, 0.0).astype(BF16)
        l_inc = _dot(tri, jnp.concatenate(_split2(ld), axis=1))
        l_inc = l_inc[:, :W] + l_inc[:, W:]
        l_end = [l_inc[(bi + 1) * C - 1:(bi + 1) * C] for bi in range(NB)]
        e_neg = jnp.exp2(-l_inc)
        e_tail = jnp.concatenate([jnp.exp2(l_end[bi] - l_inc[bi * C:(bi + 1) * C]) for bi in range(NB)],
                                 axis=0)
        new_opb = (-kk * jnp.exp2(l_inc - ld), r * jnp.exp2(l_inc), b * e_neg, k2 * e_neg,
                   b * e_tail, k2 * e_tail)
        cast = (True, True, False, False, True, True)
        ops['matmul'] = tuple(z.astype(BF16) if c else z for z, c in zip(new_opb, cast))
        ops['output'] = (v, bonus, g)
        ops['decay'] = jnp.concatenate([jnp.broadcast_to(jnp.exp2(le), (8, W)) for le in l_end], axis=0)

    def consume():
        n_grp = W // GW
        chains = [(bi, gi) for bi in range(NB) for gi in range(n_grp)]
        cut = lambda z, c: z[c[0] * C:(c[0] + 1) * C, c[1] * GW:(c[1] + 1) * GW]
        each = lambda f, *lists: [f(*args) for args in zip(*lists)]

        p_a_hat, p_r_hat, p_b_hat, p_k_hat, p_b_tail, p_k_tail = ops['matmul']
        p_v, p_bonus, p_g = ops['output']
        p_elc = ops['decay']
        a_h = [cut(p_a_hat, c) for c in chains]
        r_h = [cut(p_r_hat, c) for c in chains]
        ar = each(lambda x1, x2: jnp.concatenate([x1, x2], axis=0), a_h, r_h)
        m1 = each(_dot_nt, ar, [bd(cut(p_b_hat, c)) for c in chains])
        m2 = each(_dot_nt, ar, [bd(cut(p_k_hat, c)) for c in chains])
        m_ab = [jnp.where(strict, m[:C], 0.0) for m in m1]
        m_rb = [jnp.where(incl, m[C:], 0.0) for m in m1]
        m_ak = [jnp.where(strict, m[:C], 0.0) for m in m2]
        m_rk = [jnp.where(incl, m[C:], 0.0) for m in m2]

        tinv = [eye_n + m for m in m_ab]
        p = each(lambda m: _dot(m.astype(BF16), bd(m)), m_ab)
        power = 2
        while 2 * power < C:
            tp = each(lambda t, q: _dot(jnp.concatenate([t, q], axis=0).astype(BF16), bd(q)), tinv, p)
            tinv = each(lambda t, x1: t + x1[:C], tinv, tp)
            p = [x1[C:] for x1 in tp]
            power *= 2
        tinv = each(lambda t, q: t + _dot(t.astype(BF16), bd(q)), tinv, p)

        s_old = [state_ref[i * GW:(i + 1) * GW, :] for i in range(len(chains))]
        s_bf = [s.astype(BF16) for s in s_old]
        vg = [cut(p_v, c) for c in chains]
        bd_v = [bd(x1) for x1 in vg]
        xz = each(lambda x1, s, m, bv: _dot_nt(x1, s) + _dot(m.astype(BF16), bv), a_h, s_bf, m_ak, bd_v)
        u = each(lambda t, x1: _dot(t.astype(BF16), bd(x1)), tinv, xz)
        y = each(lambda x1, s, mb, mk, uu, bv:
                 _dot_nt(x1, s) + _dot(jnp.concatenate([mb, mk], axis=1).astype(BF16),
                                       jnp.concatenate([bd(uu), bv], axis=0)),
                 r_h, s_bf, m_rb, m_rk, u, bd_v)
        new_states = []
        for c, uu, vv, s in zip(chains, u, vg, s_old):
            upd = _dot_tn(jnp.concatenate([uu, vv], axis=0).astype(BF16),
                          jnp.concatenate([cut(p_b_tail, c), cut(p_k_tail, c)], axis=0))
            decay = p_elc[c[0] * 8:c[0] * 8 + 1, c[1] * GW:(c[1] + 1) * GW]
            new_states.append(s * decay + jnp.where(blk, upd, 0.0))
        state_ref[...] = jnp.concatenate(new_states, axis=0)
        y_rows = [jnp.concatenate(y[bi * n_grp:(bi + 1) * n_grp], axis=1) for bi in range(NB)]
        y = jnp.concatenate(y_rows, axis=0)
        mean = headsums([y])[0] * (1.0 / HEAD_DIM)
        yc = y - mean
        var = headsums([yc * yc])[0] * (1.0 / HEAD_DIM)
        yn = yc * lax.rsqrt(var + RW_LNX_EPS) * lnx_g + lnx_b
        yn = yn + p_bonus * p_v
        o_ref[...] = (yn * p_g).reshape(NB, C, W).astype(o_ref.dtype)

    prepare()
    consume()


RW_SEQS_PER_STEP = 8


def _rwkv_call(rw3, mu, vecs, w2a2, g2p):
    B, S, _ = rw3.shape
    C = RW_CHUNK
    nb = RW_SEQS_PER_STEP if B % RW_SEQS_PER_STEP == 0 else 1
    n_groups = RW_WIDTH // RW_GROUP_W
    return pl.pallas_call(
        _rwkv_kernel,
        grid=(B // nb, S // C),
        in_specs=[
            pl.BlockSpec((nb, C, RW_COLS), lambda b, t: (b, t, 0)),
            pl.BlockSpec(mu.shape, lambda b, t: (0, 0)),
            pl.BlockSpec(vecs.shape, lambda b, t: (0, 0)),
            pl.BlockSpec(w2a2.shape, lambda b, t: (0, 0)),
            pl.BlockSpec(g2p.shape, lambda b, t: (0, 0)),
        ],
        out_specs=pl.BlockSpec((nb, C, RW_WIDTH), lambda b, t: (b, t, 0)),
        out_shape=jax.ShapeDtypeStruct((B, S, RW_WIDTH), BF16),
        scratch_shapes=[
            pltpu.VMEM((nb * n_groups * RW_GROUP_W, RW_GROUP_W), F32),
            pltpu.VMEM((nb, 8, RW_COLS), F32),
        ],
        compiler_params=pltpu.CompilerParams(
            dimension_semantics=("arbitrary", "arbitrary"), vmem_limit_bytes=VMEM_LIMIT),
        name="rwkv",
    )(rw3, mu, vecs, w2a2, g2p)


def _compress_kernel(kv_ref, wab_ref, pos_ref, w1_ref, w2_ref, w2t_ref, kcb_ref, vcbt_ref):
    n_half = kv_ref.shape[1]
    for j in range(2):
        src = jnp.concatenate(
            [kv_ref[0, :, l * KVC_COLS + j * NSA_KV_WIDTH:l * KVC_COLS + (j + 1) * NSA_KV_WIDTH]
             for l in range(CMP_STRIDE)], axis=1)
        pab = _dot(src, wab_ref[j])
        half = NSA_KV_HEADS * CMP_HIDDEN
        pa, pb = pab[:, :half], pab[:, half:]
        pb = pltpu.roll(pb, n_half - 1, axis=0)
        pos_term = _dot(pos_ref[j], w1_ref[j])[0:1]
        hid = pa + pb + jnp.concatenate([pos_term] * NSA_KV_HEADS, axis=1)
        act = (hid * _sigmoid(hid)).astype(BF16)
        for hk in range(NSA_KV_HEADS):
            a_h = act[:, hk * CMP_HIDDEN:(hk + 1) * CMP_HIDDEN]
            if j == 0:
                kcb_ref[0, hk] = _dot(a_h, w2_ref[j]).astype(kcb_ref.dtype)
            else:
                vcbt_ref[0, hk] = _dot_nt(w2t_ref[j], a_h).astype(vcbt_ref.dtype)


def _compress_call(kv, wab, pos8, w1, w2, w2t):
    B, n_half, width = kv.shape
    full = lambda a: pl.BlockSpec(a.shape, lambda b: (0,) * a.ndim)
    return pl.pallas_call(
        _compress_kernel,
        grid=(B,),
        in_specs=[pl.BlockSpec((1, n_half, width), lambda b: (b, 0, 0)),
                  full(wab), full(pos8), full(w1), full(w2), full(w2t)],
        out_specs=[pl.BlockSpec((1, NSA_KV_HEADS, n_half, HEAD_DIM), lambda b: (b, 0, 0, 0)),
                   pl.BlockSpec((1, NSA_KV_HEADS, HEAD_DIM, n_half), lambda b: (b, 0, 0, 0))],
        out_shape=[jax.ShapeDtypeStruct((B, NSA_KV_HEADS, n_half, HEAD_DIM), BF16),
                   jax.ShapeDtypeStruct((B, NSA_KV_HEADS, HEAD_DIM, n_half), BF16)],
        compiler_params=pltpu.CompilerParams(
            dimension_semantics=("arbitrary",), vmem_limit_bytes=VMEM_LIMIT),
        name="nsa_compress",
    )(kv, wab, pos8, w1, w2, w2t)


NSA_TQ = 256
NSA_AUG = 128
NSA_BIAS_ROWS = 32
SEL_STAGE = 4


def _nsa_kernel(qt_ref, gt_ref, ks_ref, vst_ref, kw_ref, vwt_ref, kcb_ref, vcbt_ref, ovt_ref,
                 tri_ref, wbias_ref, o_ref, *, seq):
    tq = NSA_TQ
    G = NSA_GROUP
    R = G * tq
    n_half = kcb_ref.shape[2]
    n_cmp = n_half - 1
    n_sel = seq // SEL_BLOCK
    n_top = min(SEL_TOP, n_sel)
    n_wchunks = WINDOW // tq + 1
    step = pl.program_id(1)
    q0 = step * tq

    t_lane = q0 + lax.broadcasted_iota(jnp.int32, (1, R), 1) % tq
    gates = _sigmoid(gt_ref[0].astype(F32))
    pad_rows = jnp.zeros((NSA_AUG - HEAD_DIM - NSA_BIAS_ROWS, tq), BF16)

    hrow = lambda hk, g: slice((hk * G + g) * HEAD_DIM, (hk * G + g + 1) * HEAD_DIM)
    o_cmp, qaug = {}, {}
    for hk in range(NSA_KV_HEADS):
        q64 = jnp.concatenate([qt_ref[0, hrow(hk, g), :] for g in range(G)], axis=1)
        q64 = q64 * jnp.asarray(HEAD_DIM ** -0.5, BF16)

        cidx = lax.broadcasted_iota(jnp.int32, (n_half, R), 0)
        cvalid = (cidx * CMP_STRIDE + (CMP_BLOCK - 1) <= t_lane) & (cidx < n_cmp)
        s = jnp.where(cvalid, _dot(kcb_ref[0, hk], q64), NEG_INF)
        m = jnp.max(s, axis=0, keepdims=True)
        e = jnp.where(cvalid, jnp.exp(s - m), 0.0)
        l = jnp.sum(e, axis=0, keepdims=True)
        p_c = e / jnp.where(l > 0.0, l, 1.0)
        o_c = _dot(vcbt_ref[0, hk], p_c.astype(BF16))

        psum = p_c[:, 0:tq]
        for g in range(1, G):
            psum = psum + p_c[:, g * tq:(g + 1) * tq]
        hi, lo = _split2(psum)
        imp2 = _dot(ovt_ref[...], jnp.concatenate([hi, lo], axis=1))
        imp = imp2[:, :tq] + imp2[:, tq:]
        jblk = lax.broadcasted_iota(jnp.int32, (n_sel, tq), 0)
        cur = (q0 + lax.broadcasted_iota(jnp.int32, (n_sel, tq), 1)) // SEL_BLOCK
        forced = (jblk == 0) | (jblk == cur) | (jblk == cur - 1)
        score = jnp.where(forced, SEL_FORCE_SCORE, jnp.where(jblk <= cur, imp, -1.0))
        n_grp = n_sel // 8
        rows = [score[8 * q:8 * q + 8] for q in range(n_grp)]
        ranks = [jnp.zeros((8, tq), F32)] * n_grp
        sub8 = lax.broadcasted_iota(jnp.int32, (8, tq), 0)
        for j in range(n_sel):
            sj = score[j:j + 1, :]
            for q in range(n_grp):
                wins = jnp.where(sj > rows[q], 1.0, 0.0)
                wins_ties = jnp.where(sj >= rows[q], 1.0, 0.0)
                if 8 * q + 7 < j:
                    ahead = wins
                elif 8 * q > j:
                    ahead = wins_ties
                else:
                    ahead = jnp.where(sub8 > j - 8 * q, wins_ties, wins)
                ranks[q] = ranks[q] + ahead
        rank = jnp.concatenate(ranks, axis=0)
        sel_bias = jnp.where(rank < n_top, 0.0, NEG_INF).astype(BF16)
        q_l2 = (q64.astype(F32) * LOG2_E).astype(BF16)
        for g in range(G):
            o_cmp[hk, g] = o_c[:, g * tq:(g + 1) * tq]
            qaug[hk, g] = jnp.concatenate([q_l2[:, g * tq:(g + 1) * tq], sel_bias, pad_rows], axis=0)

    chains = [(hk, g) for hk in range(NSA_KV_HEADS) for g in range(G)]

    def softmax_pv(s_list, vt_of, carry=None):
        m_blk = [jnp.max(s, axis=0, keepdims=True) for s in s_list]
        if carry is None:
            m_new = m_blk
        else:
            m_new = [jnp.maximum(c[0], mb) for c, mb in zip(carry, m_blk)]
        p = [jnp.exp2(s - mn) for s, mn in zip(s_list, m_new)]
        l_blk = [jnp.sum(x, axis=0, keepdims=True) for x in p]
        pv = [_dot(vt_of(c), x.astype(BF16)) for c, x in zip(chains, p)]
        if carry is None:
            return [(mn, lb, a) for mn, lb, a in zip(m_new, l_blk, pv)]
        alpha = [jnp.exp2(c[0] - mn) for c, mn in zip(carry, m_new)]
        return [(mn, c[1] * al + lb, c[2] * al + a)
                for c, mn, al, lb, a in zip(carry, m_new, alpha, l_blk, pv)]

    def vcols(ref, hk, j):
        return ref[0, hk * HEAD_DIM:(hk + 1) * HEAD_DIM, pl.ds(pl.multiple_of(j * tq, tq), tq)]

    wchunk = [jnp.maximum(step - (n_wchunks - 1) + w, 0) for w in range(n_wchunks)]
    kw_rows = [jnp.concatenate([kw_ref[0, hk, j] for j in wchunk], axis=0) for hk in range(NSA_KV_HEADS)]
    vw_cols = [jnp.concatenate([vcols(vwt_ref, hk, j) for j in wchunk], axis=1)
               for hk in range(NSA_KV_HEADS)]
    wbias = wbias_ref[jnp.minimum(step, n_wchunks - 1)]
    s_win = [_dot(kw_rows[hk], qaug[hk, g]) + wbias for hk, g in chains]
    win = softmax_pv(s_win, lambda c: vw_cols[c[0]])

    tri = tri_ref[...]
    flatten = lambda carry: tuple(x for c in carry for x in c)

    def sel_variant(k):
        def stage(lo, hi, diagonal, carry):
            n = hi - lo
            rows = lambda hk: ks_ref[0, hk, lo:hi].reshape(n * tq, NSA_AUG)
            cols = lambda hk: vst_ref[0, hk * HEAD_DIM:(hk + 1) * HEAD_DIM, lo * tq:hi * tq]
            s = [_dot(rows(hk), qaug[hk, g]) for hk, g in chains]
            if diagonal:
                bias = jnp.concatenate([jnp.zeros(((n - 1) * tq, tq), F32), tri], axis=0) if n > 1 else tri
                s = [x + bias for x in s]
            return softmax_pv(s, lambda c: cols(c[0]), carry)

        def run():
            first = max(k + 1 - SEL_STAGE, 0)
            carry = stage(first, k + 1, True, None)
            if first > 0:
                carry = stage(0, first, False, carry)
            return flatten(carry)
        return run

    flat = lax.switch(step, [sel_variant(k) for k in range(seq // tq)])
    sel = [tuple(flat[3 * i:3 * i + 3]) for i in range(len(chains))]

    for i, (hk, g) in enumerate(chains):
        gate = lambda j: gates[(hk * G + g) * 3 + j:(hk * G + g) * 3 + j + 1, :]
        out = (gate(0) * o_cmp[hk, g] + gate(1) * (sel[i][2] / sel[i][1])
               + gate(2) * (win[i][2] / win[i][1]))
        o_ref[0, hrow(hk, g), :] = out.astype(o_ref.dtype)


def _nsa_call(qt, gt, ks5, vst, kw5, vwt, kcb, vcbt, ovt, tri, wbias, seq):
    B = qt.shape[0]
    tq = NSA_TQ
    per_b = lambda a: pl.BlockSpec((1,) + a.shape[1:], lambda b, i: (b,) + (0,) * (a.ndim - 1))
    full = lambda a: pl.BlockSpec(a.shape, lambda b, i: (0,) * a.ndim)
    return pl.pallas_call(
        functools.partial(_nsa_kernel, seq=seq),
        grid=(B, seq // tq),
        in_specs=[pl.BlockSpec((1, NSA_WIDTH, tq), lambda b, i: (b, 0, i)),
                  pl.BlockSpec((1, gt.shape[1], tq), lambda b, i: (b, 0, i)),
                  per_b(ks5), per_b(vst), per_b(kw5), per_b(vwt), per_b(kcb), per_b(vcbt),
                  full(ovt), full(tri), full(wbias)],
        out_specs=pl.BlockSpec((1, NSA_WIDTH, tq), lambda b, i: (b, 0, i)),
        out_shape=jax.ShapeDtypeStruct((B, NSA_WIDTH, seq), BF16),
        compiler_params=pltpu.CompilerParams(
            dimension_semantics=("arbitrary", "arbitrary"), vmem_limit_bytes=VMEM_LIMIT),
        name="nsa_attention",
    )(qt, gt, ks5, vst, kw5, vwt, kcb, vcbt, ovt, tri, wbias)


def _nsa_from_proj(kvc, ksa, kwa, qt, vst, vwt, gt, P):
    B, Hk, S, _ = ksa.shape
    tq = NSA_TQ
    Dh = HEAD_DIM
    n_sel = S // SEL_BLOCK
    assert S % tq == 0 and WINDOW % tq == 0 and n_sel == NSA_BIAS_ROWS
    n_half = S // CMP_STRIDE
    kv = kvc.reshape(B, n_half, CMP_STRIDE * KVC_COLS)
    w1 = P['nsa_cmp_w1'][0]
    w1h = w1.astype(BF16).reshape(2, 2, CMP_STRIDE, 1, HEAD_DIM, CMP_HIDDEN)
    zero = jnp.zeros_like(w1h[0, 0])

    def block(j, a, h):
        parts = [w1h[j, a] if hh == h else zero for hh in range(Hk)]
        return jnp.concatenate(parts, axis=1).reshape(CMP_STRIDE * NSA_KV_WIDTH, CMP_HIDDEN)

    wab = jnp.stack([jnp.concatenate([block(j, a, h) for a in range(2) for h in range(Hk)], axis=1)
                     for j in range(2)])
    pos8 = jnp.broadcast_to(P['nsa_cmp_pos'][0].reshape(2, 1, CMP_BLOCK * HEAD_DIM),
                            (2, 8, CMP_BLOCK * HEAD_DIM)).astype(BF16)
    w2 = P['nsa_cmp_w2'][0].astype(BF16)
    kcb, vcbt = _compress_call(kv, wab, pos8, w1.astype(BF16), w2, jnp.swapaxes(w2, 1, 2))

    ks5 = ksa.reshape(B, Hk, S // tq, tq, NSA_AUG)
    kw5 = kwa.reshape(B, Hk, S // tq, tq, NSA_AUG)

    n_cmp = (S - CMP_BLOCK) // CMP_STRIDE + 1
    cmp_start = np.arange(n_half) * CMP_STRIDE
    sel_start = np.arange(n_sel) * SEL_BLOCK
    overlap = ((cmp_start[:, None] <= sel_start[None, :] + SEL_BLOCK - 1)
               & (cmp_start[:, None] + CMP_BLOCK - 1 >= sel_start[None, :])
               & (np.arange(n_half)[:, None] < n_cmp)).astype(np.float32)
    tri = np.where(np.arange(tq)[:, None] <= np.arange(tq)[None, :], 0.0, NEG_INF).astype(np.float32)
    n_w = WINDOW // tq
    masked = np.full((tq, tq), NEG_INF, np.float32)
    clear = np.zeros((tq, tq), np.float32)
    wbias = np.stack([np.concatenate([masked] * (n_w - v) + [clear] * v + [tri], axis=0) for v in range(n_w)]
                     + [np.concatenate([NEG_INF - tri] + [clear] * (n_w - 1) + [tri], axis=0)])
    return _nsa_call(qt, gt, ks5, vst, kw5, vwt, kcb, vcbt, jnp.asarray(overlap.T, BF16),
                      jnp.asarray(tri), jnp.asarray(wbias), S)


def _merge_kernel(x_ref, ya_ref, ybt_ref, gate_ref, wa_ref, wb_ref, wo_ref, o_ref):
    D = x_ref.shape[-1]
    ta = _dot(ya_ref[...], wa_ref[...])
    tb = _dot_tn(ybt_ref[0], wb_ref[...])
    ga = _sigmoid(gate_ref[:, :D].astype(F32))
    gb = _sigmoid(gate_ref[:, D:].astype(F32))
    mix = (ga * ta + gb * tb).astype(BF16)
    o_ref[...] = x_ref[...] + _dot(mix, wo_ref[...])


def _merge_call(x2, ya2, ybt, gates, wa, wb, wo, tm=1024):
    T, D = x2.shape
    tiles_per_seq = ybt.shape[2] // tm
    row = lambda w: pl.BlockSpec((tm, w), lambda i: (i, 0))
    full = lambda a: pl.BlockSpec(a.shape, lambda i: (0,) * a.ndim)
    ybt_spec = pl.BlockSpec((1, ybt.shape[1], tm), lambda i: (i // tiles_per_seq, 0, i % tiles_per_seq))
    return pl.pallas_call(
        _merge_kernel,
        grid=(T // tm,),
        in_specs=[row(D), row(ya2.shape[1]), ybt_spec, row(gates.shape[1]),
                  full(wa), full(wb), full(wo)],
        out_specs=row(D),
        out_shape=jax.ShapeDtypeStruct((T, D), F32),
        compiler_params=pltpu.CompilerParams(
            dimension_semantics=("arbitrary",), vmem_limit_bytes=VMEM_LIMIT),
        name="merge",
    )(x2, ya2, ybt, gates, wa, wb, wo)


FFN_HALO = 8


def _rms(x, g):
    return x * lax.rsqrt(jnp.mean(x * x, axis=-1, keepdims=True) + NORM_EPS) * g


def _ffn_kernel(h_ref, halo_ref, p_ref, ln_ref, wup_ref, cw_ref, cb_ref, wdn_ref, wpg_ref, wpp_ref,
                o_ref, up0a_ref, up0b_ref, up1a_ref, up1b_ref, act_ref, *, tiles_per_seq, fc):
    up_refs = ((up0a_ref, up0b_ref), (up1a_ref, up1b_ref))
    tm = act_ref.shape[0]
    D = halo_ref.shape[1]
    V = tm // 8
    d_ff = wdn_ref.shape[0]
    slabs = lambda ref, w: jnp.swapaxes(ref[0], 0, 1).reshape(tm, w)
    h = slabs(h_ref, D)
    first = (pl.program_id(0) % tiles_per_seq) == 0
    halo = jnp.where(first, 0.0, halo_ref[...])
    ln2, ln3, lnf = ln_ref[0:1, :], ln_ref[1:2, :], ln_ref[2:3, :]
    u = jnp.concatenate([_rms(halo, ln2), _rms(h, ln2)], axis=0).astype(BF16)

    n_chunks = d_ff // fc
    sub = lax.broadcasted_iota(jnp.int32, (8, 1), 0)

    def project(c):
        for half in range(2):
            col = half * d_ff + c * fc
            up_refs[c % 2][half][...] = _dot(u, wup_ref[:, col:col + fc])

    def conv(c, half):
        ref = up_refs[c % 2][half]
        col = half * d_ff + c * fc
        halo_up = ref[0:FFN_HALO, :]
        last = lambda k: ref[FFN_HALO + tm - 8 * k:FFN_HALO + tm - 8 * (k - 1), :]
        wrap1 = pltpu.roll(jnp.where(sub == 7, halo_up, last(1)), 1, axis=0)
        wrap2 = pltpu.roll(jnp.where(sub == 7, pltpu.roll(halo_up, 1, axis=0), last(2)), 1, axis=0)
        x0 = ref[FFN_HALO:FFN_HALO + tm, :]
        x1 = jnp.concatenate([wrap1, ref[FFN_HALO:FFN_HALO + tm - 8, :]], axis=0)
        x2 = jnp.concatenate([wrap2, wrap1, ref[FFN_HALO:FFN_HALO + tm - 16, :]], axis=0)
        tap = lambda j: cw_ref[j:j + 1, col:col + fc]
        return cb_ref[:, col:col + fc] + tap(0) * x2 + tap(1) * x1 + tap(2) * x0

    project(0)
    for c in range(n_chunks):
        if c + 1 < n_chunks:
            project(c + 1)
        a = conv(c, 0)
        b = conv(c, 1)
        act_ref[:, c * fc:(c + 1) * fc] = (a * _sigmoid(a) * b).astype(BF16)
    h2 = h + _dot(act_ref[...], wdn_ref[...])
    gate = _sigmoid(_dot(_rms(h2, ln3).astype(BF16), wpg_ref[...]))
    h3 = h2 + gate * _dot(slabs(p_ref, p_ref.shape[3]).astype(BF16), wpp_ref[...])
    o_ref[0] = jnp.swapaxes(_rms(h3, lnf).reshape(V, 8, D), 0, 1)


FFN_TM = 512
FFN_FC = 256


def _ffn_call(h2d, p2d, lns, wup, cw, cb, wdn, wpg, wpp, seq):
    T, D = h2d.shape
    tm, fc = FFN_TM, FFN_FC
    assert CONV_WIDTH == 3 and seq % tm == 0 and wdn.shape[0] % fc == 0
    tiles_per_seq = seq // tm
    runs = lambda x: x.reshape(T // tm, 8, tm // 8, x.shape[1])
    run_spec = lambda w: pl.BlockSpec((1, 8, tm // 8, w), lambda i: (i, 0, 0, 0))
    full = lambda a: pl.BlockSpec(a.shape, lambda i: (0,) * a.ndim, pipeline_mode=pl.Buffered(1))
    halo = pl.BlockSpec((FFN_HALO, D), lambda i: (jnp.maximum(i * (tm // FFN_HALO) - 1, 0), 0))
    out = pl.pallas_call(
        functools.partial(_ffn_kernel, tiles_per_seq=tiles_per_seq, fc=fc),
        grid=(T // tm,),
        in_specs=[run_spec(D), halo, run_spec(p2d.shape[1]), full(lns), full(wup), full(cw), full(cb),
                  full(wdn), full(wpg), full(wpp)],
        out_specs=run_spec(D),
        out_shape=jax.ShapeDtypeStruct((T // tm, 8, tm // 8, D), F32),
        scratch_shapes=[pltpu.VMEM((FFN_HALO + tm, fc), F32)] * 4 + [pltpu.VMEM((tm, wdn.shape[0]), BF16)],
        compiler_params=pltpu.CompilerParams(
            dimension_semantics=("arbitrary",), vmem_limit_bytes=VMEM_LIMIT),
        name="ffn",
    )(runs(h2d), h2d, runs(p2d), lns, wup, cw, cb, wdn, wpg, wpp)
    return out.reshape(T, D)


def _prep_proj_weights(w_in, mu_wag, w1, a1, g1):
    D = w_in.shape[0]
    sizes = (RW_WIDTH, RW_WIDTH, RW_WIDTH, NSA_WIDTH) + (NSA_KV_WIDTH,) * 6 + (3 * NSA_Q_HEADS, D, D)
    offs = np.concatenate([[0], np.cumsum(sizes)])
    part = lambda i, j: w_in[:, offs[i]:offs[j]]
    mw, ma, mg = mu_wag[0][:, None], mu_wag[1][:, None], mu_wag[2][:, None]
    zg = jnp.zeros((D, RW_GATE_PAD - RW_GATE_LORA), F32)
    rw = jnp.concatenate([
        part(0, 3),
        (1.0 - mw) * w1, (1.0 - ma) * a1,
        mw * w1, ma * a1,
        (1.0 - mg) * g1, zg,
        mg * g1, zg], axis=1)
    def widen(w):
        w = w.reshape(D, NSA_KV_HEADS, HEAD_DIM)
        return jnp.concatenate([w, jnp.zeros_like(w)], axis=2).reshape(D, KEY_COLS)

    w_rows = jnp.concatenate([rw, part(4, 6), widen(part(6, 7)), widen(part(8, 9)), part(11, 13)], axis=1)
    w_cols = jnp.concatenate([part(3, 4), part(7, 8), part(9, 10), part(10, 11),
                              jnp.zeros((D, NSA_GATE_ROWS - 3 * NSA_Q_HEADS), F32)], axis=1)
    return w_rows.astype(BF16), w_cols.T.astype(BF16)


def _prep_rwkv_weights(w2, a2, g2):
    z = jnp.zeros_like(w2)
    w2a2 = jnp.concatenate([jnp.concatenate([w2, z], axis=1),
                            jnp.concatenate([z, a2], axis=1)], axis=0).astype(BF16)
    g2p = jnp.concatenate([g2, jnp.zeros((RW_GATE_PAD - RW_GATE_LORA, RW_WIDTH), F32)],
                          axis=0).astype(BF16)
    return w2a2, g2p


def _rwkv_from_proj(rw3, P):
    w2a2, g2p = _prep_rwkv_weights(P['rw_w2'][0], P['rw_a2'][0], P['rw_g2'][0])
    vecs = jnp.stack([P['rw_w0'][0], P['rw_a0'][0], P['rw_k_k'][0], P['rw_k_a'][0],
                      P['rw_r_k'][0].reshape(-1), P['rw_lnx_g'][0], P['rw_lnx_b'][0],
                      jnp.zeros((RW_WIDTH,), F32)], axis=0)
    return _rwkv_call(rw3, P['rw_mu_rkv'][0], vecs, w2a2, g2p)


def kernel(x, p, ln1_g, w_in, rw_mu_rkv, rw_mu_wag, rw_w0, rw_w1, rw_w2, rw_a0, rw_a1, rw_a2, rw_g1, rw_g2, rw_k_k, rw_k_a, rw_r_k, rw_lnx_g, rw_lnx_b, nsa_cmp_pos, nsa_cmp_w1, nsa_cmp_w2, w_out_a, w_out_b, w_out, ln2_g, w_up, conv_w, conv_b, w_down, ln3_g, w_ple_gate, w_ple_proj, ln_f_g):
    B, S, D = x.shape
    T = B * S
    assert w_in.shape[0] == 1, "single-layer block"
    P = dict(rw_mu_rkv=rw_mu_rkv, rw_w0=rw_w0, rw_w2=rw_w2, rw_a0=rw_a0, rw_a2=rw_a2, rw_g2=rw_g2,
             rw_k_k=rw_k_k, rw_k_a=rw_k_a, rw_r_k=rw_r_k, rw_lnx_g=rw_lnx_g, rw_lnx_b=rw_lnx_b,
             nsa_cmp_pos=nsa_cmp_pos, nsa_cmp_w1=nsa_cmp_w1, nsa_cmp_w2=nsa_cmp_w2)
    h = x.reshape(T, D)
    w_rows, w_cols = _prep_proj_weights(w_in[0], rw_mu_wag[0], rw_w1[0], rw_a1[0], rw_g1[0])
    rw, kvc, ksa, kwa, gates, qt, vst, vwt, gt = _proj_call(h, ln1_g[0][None], w_rows, w_cols, S)
    ya = _rwkv_from_proj(rw.reshape(B, S, RW_COLS), P)
    ybt = _nsa_from_proj(kvc, ksa, kwa, qt, vst, vwt, gt, P)
    h1 = _merge_call(h, ya.reshape(T, RW_WIDTH), ybt, gates,
                     w_out_a[0].astype(BF16), w_out_b[0].astype(BF16), w_out[0].astype(BF16))
    lns = jnp.stack([ln2_g[0], ln3_g[0], ln_f_g], axis=0)
    out = _ffn_call(h1, p[0].reshape(T, -1), lns, w_up[0].astype(BF16), conv_w[0], conv_b[0][None],
                    w_down[0].astype(BF16), w_ple_gate[0].astype(BF16), w_ple_proj[0].astype(BF16), S)
    return out.reshape(B, S, D)
```
